```python
import math
import jax, jax.numpy as jnp
from jax import lax
import numpy as np

D_MODEL = 1024
BATCH = 4
SEQ = 8192
DEPTH = 1

N_Q_HEADS = 8
N_KV_HEADS = 2
GROUP = N_Q_HEADS // N_KV_HEADS
HEAD_DIM = 64
WINDOW = 128
BLOCK = 128
Q_W = N_Q_HEADS * HEAD_DIM
KV_W = N_KV_HEADS * HEAD_DIM
N_BUCKETS = 32
MAX_DISTANCE = 128
GMLP_GROUPS = 8
GMLP_GROUP_DIM = 64
GMLP_WIDTH = GMLP_GROUPS * GMLP_GROUP_DIM
CHUNK = 128
N_BRANCHES = 2
IN_W = Q_W + 2 * KV_W + 2 * GMLP_WIDTH + N_BRANCHES * D_MODEL
D_FF = 2816
FFN_RES = 0.5
N_SUB = 3
N_ADA = 3 * N_SUB
EPS = 1e-6
NEG = -1e30

kernel_name = "hybrid_swa_sink_gmlp_macaron_block"


def rmsnorm(x, g):
    xf = x.astype(jnp.float32)
    r = lax.rsqrt(jnp.mean(xf * xf, axis=-1, keepdims=True) + EPS)
    return (xf * r * g.astype(jnp.float32)).astype(x.dtype)


def layernorm(x, g, b):
    xf = x.astype(jnp.float32)
    mu = jnp.mean(xf, axis=-1, keepdims=True)
    var = jnp.mean(jnp.square(xf - mu), axis=-1, keepdims=True)
    return ((xf - mu) * lax.rsqrt(var + EPS) * g + b).astype(x.dtype)


def modulate(x, shift, scale):
    return x * (1.0 + scale[:, None, :]) + shift[:, None, :]


def swiglu(x, w_in, w_out):
    gu = x @ w_in
    gate, up = jnp.split(gu, 2, axis=-1)
    return (jax.nn.silu(gate) * up) @ w_out


def t5_causal_bucket(dist):
    max_exact = N_BUCKETS // 2
    d_f = jnp.maximum(dist, max_exact).astype(jnp.float32)
    large = max_exact + (jnp.log(d_f / max_exact) / math.log(MAX_DISTANCE / max_exact)
                         * (N_BUCKETS - max_exact)).astype(jnp.int32)
    large = jnp.minimum(large, N_BUCKETS - 1)
    return jnp.where(dist < max_exact, dist, large)


def banded_sink_attention(q, k, v, sinks, rel_table):
    B, S, _ = q.shape
    nb = S // BLOCK
    q = q.reshape(B, nb, BLOCK, N_KV_HEADS, GROUP, HEAD_DIM)
    k = k.reshape(B, nb, BLOCK, N_KV_HEADS, HEAD_DIM)
    v = v.reshape(B, nb, BLOCK, N_KV_HEADS, HEAD_DIM)
    pad = ((0, 0), (1, 0), (0, 0), (0, 0), (0, 0))
    kk = jnp.concatenate([jnp.pad(k, pad)[:, :-1], k], axis=2)
    vv = jnp.concatenate([jnp.pad(v, pad)[:, :-1], v], axis=2)

    qi = jnp.arange(BLOCK, dtype=jnp.int32)[:, None]
    kj = jnp.arange(2 * BLOCK, dtype=jnp.int32)[None, :]
    dist = qi + BLOCK - kj
    in_window = (dist >= 0) & (dist < WINDOW)
    blk = jnp.arange(nb, dtype=jnp.int32)[:, None, None]
    key_pos = (blk - 1) * BLOCK + kj[None]
    mask = in_window[None] & (key_pos >= 0)
    bias = rel_table.astype(jnp.float32)[t5_causal_bucket(jnp.maximum(dist, 0))]
    bias = jnp.transpose(bias, (2, 0, 1)).reshape(N_KV_HEADS, GROUP, BLOCK, 2 * BLOCK)

    s = jnp.einsum('bnqkgd,bnskd->bnkgqs', q, kk,
                   preferred_element_type=jnp.float32) * (HEAD_DIM ** -0.5)
    s = s + bias[None, None]
    s = jnp.where(mask[None, :, None, None], s, NEG)
    sink = sinks.astype(jnp.float32).reshape(1, 1, N_KV_HEADS, GROUP, 1, 1)
    m = jnp.maximum(jnp.max(s, axis=-1, keepdims=True), sink)
    p = jnp.exp(s - m)
    denom = jnp.sum(p, axis=-1, keepdims=True) + jnp.exp(sink - m)
    o = jnp.einsum('bnkgqs,bnskd->bnqkgd', (p / denom).astype(vv.dtype), vv,
                   preferred_element_type=jnp.float32)
    return o.astype(q.dtype).reshape(B, S, Q_W)


def chunked_spatial_gating(z, ln_g, ln_b, w_s, b_s):
    z = jax.nn.gelu(z)
    u, vg = jnp.split(z, 2, axis=-1)
    vg = layernorm(vg, ln_g, ln_b)
    B, S, _ = vg.shape
    nc = S // CHUNK
    vg = vg.reshape(B, nc, CHUNK, GMLP_GROUPS, GMLP_GROUP_DIM)
    causal = jnp.tril(jnp.ones((CHUNK, CHUNK), dtype=w_s.dtype))
    w = w_s * causal[None]
    sp = jnp.einsum('gts,bnsgc->bntgc', w, vg)
    sp = sp + jnp.transpose(b_s)[None, None, :, :, None]
    return u * sp.reshape(B, S, GMLP_WIDTH)


def setup_inputs(seed: int = 0) -> dict:
    key = jax.random.key(seed)
    ks = jax.random.split(key, 24)
    f32 = jnp.float32
    nrm = lambda k, shape, scale: jax.random.normal(k, shape, f32) * scale
    L = DEPTH
    return {
        "x": nrm(ks[0], (BATCH, SEQ, D_MODEL), 1.0),
        "c": nrm(ks[1], (BATCH, D_MODEL), 1.0),
        "rel_bias": nrm(ks[2], (N_BUCKETS, N_Q_HEADS), 0.5),
        "w_ada": nrm(ks[3], (L, D_MODEL, N_ADA * D_MODEL), 0.5 * D_MODEL ** -0.5),
        "b_ada": nrm(ks[4], (L, N_ADA * D_MODEL), 0.02),
        "pre_norm_g": 1.0 + nrm(ks[5], (L, N_SUB, D_MODEL), 0.05),
        "post_norm_g": 1.0 + nrm(ks[6], (L, N_SUB, D_MODEL), 0.05),
        "w_ffn1_in": nrm(ks[7], (L, D_MODEL, 2 * D_FF), D_MODEL ** -0.5),
        "w_ffn1_out": nrm(ks[8], (L, D_FF, D_MODEL), D_FF ** -0.5),
        "w_in": nrm(ks[9], (L, D_MODEL, IN_W), D_MODEL ** -0.5),
        "sinks": nrm(ks[10], (L, N_Q_HEADS), 1.0),
        "gmlp_ln_g": 1.0 + nrm(ks[11], (L, GMLP_WIDTH), 0.05),
        "gmlp_ln_b": nrm(ks[12], (L, GMLP_WIDTH), 0.02),
        "gmlp_w_s": nrm(ks[13], (L, GMLP_GROUPS, CHUNK, CHUNK), CHUNK ** -0.5),
        "gmlp_b_s": 1.0 + nrm(ks[14], (L, GMLP_GROUPS, CHUNK), 0.02),
        "w_br_attn": nrm(ks[15], (L, Q_W, D_MODEL), Q_W ** -0.5),
        "w_br_gmlp": nrm(ks[16], (L, GMLP_WIDTH, D_MODEL), GMLP_WIDTH ** -0.5),
        "w_out": nrm(ks[17], (L, D_MODEL, D_MODEL), D_MODEL ** -0.5),
        "w_ffn2_in": nrm(ks[18], (L, D_MODEL, 2 * D_FF), D_MODEL ** -0.5),
        "w_ffn2_out": nrm(ks[19], (L, D_FF, D_MODEL), D_FF ** -0.5),
    }


def reference(x, c, rel_bias, w_ada, b_ada, pre_norm_g, post_norm_g, w_ffn1_in, w_ffn1_out,
              w_in, sinks, gmlp_ln_g, gmlp_ln_b, gmlp_w_s, gmlp_b_s, w_br_attn, w_br_gmlp,
              w_out, w_ffn2_in, w_ffn2_out):
    h = x
    splits = np.cumsum([Q_W, KV_W, KV_W, 2 * GMLP_WIDTH]).tolist()
    for l in range(DEPTH):
        ada = jax.nn.silu(c) @ w_ada[l] + b_ada[l]
        sh1, sc1, g1, sh2, sc2, g2, sh3, sc3, g3 = jnp.split(ada, N_ADA, axis=-1)

        n = modulate(rmsnorm(h, pre_norm_g[l, 0]), sh1, sc1)
        y = swiglu(n, w_ffn1_in[l], w_ffn1_out[l])
        h = h + FFN_RES * g1[:, None, :] * rmsnorm(y, post_norm_g[l, 0])

        n = modulate(rmsnorm(h, pre_norm_g[l, 1]), sh2, sc2)
        z = n @ w_in[l]
        q, k, v, zg, zgate = jnp.split(z, splits, axis=-1)
        ya = banded_sink_attention(q, k, v, sinks[l], rel_bias) @ w_br_attn[l]
        yg = chunked_spatial_gating(zg, gmlp_ln_g[l], gmlp_ln_b[l],
                                    gmlp_w_s[l], gmlp_b_s[l]) @ w_br_gmlp[l]
        ga, gg = jnp.split(jax.nn.sigmoid(zgate), N_BRANCHES, axis=-1)
        y = (ga * ya + gg * yg) @ w_out[l]
        h = h + g2[:, None, :] * rmsnorm(y, post_norm_g[l, 1])

        n = modulate(rmsnorm(h, pre_norm_g[l, 2]), sh3, sc3)
        y = swiglu(n, w_ffn2_in[l], w_ffn2_out[l])
        h = h + FFN_RES * g3[:, None, :] * rmsnorm(y, post_norm_g[l, 2])
    return h
```

```python
import functools
import math

import numpy as np
import jax
import jax.numpy as jnp
from jax import lax
from jax.experimental import pallas as pl
from jax.experimental.pallas import tpu as pltpu

D_MODEL = 1024
N_Q_HEADS = 8
N_KV_HEADS = 2
GROUP = N_Q_HEADS // N_KV_HEADS
HEAD_DIM = 64
BLOCK = 128
Q_W = N_Q_HEADS * HEAD_DIM
KV_W = N_KV_HEADS * HEAD_DIM
N_BUCKETS = 32
MAX_DISTANCE = 128
GMLP_GROUPS = 8
GMLP_WIDTH = 512
D_FF = 2816
FFN_RES = 0.5
N_ADA = 9
EPS = 1e-6
NEG = -1e30

LANES = 128
VMEM_LIMIT_BYTES = 56 * 1024 * 1024
FFN_TOKEN_TILE = 512
MIX_TOKEN_TILE = 512
FFN_CHUNKS = (512, 512, 512, 512, 512, 256)

BF16 = jnp.bfloat16
F32 = jnp.float32


def _dot(a, b):
    return jnp.dot(a, b, preferred_element_type=F32)


def _dot_nt(a, b):
    return lax.dot_general(a, b, (((1,), (1,)), ((), ())), preferred_element_type=F32)


def _resident(shape):
    nd = len(shape)
    return pl.BlockSpec(shape, lambda i: (0,) * nd, pipeline_mode=pl.Buffered(1))


def _prenorm_mod(x, vec):
    r = lax.rsqrt(jnp.mean(x * x, axis=-1, keepdims=True) + EPS)
    gs = vec[3:4] * (1.0 + vec[1:2])
    return x * r * gs + vec[0:1]


def _postnorm_residual(x, y, vec, res):
    r = lax.rsqrt(jnp.mean(y * y, axis=-1, keepdims=True) + EPS)
    return x + (res * vec[2:3]) * (y * r * vec[4:5])


def _ada_kernel(c_ref, w_ref, b_ref, o_ref):
    c = c_ref[...]
    s = c / (1.0 + jnp.exp(-c))
    o_ref[...] = jnp.dot(s, w_ref[...], preferred_element_type=F32,
                         precision=lax.Precision.HIGHEST) + b_ref[...]


def _ada(c, w_ada, b_ada):
    b, d = c.shape
    n = w_ada.shape[1]
    rows = 8
    tn = n // 8
    c_pad = jnp.zeros((rows, d), F32).at[:b].set(c)
    out = pl.pallas_call(
        _ada_kernel,
        grid=(n // tn,),
        in_specs=[pl.BlockSpec((rows, d), lambda j: (0, 0)),
                  pl.BlockSpec((d, tn), lambda j: (0, j)),
                  pl.BlockSpec((1, tn), lambda j: (0, j))],
        out_specs=pl.BlockSpec((rows, tn), lambda j: (0, j)),
        out_shape=jax.ShapeDtypeStruct((rows, n), F32),
        compiler_params=pltpu.CompilerParams(dimension_semantics=("arbitrary",),
                                             vmem_limit_bytes=VMEM_LIMIT_BYTES),
        name="ada_proj",
    )(c_pad, w_ada, b_ada.reshape(1, n))
    return out[:b]


def _ffn_kernel(h_ref, vec_ref, w_in_ref, w_out_ref, o_ref, *, chunks, d_ff, res):
    x = h_ref[...]
    vec = vec_ref[0]
    n = _prenorm_mod(x, vec).astype(BF16)
    y = None
    off = 0
    for width in chunks:
        g = _dot(n, w_in_ref[:, off:off + width])
        u = _dot(n, w_in_ref[:, d_ff + off:d_ff + off + width])
        a = (g / (1.0 + jnp.exp(-g)) * u).astype(BF16)
        part = _dot(a, w_out_ref[off:off + width, :])
        y = part if y is None else y + part
        off += width
    o_ref[...] = _postnorm_residual(x, y, vec, res)


def _ffn(h, vec, w_in, w_out, seq_len):
    t, d = h.shape
    tm = min(FFN_TOKEN_TILE, seq_len)
    tiles_per_seq = seq_len // tm
    d_ff = w_out.shape[0]
    kern = functools.partial(_ffn_kernel, chunks=FFN_CHUNKS, d_ff=d_ff, res=FFN_RES)
    return pl.pallas_call(
        kern,
        grid=(t // tm,),
        in_specs=[pl.BlockSpec((tm, d), lambda i: (i, 0)),
                  pl.BlockSpec((1, 8, d), lambda i: (i // tiles_per_seq, 0, 0)),
                  _resident(w_in.shape),
                  _resident(w_out.shape)],
        out_specs=pl.BlockSpec((tm, d), lambda i: (i, 0)),
        out_shape=jax.ShapeDtypeStruct((t, d), F32),
        compiler_params=pltpu.CompilerParams(dimension_semantics=("arbitrary",),
                                             vmem_limit_bytes=VMEM_LIMIT_BYTES),
        name="ffn_half_step",
    )(h, vec, w_in, w_out)


def _mixer_kernel(h_ref, vec_ref, bias_ref, sink_ref, ln_ref, bsp_ref, ws_ref, w_in_ref,
                  wba_ref, wbg_ref, wo_ref, o_ref, carry_ref, attn_ref, sp_ref,
                  *, tm, tiles_per_seq):
    nb = tm // BLOCK
    first = (pl.program_id(0) % tiles_per_seq) == 0

    @pl.when(first)
    def _():
        carry_ref[...] = jnp.zeros_like(carry_ref)

    x = h_ref[...]
    vec = vec_ref[0]
    n = _prenorm_mod(x, vec).astype(BF16)

    lo = lax.broadcasted_iota(jnp.int32, (1, LANES), 1) < HEAD_DIM

    q = (_dot(n, w_in_ref[:, 0:Q_W]) * (HEAD_DIM ** -0.5)).astype(BF16)
    kv = _dot(n, w_in_ref[:, Q_W:Q_W + 2 * KV_W])
    k_f, v_f = kv[:, :KV_W], kv[:, KV_W:]
    k_cur, k_rot = k_f.astype(BF16), pltpu.roll(k_f, HEAD_DIM, 1).astype(BF16)
    v_cur, v_rot = v_f.astype(BF16), pltpu.roll(v_f, HEAD_DIM, 1).astype(BF16)
    k_all = jnp.concatenate([carry_ref[0], k_cur], axis=0)
    k_all_rot = jnp.concatenate([carry_ref[1], k_rot], axis=0)
    v_all = jnp.concatenate([carry_ref[2], v_cur], axis=0)
    v_all_rot = jnp.concatenate([carry_ref[3], v_rot], axis=0)
    carry_ref[0] = k_cur[tm - BLOCK:]
    carry_ref[1] = k_rot[tm - BLOCK:]
    carry_ref[2] = v_cur[tm - BLOCK:]
    carry_ref[3] = v_rot[tm - BLOCK:]
    zero = jnp.zeros((), BF16)
    k_var = ((jnp.where(lo, k_all, zero), jnp.where(lo, zero, k_all_rot)),
             (jnp.where(lo, k_all_rot, zero), jnp.where(lo, zero, k_all)))
    v_var = ((jnp.where(lo, v_all, zero), jnp.where(lo, zero, v_all_rot)),
             (jnp.where(lo, v_all_rot, zero), jnp.where(lo, zero, v_all)))

    col = lax.broadcasted_iota(jnp.int32, (1, 2 * BLOCK), 1)
    no_prev = jnp.where(jnp.logical_and(first, col < BLOCK), NEG, 0.0)

    for b in range(nb):
        r0 = b * BLOCK
        keys = slice(r0, r0 + 2 * BLOCK)
        for kvh in range(N_KV_HEADS):
            c0 = kvh * GROUP * HEAD_DIM
            q2 = jnp.concatenate([q[r0:r0 + BLOCK, c0:c0 + LANES],
                                  q[r0:r0 + BLOCK, c0 + LANES:c0 + 2 * LANES]], axis=0)
            probs = []
            inv = []
            for par in range(2):
                s = _dot_nt(q2, k_var[kvh][par][keys]) + bias_ref[2 * kvh + par]
                if b == 0:
                    s = s + no_prev
                sink = sink_ref[2 * kvh + par]
                m = jnp.maximum(jnp.max(s, axis=-1, keepdims=True), sink)
                p = jnp.exp(s - m)
                denom = jnp.sum(p, axis=-1, keepdims=True) + jnp.exp(sink - m)
                probs.append(p.astype(BF16))
                inv.append(1.0 / denom)
            p2 = jnp.concatenate(probs, axis=1)
            v2 = jnp.concatenate([v_var[kvh][0][keys], v_var[kvh][1][keys]], axis=0)
            o2 = _dot(p2, v2) * jnp.where(lo, inv[0], inv[1])
            attn_ref[r0:r0 + BLOCK, c0:c0 + LANES] = o2[:BLOCK].astype(BF16)
            attn_ref[r0:r0 + BLOCK, c0 + LANES:c0 + 2 * LANES] = o2[BLOCK:].astype(BF16)

    zg = _dot(n, w_in_ref[:, Q_W + 2 * KV_W:Q_W + 2 * KV_W + 2 * GMLP_WIDTH])
    zg = 0.5 * zg * (1.0 + jnp.tanh(math.sqrt(2.0 / math.pi) * (zg + 0.044715 * (zg * zg * zg))))
    u, vr = zg[:, :GMLP_WIDTH], zg[:, GMLP_WIDTH:]
    ln = ln_ref[...]
    mu = jnp.mean(vr, axis=-1, keepdims=True)
    vc = vr - mu
    var = jnp.mean(vc * vc, axis=-1, keepdims=True)
    vg = (vc * lax.rsqrt(var + EPS) * ln[0:1] + ln[1:2]).astype(BF16)
    row = lax.broadcasted_iota(jnp.int32, (BLOCK, 2 * BLOCK), 0)
    col2 = lax.broadcasted_iota(jnp.int32, (BLOCK, 2 * BLOCK), 1)
    causal = jnp.where(col2 < BLOCK, col2, col2 - BLOCK) <= row
    for j in range(GMLP_GROUPS // 2):
        w_pair = jnp.where(causal, ws_ref[j], zero)
        pieces = [vg[c * BLOCK:(c + 1) * BLOCK, j * LANES:(j + 1) * LANES] for c in range(nb)]
        rhs = jnp.concatenate(
            [jnp.concatenate([jnp.where(lo, pc, zero) for pc in pieces], axis=1),
             jnp.concatenate([jnp.where(lo, zero, pc) for pc in pieces], axis=1)], axis=0)
        out = _dot(w_pair, rhs)
        for c in range(nb):
            sp_ref[c * BLOCK:(c + 1) * BLOCK, j * LANES:(j + 1) * LANES] = (
                out[:, c * LANES:(c + 1) * LANES])
    bsp = bsp_ref[...]
    gm = jnp.concatenate(
        [u[c * BLOCK:(c + 1) * BLOCK] * (sp_ref[c * BLOCK:(c + 1) * BLOCK, :] + bsp)
         for c in range(nb)], axis=0).astype(BF16)

    g0 = Q_W + 2 * KV_W + 2 * GMLP_WIDTH
    ga = _dot(n, w_in_ref[:, g0:g0 + D_MODEL])
    gg = _dot(n, w_in_ref[:, g0 + D_MODEL:g0 + 2 * D_MODEL])
    ya = _dot(attn_ref[...], wba_ref[...])
    yg = _dot(gm, wbg_ref[...])
    mix = (ya / (1.0 + jnp.exp(-ga)) + yg / (1.0 + jnp.exp(-gg))).astype(BF16)
    y = _dot(mix, wo_ref[...])
    o_ref[...] = _postnorm_residual(x, y, vec, 1.0)


def _mixer(h, vec, bias_t, sink_t, ln, bsp, ws_t, w_in, wba, wbg, wo, seq_len):
    t, d = h.shape
    tm = min(MIX_TOKEN_TILE, seq_len)
    tiles_per_seq = seq_len // tm
    kern = functools.partial(_mixer_kernel, tm=tm, tiles_per_seq=tiles_per_seq)
    return pl.pallas_call(
        kern,
        grid=(t // tm,),
        in_specs=[pl.BlockSpec((tm, d), lambda i: (i, 0)),
                  pl.BlockSpec((1, 8, d), lambda i: (i // tiles_per_seq, 0, 0)),
                  _resident(bias_t.shape), _resident(sink_t.shape), _resident(ln.shape),
                  _resident(bsp.shape), _resident(ws_t.shape), _resident(w_in.shape),
                  _resident(wba.shape), _resident(wbg.shape), _resident(wo.shape)],
        out_specs=pl.BlockSpec((tm, d), lambda i: (i, 0)),
        out_shape=jax.ShapeDtypeStruct((t, d), F32),
        scratch_shapes=[pltpu.VMEM((4, BLOCK, KV_W), BF16),
                        pltpu.VMEM((tm, Q_W), BF16),
                        pltpu.VMEM((tm, GMLP_WIDTH), F32)],
        compiler_params=pltpu.CompilerParams(dimension_semantics=("arbitrary",),
                                             vmem_limit_bytes=VMEM_LIMIT_BYTES),
        name="token_mixers",
    )(h, vec, bias_t, sink_t, ln, bsp, ws_t, w_in, wba, wbg, wo)


def _t5_bucket_table():
    qi = np.arange(BLOCK, dtype=np.int32)[:, None]
    kj = np.arange(2 * BLOCK, dtype=np.int32)[None, :]
    dist = qi + BLOCK - kj
    in_window = (dist >= 0) & (dist < BLOCK)
    dc = np.maximum(dist, 0)
    max_exact = N_BUCKETS // 2
    d_f = np.maximum(dc, max_exact).astype(np.float32)
    large = max_exact + (np.log(d_f / np.float32(max_exact)).astype(np.float32)
                         / np.float32(math.log(MAX_DISTANCE / max_exact))
                         * np.float32(N_BUCKETS - max_exact)).astype(np.int32)
    large = np.minimum(large, N_BUCKETS - 1)
    return np.where(dc < max_exact, dc, large).astype(np.int32), in_window


def _pair_rows(a):
    out = []
    for kvh in range(N_KV_HEADS):
        for par in range(2):
            out.append(jnp.concatenate([a[GROUP * kvh + par], a[GROUP * kvh + 2 + par]], axis=0))
    return jnp.stack(out)


def kernel(x, c, rel_bias, w_ada, b_ada, pre_norm_g, post_norm_g, w_ffn1_in, w_ffn1_out, w_in,
           sinks, gmlp_ln_g, gmlp_ln_b, gmlp_w_s, gmlp_b_s, w_br_attn, w_br_gmlp, w_out,
           w_ffn2_in, w_ffn2_out):
    bsz, seq, d = x.shape
    depth = w_ada.shape[0]
    bucket, in_window = _t5_bucket_table()
    h = x.reshape(bsz * seq, d)
    for l in range(depth):
        ada = _ada(c, w_ada[l], b_ada[l]).reshape(bsz, N_ADA, d)

        def vec_for(sub):
            g = jnp.broadcast_to(jnp.stack([pre_norm_g[l, sub], post_norm_g[l, sub]])[None],
                                 (bsz, 2, d))
            return jnp.concatenate([ada[:, 3 * sub:3 * sub + 3], g,
                                    jnp.zeros((bsz, 3, d), F32)], axis=1)

        bias = jnp.transpose(rel_bias.astype(F32)[bucket], (2, 0, 1))
        bias_t = _pair_rows(jnp.where(in_window[None], bias, NEG))
        sink_t = _pair_rows(jnp.broadcast_to(sinks[l].astype(F32)[:, None, None],
                                             (N_Q_HEADS, BLOCK, 1)))
        ln = jnp.stack([gmlp_ln_g[l], gmlp_ln_b[l]])
        bsp = jnp.repeat(jnp.transpose(gmlp_b_s[l]), GMLP_WIDTH // GMLP_GROUPS, axis=1)
        ws_t = jnp.concatenate([gmlp_w_s[l, 0::2], gmlp_w_s[l, 1::2]], axis=2).astype(BF16)

        h = _ffn(h, vec_for(0), w_ffn1_in[l].astype(BF16), w_ffn1_out[l].astype(BF16), seq)
        h = _mixer(h, vec_for(1), bias_t, sink_t, ln, bsp, ws_t, w_in[l].astype(BF16),
                   w_br_attn[l].astype(BF16), w_br_gmlp[l].astype(BF16), w_out[l].astype(BF16), seq)
        h = _ffn(h, vec_for(2), w_ffn2_in[l].astype(BF16), w_ffn2_out[l].astype(BF16), seq)
    return h.reshape(bsz, seq, d)
```

```python
import functools
import math

import numpy as np
import jax
import jax.numpy as jnp
from jax import lax
from jax.experimental import pallas as pl
from jax.experimental.pallas import tpu as pltpu

D_MODEL = 1024
N_Q_HEADS = 8
N_KV_HEADS = 2
GROUP = N_Q_HEADS // N_KV_HEADS
HEAD_DIM = 64
BLOCK = 128
Q_W = N_Q_HEADS * HEAD_DIM
KV_W = N_KV_HEADS * HEAD_DIM
N_BUCKETS = 32
MAX_DISTANCE = 128
GMLP_GROUPS = 8
GMLP_WIDTH = 512
D_FF = 2816
FFN_RES = 0.5
N_ADA = 9
EPS = 1e-6
NEG = -1e30

LANES = 128
VMEM_LIMIT_BYTES = 56 * 1024 * 1024
FFN_TOKEN_TILE = 512
MIX_TOKEN_TILE = 512
FFN_CHUNKS = (512, 512, 512, 512, 512, 256)

BF16 = jnp.bfloat16
F32 = jnp.float32


def _dot(a, b):
    return jnp.dot(a, b, preferred_element_type=F32)


def _dot_nt(a, b):
    return lax.dot_general(a, b, (((1,), (1,)), ((), ())), preferred_element_type=F32)


def _resident(shape):
    nd = len(shape)
    return pl.BlockSpec(shape, lambda i: (0,) * nd, pipeline_mode=pl.Buffered(1))


def _prenorm_mod(x, vec):
    r = lax.rsqrt(jnp.mean(x * x, axis=-1, keepdims=True) + EPS)
    gs = vec[3:4] * (1.0 + vec[1:2])
    return x * r * gs + vec[0:1]


def _postnorm_residual(x, y, vec, res):
    r = lax.rsqrt(jnp.mean(y * y, axis=-1, keepdims=True) + EPS)
    return x + (res * vec[2:3]) * (y * r * vec[4:5])


def _ada_kernel(c_ref, w_ref, b_ref, o_ref):
    c = c_ref[...]
    s = c / (1.0 + jnp.exp(-c))
    o_ref[...] = jnp.dot(s, w_ref[...], preferred_element_type=F32,
                         precision=lax.Precision.HIGHEST) + b_ref[...]


def _ada(c, w_ada, b_ada):
    b, d = c.shape
    n = w_ada.shape[1]
    rows = 8
    tn = n // 8
    c_pad = jnp.zeros((rows, d), F32).at[:b].set(c)
    out = pl.pallas_call(
        _ada_kernel,
        grid=(n // tn,),
        in_specs=[pl.BlockSpec((rows, d), lambda j: (0, 0)),
                  pl.BlockSpec((d, tn), lambda j: (0, j)),
                  pl.BlockSpec((1, tn), lambda j: (0, j))],
        out_specs=pl.BlockSpec((rows, tn), lambda j: (0, j)),
        out_shape=jax.ShapeDtypeStruct((rows, n), F32),
        compiler_params=pltpu.CompilerParams(dimension_semantics=("arbitrary",),
                                             vmem_limit_bytes=VMEM_LIMIT_BYTES),
        name="ada_proj",
    )(c_pad, w_ada, b_ada.reshape(1, n))
    return out[:b]


def _ffn_kernel(h_ref, vec_ref, w_in_ref, w_out_ref, o_ref, *, chunks, d_ff, res):
    x = h_ref[...]
    vec = vec_ref[0]
    n = _prenorm_mod(x, vec).astype(BF16)
    y = None
    off = 0
    for width in chunks:
        g = _dot(n, w_in_ref[:, off:off + width])
        u = _dot(n, w_in_ref[:, d_ff + off:d_ff + off + width])
        a = (g / (1.0 + jnp.exp(-g)) * u).astype(BF16)
        part = _dot(a, w_out_ref[off:off + width, :])
        y = part if y is None else y + part
        off += width
    o_ref[...] = _postnorm_residual(x, y, vec, res)


def _ffn(h, vec, w_in, w_out, seq_len):
    t, d = h.shape
    tm = min(FFN_TOKEN_TILE, seq_len)
    tiles_per_seq = seq_len // tm
    d_ff = w_out.shape[0]
    kern = functools.partial(_ffn_kernel, chunks=FFN_CHUNKS, d_ff=d_ff, res=FFN_RES)
    return pl.pallas_call(
        kern,
        grid=(t // tm,),
        in_specs=[pl.BlockSpec((tm, d), lambda i: (i, 0)),
                  pl.BlockSpec((1, 8, d), lambda i: (i // tiles_per_seq, 0, 0)),
                  _resident(w_in.shape),
                  _resident(w_out.shape)],
        out_specs=pl.BlockSpec((tm, d), lambda i: (i, 0)),
        out_shape=jax.ShapeDtypeStruct((t, d), F32),
        compiler_params=pltpu.CompilerParams(dimension_semantics=("arbitrary",),
                                             vmem_limit_bytes=VMEM_LIMIT_BYTES),
        name="ffn_half_step",
    )(h, vec, w_in, w_out)


def _mixer_kernel(rel_ref, h_ref, vec_ref, bucket_ref, sink_ref, ln_ref, bsp_ref, ws_ref, w_in_ref,
                  wba_ref, wbg_ref, wo_ref, o_ref, carry_ref, attn_ref, sp_ref, bias_ref,
                  *, tm, tiles_per_seq):
    nb = tm // BLOCK
    first = (pl.program_id(0) % tiles_per_seq) == 0

    @pl.when(pl.program_id(0) == 0)
    def _():
        bucket = bucket_ref[...]
        hits = [bucket == bk for bk in range(N_BUCKETS)]
        for hd in range(N_Q_HEADS):
            acc = jnp.full(bucket.shape, NEG, F32)
            for bk in range(N_BUCKETS):
                acc = jnp.where(hits[bk], rel_ref[bk, hd], acc)
            kvh, g = divmod(hd, GROUP)
            bias_ref[2 * kvh + g % 2, (g // 2) * BLOCK:(g // 2 + 1) * BLOCK, :] = acc

    @pl.when(first)
    def _():
        carry_ref[...] = jnp.zeros_like(carry_ref)

    x = h_ref[...]
    vec = vec_ref[0]
    n = _prenorm_mod(x, vec).astype(BF16)

    lo = lax.broadcasted_iota(jnp.int32, (1, LANES), 1) < HEAD_DIM

    q = (_dot(n, w_in_ref[:, 0:Q_W]) * (HEAD_DIM ** -0.5)).astype(BF16)
    kv = _dot(n, w_in_ref[:, Q_W:Q_W + 2 * KV_W])
    k_f, v_f = kv[:, :KV_W], kv[:, KV_W:]
    k_cur, k_rot = k_f.astype(BF16), pltpu.roll(k_f, HEAD_DIM, 1).astype(BF16)
    v_cur, v_rot = v_f.astype(BF16), pltpu.roll(v_f, HEAD_DIM, 1).astype(BF16)
    k_all = jnp.concatenate([carry_ref[0], k_cur], axis=0)
    k_all_rot = jnp.concatenate([carry_ref[1], k_rot], axis=0)
    v_all = jnp.concatenate([carry_ref[2], v_cur], axis=0)
    v_all_rot = jnp.concatenate([carry_ref[3], v_rot], axis=0)
    carry_ref[0] = k_cur[tm - BLOCK:]
    carry_ref[1] = k_rot[tm - BLOCK:]
    carry_ref[2] = v_cur[tm - BLOCK:]
    carry_ref[3] = v_rot[tm - BLOCK:]
    zero = jnp.zeros((), BF16)
    k_var = ((jnp.where(lo, k_all, zero), jnp.where(lo, zero, k_all_rot)),
             (jnp.where(lo, k_all_rot, zero), jnp.where(lo, zero, k_all)))
    v_var = ((jnp.where(lo, v_all, zero), jnp.where(lo, zero, v_all_rot)),
             (jnp.where(lo, v_all_rot, zero), jnp.where(lo, zero, v_all)))

    col = lax.broadcasted_iota(jnp.int32, (1, 2 * BLOCK), 1)
    no_prev = jnp.where(jnp.logical_and(first, col < BLOCK), NEG, 0.0)

    scores = []
    for b in range(nb):
        r0 = b * BLOCK
        keys = slice(r0, r0 + 2 * BLOCK)
        per_block = []
        for kvh in range(N_KV_HEADS):
            c0 = kvh * GROUP * HEAD_DIM
            q2 = jnp.concatenate([q[r0:r0 + BLOCK, c0:c0 + LANES],
                                  q[r0:r0 + BLOCK, c0 + LANES:c0 + 2 * LANES]], axis=0)
            for par in range(2):
                s = _dot_nt(q2, k_var[kvh][par][keys]) + bias_ref[2 * kvh + par]
                per_block.append(s + no_prev if b == 0 else s)
        scores.append(jnp.stack(per_block))
    sink = sink_ref[...]
    probs, invs = [], []
    for b in range(nb):
        s = scores[b]
        m = jnp.maximum(jnp.max(s, axis=-1, keepdims=True), sink)
        p = jnp.exp(s - m)
        denom = jnp.sum(p, axis=-1, keepdims=True) + jnp.exp(sink - m)
        probs.append(p.astype(BF16))
        invs.append(1.0 / denom)
    for b in range(nb):
        r0 = b * BLOCK
        keys = slice(r0, r0 + 2 * BLOCK)
        for kvh in range(N_KV_HEADS):
            c0 = kvh * GROUP * HEAD_DIM
            p2 = jnp.concatenate([probs[b][2 * kvh], probs[b][2 * kvh + 1]], axis=1)
            v2 = jnp.concatenate([v_var[kvh][0][keys], v_var[kvh][1][keys]], axis=0)
            o2 = _dot(p2, v2) * jnp.where(lo, invs[b][2 * kvh], invs[b][2 * kvh + 1])
            attn_ref[r0:r0 + BLOCK, c0:c0 + LANES] = o2[:BLOCK].astype(BF16)
            attn_ref[r0:r0 + BLOCK, c0 + LANES:c0 + 2 * LANES] = o2[BLOCK:].astype(BF16)

    zg = _dot(n, w_in_ref[:, Q_W + 2 * KV_W:Q_W + 2 * KV_W + 2 * GMLP_WIDTH])
    zg = 0.5 * zg * (1.0 + jnp.tanh(math.sqrt(2.0 / math.pi) * (zg + 0.044715 * (zg * zg * zg))))
    u, vr = zg[:, :GMLP_WIDTH], zg[:, GMLP_WIDTH:]
    ln = ln_ref[...]
    mu = jnp.mean(vr, axis=-1, keepdims=True)
    vc = vr - mu
    var = jnp.mean(vc * vc, axis=-1, keepdims=True)
    vg = (vc * lax.rsqrt(var + EPS) * ln[0:1] + ln[1:2]).astype(BF16)
    row = lax.broadcasted_iota(jnp.int32, (BLOCK, 2 * BLOCK), 0)
    col2 = lax.broadcasted_iota(jnp.int32, (BLOCK, 2 * BLOCK), 1)
    causal = jnp.where(col2 < BLOCK, col2, col2 - BLOCK) <= row
    for j in range(GMLP_GROUPS // 2):
        w_pair = jnp.where(causal, ws_ref[j], zero)
        pieces = [vg[c * BLOCK:(c + 1) * BLOCK, j * LANES:(j + 1) * LANES] for c in range(nb)]
        rhs = jnp.concatenate(
            [jnp.concatenate([jnp.where(lo, pc, zero) for pc in pieces], axis=1),
             jnp.concatenate([jnp.where(lo, zero, pc) for pc in pieces], axis=1)], axis=0)
        out = _dot(w_pair, rhs)
        for c in range(nb):
            sp_ref[c * BLOCK:(c + 1) * BLOCK, j * LANES:(j + 1) * LANES] = (
                out[:, c * LANES:(c + 1) * LANES])
    bsp = bsp_ref[...]
    gm = jnp.concatenate(
        [u[c * BLOCK:(c + 1) * BLOCK] * (sp_ref[c * BLOCK:(c + 1) * BLOCK, :] + bsp)
         for c in range(nb)], axis=0).astype(BF16)

    g0 = Q_W + 2 * KV_W + 2 * GMLP_WIDTH
    ga = _dot(n, w_in_ref[:, g0:g0 + D_MODEL])
    gg = _dot(n, w_in_ref[:, g0 + D_MODEL:g0 + 2 * D_MODEL])
    ya = _dot(attn_ref[...], wba_ref[...])
    yg = _dot(gm, wbg_ref[...])
    mix = (ya / (1.0 + jnp.exp(-ga)) + yg / (1.0 + jnp.exp(-gg))).astype(BF16)
    y = _dot(mix, wo_ref[...])
    o_ref[...] = _postnorm_residual(x, y, vec, 1.0)


def _mixer(rel, h, vec, bucket, sink_t, ln, bsp, ws_t, w_in, wba, wbg, wo, seq_len):
    t, d = h.shape
    tm = min(MIX_TOKEN_TILE, seq_len)
    tiles_per_seq = seq_len // tm
    kern = functools.partial(_mixer_kernel, tm=tm, tiles_per_seq=tiles_per_seq)
    return pl.pallas_call(
        kern,
        grid=(t // tm,),
        in_specs=[pl.BlockSpec(memory_space=pltpu.SMEM),
                  pl.BlockSpec((tm, d), lambda i: (i, 0)),
                  pl.BlockSpec((1, 8, d), lambda i: (i // tiles_per_seq, 0, 0)),
                  _resident(bucket.shape), _resident(sink_t.shape), _resident(ln.shape),
                  _resident(bsp.shape), _resident(ws_t.shape), _resident(w_in.shape),
                  _resident(wba.shape), _resident(wbg.shape), _resident(wo.shape)],
        out_specs=pl.BlockSpec((tm, d), lambda i: (i, 0)),
        out_shape=jax.ShapeDtypeStruct((t, d), F32),
        scratch_shapes=[pltpu.VMEM((4, BLOCK, KV_W), BF16),
                        pltpu.VMEM((tm, Q_W), BF16),
                        pltpu.VMEM((tm, GMLP_WIDTH), F32),
                        pltpu.VMEM((4, 2 * BLOCK, 2 * BLOCK), F32)],
        compiler_params=pltpu.CompilerParams(dimension_semantics=("arbitrary",),
                                             vmem_limit_bytes=VMEM_LIMIT_BYTES),
        name="token_mixers",
    )(rel, h, vec, bucket, sink_t, ln, bsp, ws_t, w_in, wba, wbg, wo)


def _t5_bucket_table():
    qi = np.arange(BLOCK, dtype=np.int32)[:, None]
    kj = np.arange(2 * BLOCK, dtype=np.int32)[None, :]
    dist = qi + BLOCK - kj
    in_window = (dist >= 0) & (dist < BLOCK)
    dc = np.maximum(dist, 0)
    max_exact = N_BUCKETS // 2
    d_f = np.maximum(dc, max_exact).astype(np.float32)
    large = max_exact + (np.log(d_f / np.float32(max_exact)).astype(np.float32)
                         / np.float32(math.log(MAX_DISTANCE / max_exact))
                         * np.float32(N_BUCKETS - max_exact)).astype(np.int32)
    large = np.minimum(large, N_BUCKETS - 1)
    return np.where(in_window, np.where(dc < max_exact, dc, large), -1).astype(np.int32)


def _pair_rows(a):
    out = []
    for kvh in range(N_KV_HEADS):
        for par in range(2):
            out.append(jnp.concatenate([a[GROUP * kvh + par], a[GROUP * kvh + 2 + par]], axis=0))
    return jnp.stack(out)


def kernel(x, c, rel_bias, w_ada, b_ada, pre_norm_g, post_norm_g, w_ffn1_in, w_ffn1_out, w_in,
           sinks, gmlp_ln_g, gmlp_ln_b, gmlp_w_s, gmlp_b_s, w_br_attn, w_br_gmlp, w_out,
           w_ffn2_in, w_ffn2_out):
    bsz, seq, d = x.shape
    depth = w_ada.shape[0]
    bucket = jnp.asarray(_t5_bucket_table())
    h = x.reshape(bsz * seq, d)
    for l in range(depth):
        ada = _ada(c, w_ada[l], b_ada[l]).reshape(bsz, N_ADA, d)

        def vec_for(sub):
            g = jnp.broadcast_to(jnp.stack([pre_norm_g[l, sub], post_norm_g[l, sub]])[None],
                                 (bsz, 2, d))
            return jnp.concatenate([ada[:, 3 * sub:3 * sub + 3], g,
                                    jnp.zeros((bsz, 3, d), F32)], axis=1)

        sink_t = _pair_rows(jnp.broadcast_to(sinks[l].astype(F32)[:, None, None],
                                             (N_Q_HEADS, BLOCK, 1)))
        ln = jnp.stack([gmlp_ln_g[l], gmlp_ln_b[l]])
        bsp = jnp.repeat(jnp.transpose(gmlp_b_s[l]), GMLP_WIDTH // GMLP_GROUPS, axis=1)
        ws_t = jnp.concatenate([gmlp_w_s[l, 0::2], gmlp_w_s[l, 1::2]], axis=2).astype(BF16)

        h = _ffn(h, vec_for(0), w_ffn1_in[l].astype(BF16), w_ffn1_out[l].astype(BF16), seq)
        h = _mixer(rel_bias.astype(F32), h, vec_for(1), bucket, sink_t, ln, bsp, ws_t,
                   w_in[l].astype(BF16),
                   w_br_attn[l].astype(BF16), w_br_gmlp[l].astype(BF16), w_out[l].astype(BF16), seq)
        h = _ffn(h, vec_for(2), w_ffn2_in[l].astype(BF16), w_ffn2_out[l].astype(BF16), seq)
    return h.reshape(bsz, seq, d)
```

```python
import functools
import math

import numpy as np
import jax
import jax.numpy as jnp
from jax import lax
from jax.experimental import pallas as pl
from jax.experimental.pallas import tpu as pltpu

D_MODEL = 1024
N_Q_HEADS = 8
N_KV_HEADS = 2
GROUP = N_Q_HEADS // N_KV_HEADS
HEAD_DIM = 64
BLOCK = 128
Q_W = N_Q_HEADS * HEAD_DIM
KV_W = N_KV_HEADS * HEAD_DIM
N_BUCKETS = 32
MAX_DISTANCE = 128
GMLP_GROUPS = 8
GMLP_WIDTH = 512
D_FF = 2816
FFN_RES = 0.5
N_ADA = 9
EPS = 1e-6
NEG = -1e30

LANES = 128
VMEM_LIMIT_BYTES = 56 * 1024 * 1024
FFN_TOKEN_TILE = 1024
FFN_SUB_TILE = 512
MIX_TOKEN_TILE = 512
FFN_CHUNKS = (512, 512, 512, 512, 512, 256)

BF16 = jnp.bfloat16
F32 = jnp.float32


def _dot(a, b):
    return jnp.dot(a, b, preferred_element_type=F32)


def _dot_nt(a, b):
    return lax.dot_general(a, b, (((1,), (1,)), ((), ())), preferred_element_type=F32)


def _resident(shape):
    nd = len(shape)
    return pl.BlockSpec(shape, lambda i: (0,) * nd, pipeline_mode=pl.Buffered(1))


def _prenorm_mod(x, vec):
    r = lax.rsqrt(jnp.mean(x * x, axis=-1, keepdims=True) + EPS)
    gs = vec[3:4] * (1.0 + vec[1:2])
    return x * r * gs + vec[0:1]


def _postnorm_residual(x, y, vec, res):
    r = lax.rsqrt(jnp.mean(y * y, axis=-1, keepdims=True) + EPS)
    return x + (res * vec[2:3]) * (y * r * vec[4:5])


def _ada_kernel(c_ref, w_ref, b_ref, o_ref):
    c = c_ref[...]
    s = c / (1.0 + jnp.exp(-c))
    o_ref[...] = jnp.dot(s, w_ref[...], preferred_element_type=F32,
                         precision=lax.Precision.HIGHEST) + b_ref[...]


def _ada(c, w_ada, b_ada):
    b, d = c.shape
    n = w_ada.shape[1]
    rows = 8
    tn = n // 8
    c_pad = jnp.zeros((rows, d), F32).at[:b].set(c)
    out = pl.pallas_call(
        _ada_kernel,
        grid=(n // tn,),
        in_specs=[pl.BlockSpec((rows, d), lambda j: (0, 0)),
                  pl.BlockSpec((d, tn), lambda j: (0, j)),
                  pl.BlockSpec((1, tn), lambda j: (0, j))],
        out_specs=pl.BlockSpec((rows, tn), lambda j: (0, j)),
        out_shape=jax.ShapeDtypeStruct((rows, n), F32),
        compiler_params=pltpu.CompilerParams(dimension_semantics=("arbitrary",),
                                             vmem_limit_bytes=VMEM_LIMIT_BYTES),
        name="ada_proj",
    )(c_pad, w_ada, b_ada.reshape(1, n))
    return out[:b]


def _ffn_kernel(h_ref, vec_ref, w_in_ref, w_out_ref, o_ref, *, sub, chunks, d_ff, res):
    vec = vec_ref[0]
    starts = list(range(0, h_ref.shape[0], sub))

    def prenorm(r0):
        return _prenorm_mod(h_ref[r0:r0 + sub, :], vec).astype(BF16)

    def finish(r0, y):
        o_ref[r0:r0 + sub, :] = _postnorm_residual(h_ref[r0:r0 + sub, :], y, vec, res)

    n = prenorm(starts[0])
    pending = None
    for idx, r0 in enumerate(starts):
        y = None
        off = 0
        n_next = None
        for ci, width in enumerate(chunks):
            g = _dot(n, w_in_ref[:, off:off + width])
            u = _dot(n, w_in_ref[:, d_ff + off:d_ff + off + width])
            a = (g / (1.0 + jnp.exp(-g)) * u).astype(BF16)
            part = _dot(a, w_out_ref[off:off + width, :])
            y = part if y is None else y + part
            off += width
            if ci == 0 and pending is not None:
                finish(*pending)
                pending = None
            if ci == 1 and idx + 1 < len(starts):
                n_next = prenorm(starts[idx + 1])
        pending = (r0, y)
        n = n_next
    finish(*pending)


def _ffn(h, vec, w_in, w_out, seq_len):
    t, d = h.shape
    tm = min(FFN_TOKEN_TILE, seq_len)
    tiles_per_seq = seq_len // tm
    d_ff = w_out.shape[0]
    kern = functools.partial(_ffn_kernel, sub=min(FFN_SUB_TILE, tm), chunks=FFN_CHUNKS, d_ff=d_ff,
                             res=FFN_RES)
    return pl.pallas_call(
        kern,
        grid=(t // tm,),
        in_specs=[pl.BlockSpec((tm, d), lambda i: (i, 0)),
                  pl.BlockSpec((1, 8, d), lambda i: (i // tiles_per_seq, 0, 0)),
                  _resident(w_in.shape),
                  _resident(w_out.shape)],
        out_specs=pl.BlockSpec((tm, d), lambda i: (i, 0)),
        out_shape=jax.ShapeDtypeStruct((t, d), F32),
        compiler_params=pltpu.CompilerParams(dimension_semantics=("arbitrary",),
                                             vmem_limit_bytes=VMEM_LIMIT_BYTES),
        name="ffn_half_step",
    )(h, vec, w_in, w_out)


def _mixer_kernel(rel_ref, h_ref, vec_ref, bucket_ref, sink_ref, ln_ref, bsp_ref, ws_ref, w_in_ref,
                  wba_ref, wbg_ref, wo_ref, o_ref, carry_ref, attn_ref, sp_ref, bias_ref,
                  *, tm, tiles_per_seq):
    nb = tm // BLOCK
    first = (pl.program_id(0) % tiles_per_seq) == 0

    @pl.when(pl.program_id(0) == 0)
    def _():
        bucket = bucket_ref[...]
        hits = [bucket == bk for bk in range(N_BUCKETS)]
        for hd in range(N_Q_HEADS):
            acc = jnp.full(bucket.shape, NEG, F32)
            for bk in range(N_BUCKETS):
                acc = jnp.where(hits[bk], rel_ref[bk, hd], acc)
            kvh, g = divmod(hd, GROUP)
            bias_ref[2 * kvh + g % 2, (g // 2) * BLOCK:(g // 2 + 1) * BLOCK, :] = acc

    @pl.when(first)
    def _():
        carry_ref[...] = jnp.zeros_like(carry_ref)

    x = h_ref[...]
    vec = vec_ref[0]
    n = _prenorm_mod(x, vec).astype(BF16)

    lo = lax.broadcasted_iota(jnp.int32, (1, LANES), 1) < HEAD_DIM

    q = (_dot(n, w_in_ref[:, 0:Q_W]) * (HEAD_DIM ** -0.5)).astype(BF16)
    kv = _dot(n, w_in_ref[:, Q_W:Q_W + 2 * KV_W])
    k_f, v_f = kv[:, :KV_W], kv[:, KV_W:]
    k_cur, k_rot = k_f.astype(BF16), pltpu.roll(k_f, HEAD_DIM, 1).astype(BF16)
    v_cur, v_rot = v_f.astype(BF16), pltpu.roll(v_f, HEAD_DIM, 1).astype(BF16)
    k_all = jnp.concatenate([carry_ref[0], k_cur], axis=0)
    k_all_rot = jnp.concatenate([carry_ref[1], k_rot], axis=0)
    v_all = jnp.concatenate([carry_ref[2], v_cur], axis=0)
    v_all_rot = jnp.concatenate([carry_ref[3], v_rot], axis=0)
    carry_ref[0] = k_cur[tm - BLOCK:]
    carry_ref[1] = k_rot[tm - BLOCK:]
    carry_ref[2] = v_cur[tm - BLOCK:]
    carry_ref[3] = v_rot[tm - BLOCK:]
    zero = jnp.zeros((), BF16)
    k_var = ((jnp.where(lo, k_all, zero), jnp.where(lo, zero, k_all_rot)),
             (jnp.where(lo, k_all_rot, zero), jnp.where(lo, zero, k_all)))
    v_var = ((jnp.where(lo, v_all, zero), jnp.where(lo, zero, v_all_rot)),
             (jnp.where(lo, v_all_rot, zero), jnp.where(lo, zero, v_all)))

    col = lax.broadcasted_iota(jnp.int32, (1, 2 * BLOCK), 1)
    no_prev = jnp.where(jnp.logical_and(first, col < BLOCK), NEG, 0.0)

    scores = []
    for b in range(nb):
        r0 = b * BLOCK
        keys = slice(r0, r0 + 2 * BLOCK)
        per_block = []
        for kvh in range(N_KV_HEADS):
            c0 = kvh * GROUP * HEAD_DIM
            q2 = jnp.concatenate([q[r0:r0 + BLOCK, c0:c0 + LANES],
                                  q[r0:r0 + BLOCK, c0 + LANES:c0 + 2 * LANES]], axis=0)
            for par in range(2):
                s = _dot_nt(q2, k_var[kvh][par][keys]) + bias_ref[2 * kvh + par]
                per_block.append(s + no_prev if b == 0 else s)
        scores.append(jnp.stack(per_block))
    sink = sink_ref[...]
    g0 = Q_W + 2 * KV_W + 2 * GMLP_WIDTH

    def softmax_block(b):
        s = scores[b]
        m = jnp.maximum(jnp.max(s, axis=-1, keepdims=True), sink)
        return jnp.exp(s - m).astype(BF16), jnp.exp(sink - m)

    soft = [None] * nb
    zg = _dot(n, w_in_ref[:, Q_W + 2 * KV_W:g0])
    for b in range(0, nb // 2):
        soft[b] = softmax_block(b)
    ga = _dot(n, w_in_ref[:, g0:g0 + D_MODEL])
    for b in range(nb // 2, nb):
        soft[b] = softmax_block(b)
    gg = _dot(n, w_in_ref[:, g0 + D_MODEL:g0 + 2 * D_MODEL])

    rows2 = lax.broadcasted_iota(jnp.int32, (4 * BLOCK, LANES), 0) < 2 * BLOCK
    ones_cols = jnp.where(rows2 == lo, 1.0, 0.0).astype(BF16)
    for b in range(nb):
        r0 = b * BLOCK
        keys = slice(r0, r0 + 2 * BLOCK)
        probs, sink_term = soft[b]
        for kvh in range(N_KV_HEADS):
            c0 = kvh * GROUP * HEAD_DIM
            p2 = jnp.concatenate([probs[2 * kvh], probs[2 * kvh + 1]], axis=1)
            v2 = jnp.concatenate([v_var[kvh][0][keys], v_var[kvh][1][keys]], axis=0)
            o2 = _dot(p2, jnp.concatenate([v2, ones_cols], axis=1))
            denom = o2[:, LANES:] + jnp.where(lo, sink_term[2 * kvh], sink_term[2 * kvh + 1])
            o2 = o2[:, :LANES] / denom
            attn_ref[r0:r0 + BLOCK, c0:c0 + LANES] = o2[:BLOCK].astype(BF16)
            attn_ref[r0:r0 + BLOCK, c0 + LANES:c0 + 2 * LANES] = o2[BLOCK:].astype(BF16)

    zg = 0.5 * zg * (1.0 + jnp.tanh(math.sqrt(2.0 / math.pi) * (zg + 0.044715 * (zg * zg * zg))))
    u, vr = zg[:, :GMLP_WIDTH], zg[:, GMLP_WIDTH:]
    ln = ln_ref[...]
    mu = jnp.mean(vr, axis=-1, keepdims=True)
    vc = vr - mu
    var = jnp.mean(vc * vc, axis=-1, keepdims=True)
    vg = (vc * lax.rsqrt(var + EPS) * ln[0:1] + ln[1:2]).astype(BF16)
    row = lax.broadcasted_iota(jnp.int32, (BLOCK, 2 * BLOCK), 0)
    col2 = lax.broadcasted_iota(jnp.int32, (BLOCK, 2 * BLOCK), 1)
    causal = jnp.where(col2 < BLOCK, col2, col2 - BLOCK) <= row
    for j in range(GMLP_GROUPS // 2):
        w_pair = jnp.where(causal, ws_ref[j], zero)
        pieces = [vg[c * BLOCK:(c + 1) * BLOCK, j * LANES:(j + 1) * LANES] for c in range(nb)]
        rhs = jnp.concatenate(
            [jnp.concatenate([jnp.where(lo, pc, zero) for pc in pieces], axis=1),
             jnp.concatenate([jnp.where(lo, zero, pc) for pc in pieces], axis=1)], axis=0)
        out = _dot(w_pair, rhs)
        for c in range(nb):
            sp_ref[c * BLOCK:(c + 1) * BLOCK, j * LANES:(j + 1) * LANES] = (
                out[:, c * LANES:(c + 1) * LANES])
    bsp = bsp_ref[...]
    gm = jnp.concatenate(
        [u[c * BLOCK:(c + 1) * BLOCK] * (sp_ref[c * BLOCK:(c + 1) * BLOCK, :] + bsp)
         for c in range(nb)], axis=0).astype(BF16)

    ya = _dot(attn_ref[...], wba_ref[...])
    yg = _dot(gm, wbg_ref[...])
    mix = (ya / (1.0 + jnp.exp(-ga)) + yg / (1.0 + jnp.exp(-gg))).astype(BF16)
    y = _dot(mix, wo_ref[...])
    o_ref[...] = _postnorm_residual(x, y, vec, 1.0)


def _mixer(rel, h, vec, bucket, sink_t, ln, bsp, ws_t, w_in, wba, wbg, wo, seq_len):
    t, d = h.shape
    tm = min(MIX_TOKEN_TILE, seq_len)
    tiles_per_seq = seq_len // tm
    kern = functools.partial(_mixer_kernel, tm=tm, tiles_per_seq=tiles_per_seq)
    return pl.pallas_call(
        kern,
        grid=(t // tm,),
        in_specs=[pl.BlockSpec(memory_space=pltpu.SMEM),
                  pl.BlockSpec((tm, d), lambda i: (i, 0)),
                  pl.BlockSpec((1, 8, d), lambda i: (i // tiles_per_seq, 0, 0)),
                  _resident(bucket.shape), _resident(sink_t.shape), _resident(ln.shape),
                  _resident(bsp.shape), _resident(ws_t.shape), _resident(w_in.shape),
                  _resident(wba.shape), _resident(wbg.shape), _resident(wo.shape)],
        out_specs=pl.BlockSpec((tm, d), lambda i: (i, 0)),
        out_shape=jax.ShapeDtypeStruct((t, d), F32),
        scratch_shapes=[pltpu.VMEM((4, BLOCK, KV_W), BF16),
                        pltpu.VMEM((tm, Q_W), BF16),
                        pltpu.VMEM((tm, GMLP_WIDTH), F32),
                        pltpu.VMEM((4, 2 * BLOCK, 2 * BLOCK), F32)],
        compiler_params=pltpu.CompilerParams(dimension_semantics=("arbitrary",),
                                             vmem_limit_bytes=VMEM_LIMIT_BYTES),
        name="token_mixers",
    )(rel, h, vec, bucket, sink_t, ln, bsp, ws_t, w_in, wba, wbg, wo)


def _t5_bucket_table():
    qi = np.arange(BLOCK, dtype=np.int32)[:, None]
    kj = np.arange(2 * BLOCK, dtype=np.int32)[None, :]
    dist = qi + BLOCK - kj
    in_window = (dist >= 0) & (dist < BLOCK)
    dc = np.maximum(dist, 0)
    max_exact = N_BUCKETS // 2
    d_f = np.maximum(dc, max_exact).astype(np.float32)
    large = max_exact + (np.log(d_f / np.float32(max_exact)).astype(np.float32)
                         / np.float32(math.log(MAX_DISTANCE / max_exact))
                         * np.float32(N_BUCKETS - max_exact)).astype(np.int32)
    large = np.minimum(large, N_BUCKETS - 1)
    return np.where(in_window, np.where(dc < max_exact, dc, large), -1).astype(np.int32)


def _pair_rows(a):
    out = []
    for kvh in range(N_KV_HEADS):
        for par in range(2):
            out.append(jnp.concatenate([a[GROUP * kvh + par], a[GROUP * kvh + 2 + par]], axis=0))
    return jnp.stack(out)


def kernel(x, c, rel_bias, w_ada, b_ada, pre_norm_g, post_norm_g, w_ffn1_in, w_ffn1_out, w_in,
           sinks, gmlp_ln_g, gmlp_ln_b, gmlp_w_s, gmlp_b_s, w_br_attn, w_br_gmlp, w_out,
           w_ffn2_in, w_ffn2_out):
    bsz, seq, d = x.shape
    depth = w_ada.shape[0]
    bucket = jnp.asarray(_t5_bucket_table())
    h = x.reshape(bsz * seq, d)
    for l in range(depth):
        ada = _ada(c, w_ada[l], b_ada[l]).reshape(bsz, N_ADA, d)

        def vec_for(sub):
            g = jnp.broadcast_to(jnp.stack([pre_norm_g[l, sub], post_norm_g[l, sub]])[None],
                                 (bsz, 2, d))
            return jnp.concatenate([ada[:, 3 * sub:3 * sub + 3], g,
                                    jnp.zeros((bsz, 3, d), F32)], axis=1)

        sink_t = _pair_rows(jnp.broadcast_to(sinks[l].astype(F32)[:, None, None],
                                             (N_Q_HEADS, BLOCK, 1)))
        ln = jnp.stack([gmlp_ln_g[l], gmlp_ln_b[l]])
        bsp = jnp.repeat(jnp.transpose(gmlp_b_s[l]), GMLP_WIDTH // GMLP_GROUPS, axis=1)
        ws_t = jnp.concatenate([gmlp_w_s[l, 0::2], gmlp_w_s[l, 1::2]], axis=2).astype(BF16)

        h = _ffn(h, vec_for(0), w_ffn1_in[l].astype(BF16), w_ffn1_out[l].astype(BF16), seq)
        h = _mixer(rel_bias.astype(F32), h, vec_for(1), bucket, sink_t, ln, bsp, ws_t,
                   w_in[l].astype(BF16),
                   w_br_attn[l].astype(BF16), w_br_gmlp[l].astype(BF16), w_out[l].astype(BF16), seq)
        h = _ffn(h, vec_for(2), w_ffn2_in[l].astype(BF16), w_ffn2_out[l].astype(BF16), seq)
    return h.reshape(bsz, seq, d)
```

```python
import functools
import math

import numpy as np
import jax
import jax.numpy as jnp
from jax import lax
from jax.experimental import pallas as pl
from jax.experimental.pallas import tpu as pltpu

D_MODEL = 1024
N_Q_HEADS = 8
N_KV_HEADS = 2
GROUP = N_Q_HEADS // N_KV_HEADS
HEAD_DIM = 64
BLOCK = 128
Q_W = N_Q_HEADS * HEAD_DIM
KV_W = N_KV_HEADS * HEAD_DIM
N_BUCKETS = 32
MAX_DISTANCE = 128
GMLP_GROUPS = 8
GMLP_WIDTH = 512
D_FF = 2816
FFN_RES = 0.5
N_ADA = 9
EPS = 1e-6
NEG = -1e30

LANES = 128
VMEM_LIMIT_BYTES = 56 * 1024 * 1024
FFN_TOKEN_TILE = 1024
FFN_SUB_TILE = 512
MIX_TOKEN_TILE = 1024
MIX_SUB_TILE = 512
MERGE_CHUNK = 512
FFN_CHUNKS = (512, 512, 512, 512, 512, 256)

BF16 = jnp.bfloat16
F32 = jnp.float32


def _dot(a, b):
    return jnp.dot(a, b, preferred_element_type=F32)


def _dot_nt(a, b):
    return lax.dot_general(a, b, (((1,), (1,)), ((), ())), preferred_element_type=F32)


def _resident(shape):
    nd = len(shape)
    return pl.BlockSpec(shape, lambda i: (0,) * nd, pipeline_mode=pl.Buffered(1))


def _prenorm_mod(x, vec):
    r = lax.rsqrt(jnp.mean(x * x, axis=-1, keepdims=True) + EPS)
    gs = vec[3:4] * (1.0 + vec[1:2])
    return x * r * gs + vec[0:1]


def _postnorm_residual(x, y, vec, res):
    r = lax.rsqrt(jnp.mean(y * y, axis=-1, keepdims=True) + EPS)
    return x + (res * vec[2:3]) * (y * r * vec[4:5])


def _ada_kernel(c_ref, w_ref, b_ref, o_ref):
    c = c_ref[...]
    s = c / (1.0 + jnp.exp(-c))
    o_ref[...] = jnp.dot(s, w_ref[...], preferred_element_type=F32,
                         precision=lax.Precision.HIGHEST) + b_ref[...]


def _ada(c, w_ada, b_ada):
    b, d = c.shape
    n = w_ada.shape[1]
    rows = 8
    tn = n // 8
    c_pad = jnp.zeros((rows, d), F32).at[:b].set(c)
    out = pl.pallas_call(
        _ada_kernel,
        grid=(n // tn,),
        in_specs=[pl.BlockSpec((rows, d), lambda j: (0, 0)),
                  pl.BlockSpec((d, tn), lambda j: (0, j)),
                  pl.BlockSpec((1, tn), lambda j: (0, j))],
        out_specs=pl.BlockSpec((rows, tn), lambda j: (0, j)),
        out_shape=jax.ShapeDtypeStruct((rows, n), F32),
        compiler_params=pltpu.CompilerParams(dimension_semantics=("arbitrary",),
                                             vmem_limit_bytes=VMEM_LIMIT_BYTES),
        name="ada_proj",
    )(c_pad, w_ada, b_ada.reshape(1, n))
    return out[:b]


def _ffn_kernel(h_ref, vec_ref, w_in_ref, w_out_ref, o_ref, *, sub, chunks, d_ff, res):
    vec = vec_ref[0]
    starts = list(range(0, h_ref.shape[0], sub))

    def prenorm(r0):
        return _prenorm_mod(h_ref[r0:r0 + sub, :], vec).astype(BF16)

    def finish(r0, y):
        o_ref[r0:r0 + sub, :] = _postnorm_residual(h_ref[r0:r0 + sub, :], y, vec, res)

    n = prenorm(starts[0])
    pending = None
    for idx, r0 in enumerate(starts):
        y = None
        off = 0
        n_next = None
        for ci, width in enumerate(chunks):
            g = _dot(n, w_in_ref[:, off:off + width])
            u = _dot(n, w_in_ref[:, d_ff + off:d_ff + off + width])
            a = (g / (1.0 + jnp.exp(-g)) * u).astype(BF16)
            part = _dot(a, w_out_ref[off:off + width, :])
            y = part if y is None else y + part
            off += width
            if ci == 0 and pending is not None:
                finish(*pending)
                pending = None
            if ci == 1 and idx + 1 < len(starts):
                n_next = prenorm(starts[idx + 1])
        pending = (r0, y)
        n = n_next
    finish(*pending)


def _ffn(h, vec, w_in, w_out, seq_len):
    t, d = h.shape
    tm = min(FFN_TOKEN_TILE, seq_len)
    tiles_per_seq = seq_len // tm
    d_ff = w_out.shape[0]
    kern = functools.partial(_ffn_kernel, sub=min(FFN_SUB_TILE, tm), chunks=FFN_CHUNKS, d_ff=d_ff,
                             res=FFN_RES)
    return pl.pallas_call(
        kern,
        grid=(t // tm,),
        in_specs=[pl.BlockSpec((tm, d), lambda i: (i, 0)),
                  pl.BlockSpec((1, 8, d), lambda i: (i // tiles_per_seq, 0, 0)),
                  _resident(w_in.shape),
                  _resident(w_out.shape)],
        out_specs=pl.BlockSpec((tm, d), lambda i: (i, 0)),
        out_shape=jax.ShapeDtypeStruct((t, d), F32),
        compiler_params=pltpu.CompilerParams(dimension_semantics=("arbitrary",),
                                             vmem_limit_bytes=VMEM_LIMIT_BYTES),
        name="ffn_half_step",
    )(h, vec, w_in, w_out)


def _mixer_kernel(rel_ref, h_ref, vec_ref, bucket_ref, sink_ref, ln_ref, bsp_ref, ws_ref, w_in_ref,
                  wba_ref, wbg_ref, wo_ref, o_ref, carry_ref, attn_ref, sp_ref, bias_ref,
                  *, tm, sub, tiles_per_seq):
    nsub = tm // sub
    nb = sub // BLOCK
    first = (pl.program_id(0) % tiles_per_seq) == 0

    @pl.when(pl.program_id(0) == 0)
    def _():
        bucket = bucket_ref[...]
        hits = [bucket == bk for bk in range(N_BUCKETS)]
        for hd in range(N_Q_HEADS):
            acc = jnp.full(bucket.shape, NEG, F32)
            for bk in range(N_BUCKETS):
                acc = jnp.where(hits[bk], rel_ref[bk, hd], acc)
            kvh, g = divmod(hd, GROUP)
            bias_ref[2 * kvh + g % 2, (g // 2) * BLOCK:(g // 2 + 1) * BLOCK, :] = acc

    @pl.when(first)
    def _():
        carry_ref[...] = jnp.zeros_like(carry_ref)

    vec = vec_ref[0]
    sink = sink_ref[...]
    ln = ln_ref[...]
    bsp = bsp_ref[...]
    zero = jnp.zeros((), BF16)
    lo = lax.broadcasted_iota(jnp.int32, (1, LANES), 1) < HEAD_DIM
    col = lax.broadcasted_iota(jnp.int32, (1, 2 * BLOCK), 1)
    no_prev = jnp.where(jnp.logical_and(first, col < BLOCK), NEG, 0.0)
    row = lax.broadcasted_iota(jnp.int32, (BLOCK, 2 * BLOCK), 0)
    col2 = lax.broadcasted_iota(jnp.int32, (BLOCK, 2 * BLOCK), 1)
    causal = jnp.where(col2 < BLOCK, col2, col2 - BLOCK) <= row
    rows2 = lax.broadcasted_iota(jnp.int32, (4 * BLOCK, LANES), 0) < 2 * BLOCK
    ones_cols = jnp.where(rows2 == lo, 1.0, 0.0).astype(BF16)
    g0 = Q_W + 2 * KV_W + 2 * GMLP_WIDTH
    st = [dict() for _ in range(nsub)]

    def project(i):
        t, r = st[i], i * sub
        w_kv = w_in_ref[:, Q_W:Q_W + 2 * KV_W]
        ns, qs, kvs = [], [], []
        for h0 in (0, sub // 2):
            nh = _prenorm_mod(h_ref[r + h0:r + h0 + sub // 2, :], vec).astype(BF16)
            qs.append((_dot(nh, w_in_ref[:, 0:Q_W]) * (HEAD_DIM ** -0.5)).astype(BF16))
            kvs.append(_dot(nh, w_kv))
            ns.append(nh)
        n = jnp.concatenate(ns, axis=0)
        q = jnp.concatenate(qs, axis=0)
        kv = jnp.concatenate(kvs, axis=0)
        k_f, v_f = kv[:, :KV_W], kv[:, KV_W:]
        cur = (k_f.astype(BF16), pltpu.roll(k_f, HEAD_DIM, 1).astype(BF16),
               v_f.astype(BF16), pltpu.roll(v_f, HEAD_DIM, 1).astype(BF16))
        prev = tuple(carry_ref[a] for a in range(4)) if i == 0 else st[i - 1]["last"]
        k_all, k_all_rot, v_all, v_all_rot = (
            jnp.concatenate([pv, cu], axis=0) for pv, cu in zip(prev, cur))
        t["last"] = tuple(cu[sub - BLOCK:] for cu in cur)
        if i == nsub - 1:
            for a in range(4):
                carry_ref[a] = t["last"][a]
        t["k_var"] = ((jnp.where(lo, k_all, zero), jnp.where(lo, zero, k_all_rot)),
                      (jnp.where(lo, k_all_rot, zero), jnp.where(lo, zero, k_all)))
        t["v_var"] = ((jnp.where(lo, v_all, zero), jnp.where(lo, zero, v_all_rot)),
                      (jnp.where(lo, v_all_rot, zero), jnp.where(lo, zero, v_all)))
        t["n"], t["q"] = n, q

    def score(i):
        t = st[i]
        scores = []
        for b in range(nb):
            r0 = b * BLOCK
            keys = slice(r0, r0 + 2 * BLOCK)
            per_block = []
            for kvh in range(N_KV_HEADS):
                c0 = kvh * GROUP * HEAD_DIM
                q2 = jnp.concatenate([t["q"][r0:r0 + BLOCK, c0:c0 + LANES],
                                      t["q"][r0:r0 + BLOCK, c0 + LANES:c0 + 2 * LANES]], axis=0)
                for par in range(2):
                    s = _dot_nt(q2, t["k_var"][kvh][par][keys]) + bias_ref[2 * kvh + par]
                    per_block.append(s + no_prev if (i == 0 and b == 0) else s)
            scores.append(per_block)
        t["scores"] = scores

    def softmax_piece(s, sk):
        m = jnp.max(s, axis=-1, keepdims=True)
        return jnp.exp(s - m).astype(BF16), jnp.exp(sk - m)

    def wide(i):
        t = st[i]
        half = D_MODEL // 2
        cols = [Q_W + 2 * KV_W + k * half for k in range(6)]
        todo = [(b, v) for b in range(nb) for v in range(4)]
        per_dot = -(-len(todo) // len(cols))
        outs, soft = [], {}
        for k, c0 in enumerate(cols):
            outs.append(_dot(t["n"], w_in_ref[:, c0:c0 + half]))
            for b, v in todo[k * per_dot:(k + 1) * per_dot]:
                soft[(b, v)] = softmax_piece(t["scores"][b][v], sink[v])
        t["zg"] = jnp.concatenate(outs[0:2], axis=1)
        t["sa"] = 1.0 / (1.0 + jnp.exp(-jnp.concatenate(outs[2:4], axis=1)))
        t["sg"] = 1.0 / (1.0 + jnp.exp(-jnp.concatenate(outs[4:6], axis=1)))
        t["soft"] = soft

    def branches(i):
        t, r = st[i], i * sub
        for b in range(nb):
            r0 = b * BLOCK
            keys = slice(r0, r0 + 2 * BLOCK)
            for kvh in range(N_KV_HEADS):
                c0 = kvh * GROUP * HEAD_DIM
                (pe, se), (po, so) = t["soft"][(b, 2 * kvh)], t["soft"][(b, 2 * kvh + 1)]
                p2 = jnp.concatenate([pe, po], axis=1)
                v2 = jnp.concatenate([t["v_var"][kvh][0][keys], t["v_var"][kvh][1][keys]], axis=0)
                o2 = _dot(p2, jnp.concatenate([v2, ones_cols], axis=1))
                denom = o2[:, LANES:] + jnp.where(lo, se, so)
                o2 = o2[:, :LANES] / denom
                attn_ref[r + r0:r + r0 + BLOCK, c0:c0 + LANES] = o2[:BLOCK].astype(BF16)
                attn_ref[r + r0:r + r0 + BLOCK, c0 + LANES:c0 + 2 * LANES] = o2[BLOCK:].astype(BF16)

        zg = t["zg"]
        zg = 0.5 * zg * (1.0 + jnp.tanh(math.sqrt(2.0 / math.pi) * (zg + 0.044715 * (zg * zg * zg))))
        u, vr = zg[:, :GMLP_WIDTH], zg[:, GMLP_WIDTH:]
        mu = jnp.mean(vr, axis=-1, keepdims=True)
        vc = vr - mu
        var = jnp.mean(vc * vc, axis=-1, keepdims=True)
        vg = (vc * lax.rsqrt(var + EPS) * ln[0:1] + ln[1:2]).astype(BF16)
        for j in range(GMLP_GROUPS // 2):
            w_pair = jnp.where(causal, ws_ref[j], zero)
            pieces = [vg[c * BLOCK:(c + 1) * BLOCK, j * LANES:(j + 1) * LANES] for c in range(nb)]
            rhs = jnp.concatenate(
                [jnp.concatenate([jnp.where(lo, pc, zero) for pc in pieces], axis=1),
                 jnp.concatenate([jnp.where(lo, zero, pc) for pc in pieces], axis=1)], axis=0)
            out = _dot(w_pair, rhs)
            for c in range(nb):
                sp_ref[r + c * BLOCK:r + (c + 1) * BLOCK, j * LANES:(j + 1) * LANES] = (
                    out[:, c * LANES:(c + 1) * LANES])
        t["gm"] = jnp.concatenate(
            [u[c * BLOCK:(c + 1) * BLOCK] * (sp_ref[r + c * BLOCK:r + (c + 1) * BLOCK, :] + bsp)
             for c in range(nb)], axis=0).astype(BF16)

    def merge(i):
        t, r = st[i], i * sub
        hs = sub // 2
        for h0 in (0, hs):
            attn = attn_ref[r + h0:r + h0 + hs, :]
            gm = t["gm"][h0:h0 + hs]
            y = None
            for k in range(0, D_MODEL, MERGE_CHUNK):
                ya = _dot(attn, wba_ref[:, k:k + MERGE_CHUNK])
                yg = _dot(gm, wbg_ref[:, k:k + MERGE_CHUNK])
                mix = (ya * t["sa"][h0:h0 + hs, k:k + MERGE_CHUNK]
                       + yg * t["sg"][h0:h0 + hs, k:k + MERGE_CHUNK]).astype(BF16)
                part = _dot(mix, wo_ref[k:k + MERGE_CHUNK, :])
                y = part if y is None else y + part
            o_ref[r + h0:r + h0 + hs, :] = _postnorm_residual(
                h_ref[r + h0:r + h0 + hs, :], y, vec, 1.0)

    stages = (project, score, wide, branches, merge)
    skew = 2
    for step in range(len(stages) + skew * (nsub - 1)):
        for i in reversed(range(nsub)):
            k = step - skew * i
            if 0 <= k < len(stages):
                stages[k](i)


def _mixer(rel, h, vec, bucket, sink_t, ln, bsp, ws_t, w_in, wba, wbg, wo, seq_len):
    t, d = h.shape
    tm = min(MIX_TOKEN_TILE, seq_len)
    tiles_per_seq = seq_len // tm
    kern = functools.partial(_mixer_kernel, tm=tm, sub=min(MIX_SUB_TILE, tm),
                             tiles_per_seq=tiles_per_seq)
    return pl.pallas_call(
        kern,
        grid=(t // tm,),
        in_specs=[pl.BlockSpec(memory_space=pltpu.SMEM),
                  pl.BlockSpec((tm, d), lambda i: (i, 0)),
                  pl.BlockSpec((1, 8, d), lambda i: (i // tiles_per_seq, 0, 0)),
                  _resident(bucket.shape), _resident(sink_t.shape), _resident(ln.shape),
                  _resident(bsp.shape), _resident(ws_t.shape), _resident(w_in.shape),
                  _resident(wba.shape), _resident(wbg.shape), _resident(wo.shape)],
        out_specs=pl.BlockSpec((tm, d), lambda i: (i, 0)),
        out_shape=jax.ShapeDtypeStruct((t, d), F32),
        scratch_shapes=[pltpu.VMEM((4, BLOCK, KV_W), BF16),
                        pltpu.VMEM((tm, Q_W), BF16),
                        pltpu.VMEM((tm, GMLP_WIDTH), F32),
                        pltpu.VMEM((4, 2 * BLOCK, 2 * BLOCK), F32)],
        compiler_params=pltpu.CompilerParams(dimension_semantics=("arbitrary",),
                                             vmem_limit_bytes=VMEM_LIMIT_BYTES),
        name="token_mixers",
    )(rel, h, vec, bucket, sink_t, ln, bsp, ws_t, w_in, wba, wbg, wo)


def _t5_bucket_table():
    qi = np.arange(BLOCK, dtype=np.int32)[:, None]
    kj = np.arange(2 * BLOCK, dtype=np.int32)[None, :]
    dist = qi + BLOCK - kj
    in_window = (dist >= 0) & (dist < BLOCK)
    dc = np.maximum(dist, 0)
    max_exact = N_BUCKETS // 2
    d_f = np.maximum(dc, max_exact).astype(np.float32)
    large = max_exact + (np.log(d_f / np.float32(max_exact)).astype(np.float32)
                         / np.float32(math.log(MAX_DISTANCE / max_exact))
                         * np.float32(N_BUCKETS - max_exact)).astype(np.int32)
    large = np.minimum(large, N_BUCKETS - 1)
    return np.where(in_window, np.where(dc < max_exact, dc, large), -1).astype(np.int32)


def _pair_rows(a):
    out = []
    for kvh in range(N_KV_HEADS):
        for par in range(2):
            out.append(jnp.concatenate([a[GROUP * kvh + par], a[GROUP * kvh + 2 + par]], axis=0))
    return jnp.stack(out)


def kernel(x, c, rel_bias, w_ada, b_ada, pre_norm_g, post_norm_g, w_ffn1_in, w_ffn1_out, w_in,
           sinks, gmlp_ln_g, gmlp_ln_b, gmlp_w_s, gmlp_b_s, w_br_attn, w_br_gmlp, w_out,
           w_ffn2_in, w_ffn2_out):
    bsz, seq, d = x.shape
    depth = w_ada.shape[0]
    bucket = jnp.asarray(_t5_bucket_table())
    h = x.reshape(bsz * seq, d)
    for l in range(depth):
        ada = _ada(c, w_ada[l], b_ada[l]).reshape(bsz, N_ADA, d)

        def vec_for(sub):
            g = jnp.broadcast_to(jnp.stack([pre_norm_g[l, sub], post_norm_g[l, sub]])[None],
                                 (bsz, 2, d))
            return jnp.concatenate([ada[:, 3 * sub:3 * sub + 3], g,
                                    jnp.zeros((bsz, 3, d), F32)], axis=1)

        sink_t = _pair_rows(jnp.broadcast_to(sinks[l].astype(F32)[:, None, None],
                                             (N_Q_HEADS, BLOCK, LANES)))
        ln = jnp.stack([gmlp_ln_g[l], gmlp_ln_b[l]])
        bsp = jnp.repeat(jnp.transpose(gmlp_b_s[l]), GMLP_WIDTH // GMLP_GROUPS, axis=1)
        ws_t = jnp.concatenate([gmlp_w_s[l, 0::2], gmlp_w_s[l, 1::2]], axis=2).astype(BF16)

        h = _ffn(h, vec_for(0), w_ffn1_in[l].astype(BF16), w_ffn1_out[l].astype(BF16), seq)
        h = _mixer(rel_bias.astype(F32), h, vec_for(1), bucket, sink_t, ln, bsp, ws_t,
                   w_in[l].astype(BF16),
                   w_br_attn[l].astype(BF16), w_br_gmlp[l].astype(BF16), w_out[l].astype(BF16), seq)
        h = _ffn(h, vec_for(2), w_ffn2_in[l].astype(BF16), w_ffn2_out[l].astype(BF16), seq)
    return h.reshape(bsz, seq, d)
```

```python
import functools
import math

import numpy as np
import jax
import jax.numpy as jnp
from jax import lax
from jax.experimental import pallas as pl
from jax.experimental.pallas import tpu as pltpu

D_MODEL = 1024
N_Q_HEADS = 8
N_KV_HEADS = 2
GROUP = N_Q_HEADS // N_KV_HEADS
HEAD_DIM = 64
BLOCK = 128
Q_W = N_Q_HEADS * HEAD_DIM
KV_W = N_KV_HEADS * HEAD_DIM
N_BUCKETS = 32
MAX_DISTANCE = 128
GMLP_GROUPS = 8
GMLP_WIDTH = 512
D_FF = 2816
FFN_RES = 0.5
N_ADA = 9
EPS = 1e-6
NEG = -1e30

LANES = 128
VMEM_LIMIT_BYTES = 56 * 1024 * 1024
FFN_TOKEN_TILE = 1024
FFN_SUB_TILE = 512
MIX_TOKEN_TILE = 512
MIX_SUB_TILE = 512
MERGE_CHUNK = 512
FFN_CHUNKS = (512, 512, 512, 512, 512, 256)

BF16 = jnp.bfloat16
F32 = jnp.float32


def _dot(a, b):
    return jnp.dot(a, b, preferred_element_type=F32)


def _dot_nt(a, b):
    return lax.dot_general(a, b, (((1,), (1,)), ((), ())), preferred_element_type=F32)


def _resident(shape):
    nd = len(shape)
    return pl.BlockSpec(shape, lambda i: (0,) * nd, pipeline_mode=pl.Buffered(1))


def _prenorm_mod(x, vec):
    r = lax.rsqrt(jnp.mean(x * x, axis=-1, keepdims=True) + EPS)
    gs = vec[3:4] * (1.0 + vec[1:2])
    return x * r * gs + vec[0:1]


def _postnorm_residual(x, y, vec, res):
    r = lax.rsqrt(jnp.mean(y * y, axis=-1, keepdims=True) + EPS)
    return x + (res * vec[2:3]) * (y * r * vec[4:5])


def _ada_kernel(c_ref, w_ref, b_ref, o_ref):
    c = c_ref[...]
    s = c / (1.0 + jnp.exp(-c))
    o_ref[...] = jnp.dot(s, w_ref[...], preferred_element_type=F32,
                         precision=lax.Precision.HIGHEST) + b_ref[...]


def _ada(c, w_ada, b_ada):
    b, d = c.shape
    n = w_ada.shape[1]
    rows = 8
    tn = n // 8
    c_pad = jnp.zeros((rows, d), F32).at[:b].set(c)
    out = pl.pallas_call(
        _ada_kernel,
        grid=(n // tn,),
        in_specs=[pl.BlockSpec((rows, d), lambda j: (0, 0)),
                  pl.BlockSpec((d, tn), lambda j: (0, j)),
                  pl.BlockSpec((1, tn), lambda j: (0, j))],
        out_specs=pl.BlockSpec((rows, tn), lambda j: (0, j)),
        out_shape=jax.ShapeDtypeStruct((rows, n), F32),
        compiler_params=pltpu.CompilerParams(dimension_semantics=("arbitrary",),
                                             vmem_limit_bytes=VMEM_LIMIT_BYTES),
        name="ada_proj",
    )(c_pad, w_ada, b_ada.reshape(1, n))
    return out[:b]


def _ffn_kernel(h_ref, vec_ref, w_in_ref, w_out_ref, o_ref, *, sub, chunks, d_ff, res):
    vec = vec_ref[0]
    starts = list(range(0, h_ref.shape[0], sub))

    def prenorm(r0):
        return _prenorm_mod(h_ref[r0:r0 + sub, :], vec).astype(BF16)

    def finish(r0, y):
        o_ref[r0:r0 + sub, :] = _postnorm_residual(h_ref[r0:r0 + sub, :], y, vec, res)

    n = prenorm(starts[0])
    pending = None
    for idx, r0 in enumerate(starts):
        y = None
        off = 0
        n_next = None
        for ci, width in enumerate(chunks):
            g = _dot(n, w_in_ref[:, off:off + width])
            u = _dot(n, w_in_ref[:, d_ff + off:d_ff + off + width])
            a = (g / (1.0 + jnp.exp(-g)) * u).astype(BF16)
            part = _dot(a, w_out_ref[off:off + width, :])
            y = part if y is None else y + part
            off += width
            if ci == 0 and pending is not None:
                finish(*pending)
                pending = None
            if ci == 1 and idx + 1 < len(starts):
                n_next = prenorm(starts[idx + 1])
        pending = (r0, y)
        n = n_next
    finish(*pending)


def _ffn(h, vec, w_in, w_out, seq_len):
    t, d = h.shape
    tm = min(FFN_TOKEN_TILE, seq_len)
    tiles_per_seq = seq_len // tm
    d_ff = w_out.shape[0]
    kern = functools.partial(_ffn_kernel, sub=min(FFN_SUB_TILE, tm), chunks=FFN_CHUNKS, d_ff=d_ff,
                             res=FFN_RES)
    return pl.pallas_call(
        kern,
        grid=(t // tm,),
        in_specs=[pl.BlockSpec((tm, d), lambda i: (i, 0)),
                  pl.BlockSpec((1, 8, d), lambda i: (i // tiles_per_seq, 0, 0)),
                  _resident(w_in.shape),
                  _resident(w_out.shape)],
        out_specs=pl.BlockSpec((tm, d), lambda i: (i, 0)),
        out_shape=jax.ShapeDtypeStruct((t, d), F32),
        compiler_params=pltpu.CompilerParams(dimension_semantics=("arbitrary",),
                                             vmem_limit_bytes=VMEM_LIMIT_BYTES),
        name="ffn_half_step",
    )(h, vec, w_in, w_out)


def _mixer_kernel(rel_ref, h_ref, vec_ref, bucket_ref, sink_ref, ln_ref, bsp_ref, ws_ref, w_in_ref,
                  wba_ref, wbg_ref, wo_ref, o_ref, carry_ref, attn_ref, sp_ref, bias_ref,
                  *, tm, sub, tiles_per_seq):
    nsub = tm // sub
    nb = sub // BLOCK
    first = (pl.program_id(0) % tiles_per_seq) == 0

    @pl.when(pl.program_id(0) == 0)
    def _():
        bucket = bucket_ref[...]
        hits = [bucket == bk for bk in range(N_BUCKETS)]
        for hd in range(N_Q_HEADS):
            acc = jnp.full(bucket.shape, NEG, F32)
            for bk in range(N_BUCKETS):
                acc = jnp.where(hits[bk], rel_ref[bk, hd], acc)
            kvh, g = divmod(hd, GROUP)
            bias_ref[2 * kvh + g % 2, (g // 2) * BLOCK:(g // 2 + 1) * BLOCK, :] = acc

    @pl.when(first)
    def _():
        carry_ref[...] = jnp.zeros_like(carry_ref)

    vec = vec_ref[0]
    sink = sink_ref[...]
    ln = ln_ref[...]
    bsp = bsp_ref[...]
    zero = jnp.zeros((), BF16)
    lo = lax.broadcasted_iota(jnp.int32, (1, LANES), 1) < HEAD_DIM
    col = lax.broadcasted_iota(jnp.int32, (1, 2 * BLOCK), 1)
    no_prev = jnp.where(jnp.logical_and(first, col < BLOCK), NEG, 0.0)
    row = lax.broadcasted_iota(jnp.int32, (BLOCK, 2 * BLOCK), 0)
    col2 = lax.broadcasted_iota(jnp.int32, (BLOCK, 2 * BLOCK), 1)
    causal = jnp.where(col2 < BLOCK, col2, col2 - BLOCK) <= row
    rows2 = lax.broadcasted_iota(jnp.int32, (4 * BLOCK, LANES), 0) < 2 * BLOCK
    ones_cols = jnp.where(rows2 == lo, 1.0, 0.0).astype(BF16)
    g0 = Q_W + 2 * KV_W + 2 * GMLP_WIDTH
    st = [dict() for _ in range(nsub)]

    def project(i):
        t, r = st[i], i * sub
        w_kv = w_in_ref[:, Q_W:Q_W + 2 * KV_W]
        ns, qs, kvs = [], [], []
        for h0 in (0, sub // 2):
            nh = _prenorm_mod(h_ref[r + h0:r + h0 + sub // 2, :], vec).astype(BF16)
            qs.append((_dot(nh, w_in_ref[:, 0:Q_W]) * (HEAD_DIM ** -0.5)).astype(BF16))
            kvs.append(_dot(nh, w_kv))
            ns.append(nh)
        n = jnp.concatenate(ns, axis=0)
        q = jnp.concatenate(qs, axis=0)
        kv = jnp.concatenate(kvs, axis=0)
        k_f, v_f = kv[:, :KV_W], kv[:, KV_W:]
        cur = (k_f.astype(BF16), pltpu.roll(k_f, HEAD_DIM, 1).astype(BF16),
               v_f.astype(BF16), pltpu.roll(v_f, HEAD_DIM, 1).astype(BF16))
        prev = tuple(carry_ref[a] for a in range(4)) if i == 0 else st[i - 1]["last"]
        k_all, k_all_rot, v_all, v_all_rot = (
            jnp.concatenate([pv, cu], axis=0) for pv, cu in zip(prev, cur))
        t["last"] = tuple(cu[sub - BLOCK:] for cu in cur)
        if i == nsub - 1:
            for a in range(4):
                carry_ref[a] = t["last"][a]
        t["k_var"] = ((jnp.where(lo, k_all, zero), jnp.where(lo, zero, k_all_rot)),
                      (jnp.where(lo, k_all_rot, zero), jnp.where(lo, zero, k_all)))
        t["v_var"] = ((jnp.where(lo, v_all, zero), jnp.where(lo, zero, v_all_rot)),
                      (jnp.where(lo, v_all_rot, zero), jnp.where(lo, zero, v_all)))
        t["n"], t["q"] = n, q

    def score(i):
        t = st[i]
        scores = []
        for b in range(nb):
            r0 = b * BLOCK
            keys = slice(r0, r0 + 2 * BLOCK)
            per_block = []
            for kvh in range(N_KV_HEADS):
                c0 = kvh * GROUP * HEAD_DIM
                q2 = jnp.concatenate([t["q"][r0:r0 + BLOCK, c0:c0 + LANES],
                                      t["q"][r0:r0 + BLOCK, c0 + LANES:c0 + 2 * LANES]], axis=0)
                for par in range(2):
                    s = _dot_nt(q2, t["k_var"][kvh][par][keys]) + bias_ref[2 * kvh + par]
                    per_block.append(s + no_prev if (i == 0 and b == 0) else s)
            scores.append(per_block)
        t["scores"] = scores

    def softmax_piece(s, sk):
        m = jnp.max(s, axis=-1, keepdims=True)
        return jnp.exp(s - m).astype(BF16), jnp.exp(sk - m)

    def wide(i):
        t = st[i]
        half = D_MODEL // 2
        cols = [Q_W + 2 * KV_W + k * half for k in range(6)]
        todo = [(b, v) for b in range(nb) for v in range(4)]
        per_dot = -(-len(todo) // len(cols))
        outs, soft = [], {}
        for k, c0 in enumerate(cols):
            outs.append(_dot(t["n"], w_in_ref[:, c0:c0 + half]))
            for b, v in todo[k * per_dot:(k + 1) * per_dot]:
                soft[(b, v)] = softmax_piece(t["scores"][b][v], sink[v])
        t["zg"] = jnp.concatenate(outs[0:2], axis=1)
        t["sa"] = 1.0 / (1.0 + jnp.exp(-jnp.concatenate(outs[2:4], axis=1)))
        t["sg"] = 1.0 / (1.0 + jnp.exp(-jnp.concatenate(outs[4:6], axis=1)))
        t["soft"] = soft

    def branches(i):
        t, r = st[i], i * sub
        for b in range(nb):
            r0 = b * BLOCK
            keys = slice(r0, r0 + 2 * BLOCK)
            for kvh in range(N_KV_HEADS):
                c0 = kvh * GROUP * HEAD_DIM
                (pe, se), (po, so) = t["soft"][(b, 2 * kvh)], t["soft"][(b, 2 * kvh + 1)]
                p2 = jnp.concatenate([pe, po], axis=1)
                v2 = jnp.concatenate([t["v_var"][kvh][0][keys], t["v_var"][kvh][1][keys]], axis=0)
                o2 = _dot(p2, jnp.concatenate([v2, ones_cols], axis=1))
                denom = o2[:, LANES:] + jnp.where(lo, se, so)
                o2 = o2[:, :LANES] / denom
                attn_ref[r + r0:r + r0 + BLOCK, c0:c0 + LANES] = o2[:BLOCK].astype(BF16)
                attn_ref[r + r0:r + r0 + BLOCK, c0 + LANES:c0 + 2 * LANES] = o2[BLOCK:].astype(BF16)

        zg = t["zg"]
        zg = 0.5 * zg * (1.0 + jnp.tanh(math.sqrt(2.0 / math.pi) * (zg + 0.044715 * (zg * zg * zg))))
        u, vr = zg[:, :GMLP_WIDTH], zg[:, GMLP_WIDTH:]
        mu = jnp.mean(vr, axis=-1, keepdims=True)
        vc = vr - mu
        var = jnp.mean(vc * vc, axis=-1, keepdims=True)
        vg = (vc * lax.rsqrt(var + EPS) * ln[0:1] + ln[1:2]).astype(BF16)
        for j in range(GMLP_GROUPS // 2):
            w_pair = jnp.where(causal, ws_ref[j], zero)
            pieces = [vg[c * BLOCK:(c + 1) * BLOCK, j * LANES:(j + 1) * LANES] for c in range(nb)]
            rhs = jnp.concatenate(
                [jnp.concatenate([jnp.where(lo, pc, zero) for pc in pieces], axis=1),
                 jnp.concatenate([jnp.where(lo, zero, pc) for pc in pieces], axis=1)], axis=0)
            out = _dot(w_pair, rhs)
            for c in range(nb):
                sp_ref[r + c * BLOCK:r + (c + 1) * BLOCK, j * LANES:(j + 1) * LANES] = (
                    out[:, c * LANES:(c + 1) * LANES])
        t["gm"] = jnp.concatenate(
            [u[c * BLOCK:(c + 1) * BLOCK] * (sp_ref[r + c * BLOCK:r + (c + 1) * BLOCK, :] + bsp)
             for c in range(nb)], axis=0).astype(BF16)

    def merge(i):
        t, r = st[i], i * sub
        hs = sub // 2
        for h0 in (0, hs):
            attn = attn_ref[r + h0:r + h0 + hs, :]
            gm = t["gm"][h0:h0 + hs]
            y = None
            for k in range(0, D_MODEL, MERGE_CHUNK):
                ya = _dot(attn, wba_ref[:, k:k + MERGE_CHUNK])
                yg = _dot(gm, wbg_ref[:, k:k + MERGE_CHUNK])
                mix = (ya * t["sa"][h0:h0 + hs, k:k + MERGE_CHUNK]
                       + yg * t["sg"][h0:h0 + hs, k:k + MERGE_CHUNK]).astype(BF16)
                part = _dot(mix, wo_ref[k:k + MERGE_CHUNK, :])
                y = part if y is None else y + part
            o_ref[r + h0:r + h0 + hs, :] = _postnorm_residual(
                h_ref[r + h0:r + h0 + hs, :], y, vec, 1.0)

    stages = (project, score, wide, branches, merge)
    skew = 2
    for step in range(len(stages) + skew * (nsub - 1)):
        for i in reversed(range(nsub)):
            k = step - skew * i
            if 0 <= k < len(stages):
                stages[k](i)


def _mixer(rel, h, vec, bucket, sink_t, ln, bsp, ws_t, w_in, wba, wbg, wo, seq_len):
    t, d = h.shape
    tm = min(MIX_TOKEN_TILE, seq_len)
    tiles_per_seq = seq_len // tm
    kern = functools.partial(_mixer_kernel, tm=tm, sub=min(MIX_SUB_TILE, tm),
                             tiles_per_seq=tiles_per_seq)
    return pl.pallas_call(
        kern,
        grid=(t // tm,),
        in_specs=[pl.BlockSpec(memory_space=pltpu.SMEM),
                  pl.BlockSpec((tm, d), lambda i: (i, 0)),
                  pl.BlockSpec((1, 8, d), lambda i: (i // tiles_per_seq, 0, 0)),
                  _resident(bucket.shape), _resident(sink_t.shape), _resident(ln.shape),
                  _resident(bsp.shape), _resident(ws_t.shape), _resident(w_in.shape),
                  _resident(wba.shape), _resident(wbg.shape), _resident(wo.shape)],
        out_specs=pl.BlockSpec((tm, d), lambda i: (i, 0)),
        out_shape=jax.ShapeDtypeStruct((t, d), F32),
        scratch_shapes=[pltpu.VMEM((4, BLOCK, KV_W), BF16),
                        pltpu.VMEM((tm, Q_W), BF16),
                        pltpu.VMEM((tm, GMLP_WIDTH), F32),
                        pltpu.VMEM((4, 2 * BLOCK, 2 * BLOCK), F32)],
        compiler_params=pltpu.CompilerParams(dimension_semantics=("arbitrary",),
                                             vmem_limit_bytes=VMEM_LIMIT_BYTES),
        name="token_mixers",
    )(rel, h, vec, bucket, sink_t, ln, bsp, ws_t, w_in, wba, wbg, wo)


def _t5_bucket_table():
    qi = np.arange(BLOCK, dtype=np.int32)[:, None]
    kj = np.arange(2 * BLOCK, dtype=np.int32)[None, :]
    dist = qi + BLOCK - kj
    in_window = (dist >= 0) & (dist < BLOCK)
    dc = np.maximum(dist, 0)
    max_exact = N_BUCKETS // 2
    d_f = np.maximum(dc, max_exact).astype(np.float32)
    large = max_exact + (np.log(d_f / np.float32(max_exact)).astype(np.float32)
                         / np.float32(math.log(MAX_DISTANCE / max_exact))
                         * np.float32(N_BUCKETS - max_exact)).astype(np.int32)
    large = np.minimum(large, N_BUCKETS - 1)
    return np.where(in_window, np.where(dc < max_exact, dc, large), -1).astype(np.int32)


def _pair_rows(a):
    out = []
    for kvh in range(N_KV_HEADS):
        for par in range(2):
            out.append(jnp.concatenate([a[GROUP * kvh + par], a[GROUP * kvh + 2 + par]], axis=0))
    return jnp.stack(out)


def kernel(x, c, rel_bias, w_ada, b_ada, pre_norm_g, post_norm_g, w_ffn1_in, w_ffn1_out, w_in,
           sinks, gmlp_ln_g, gmlp_ln_b, gmlp_w_s, gmlp_b_s, w_br_attn, w_br_gmlp, w_out,
           w_ffn2_in, w_ffn2_out):
    bsz, seq, d = x.shape
    depth = w_ada.shape[0]
    bucket = jnp.asarray(_t5_bucket_table())
    h = x.reshape(bsz * seq, d)
    for l in range(depth):
        ada = _ada(c, w_ada[l], b_ada[l]).reshape(bsz, N_ADA, d)

        def vec_for(sub):
            g = jnp.broadcast_to(jnp.stack([pre_norm_g[l, sub], post_norm_g[l, sub]])[None],
                                 (bsz, 2, d))
            return jnp.concatenate([ada[:, 3 * sub:3 * sub + 3], g,
                                    jnp.zeros((bsz, 3, d), F32)], axis=1)

        sink_t = _pair_rows(jnp.broadcast_to(sinks[l].astype(F32)[:, None, None],
                                             (N_Q_HEADS, BLOCK, LANES)))
        ln = jnp.stack([gmlp_ln_g[l], gmlp_ln_b[l]])
        bsp = jnp.repeat(jnp.transpose(gmlp_b_s[l]), GMLP_WIDTH // GMLP_GROUPS, axis=1)
        ws_t = jnp.concatenate([gmlp_w_s[l, 0::2], gmlp_w_s[l, 1::2]], axis=2).astype(BF16)

        h = _ffn(h, vec_for(0), w_ffn1_in[l].astype(BF16), w_ffn1_out[l].astype(BF16), seq)
        h = _mixer(rel_bias.astype(F32), h, vec_for(1), bucket, sink_t, ln, bsp, ws_t,
                   w_in[l].astype(BF16),
                   w_br_attn[l].astype(BF16), w_br_gmlp[l].astype(BF16), w_out[l].astype(BF16), seq)
        h = _ffn(h, vec_for(2), w_ffn2_in[l].astype(BF16), w_ffn2_out[l].astype(BF16), seq)
    return h.reshape(bsz, seq, d)
```

```python
import functools
import math

import numpy as np
import jax
import jax.numpy as jnp
from jax import lax
from jax.experimental import pallas as pl
from jax.experimental.pallas import tpu as pltpu

D_MODEL = 1024
N_Q_HEADS = 8
N_KV_HEADS = 2
GROUP = N_Q_HEADS // N_KV_HEADS
HEAD_DIM = 64
BLOCK = 128
Q_W = N_Q_HEADS * HEAD_DIM
KV_W = N_KV_HEADS * HEAD_DIM
N_BUCKETS = 32
MAX_DISTANCE = 128
GMLP_GROUPS = 8
GMLP_WIDTH = 512
D_FF = 2816
FFN_RES = 0.5
N_ADA = 9
EPS = 1e-6
NEG = -1e30

LANES = 128
VMEM_LIMIT_BYTES = 56 * 1024 * 1024
FFN_TOKEN_TILE = 1024
FFN_SUB_TILE = 512
MIX_TOKEN_TILE = 512
FFN_CHUNKS = (512, 512, 512, 512, 512, 256)

BF16 = jnp.bfloat16
F32 = jnp.float32


def _dot(a, b):
    return jnp.dot(a, b, preferred_element_type=F32)


def _dot_nt(a, b):
    return lax.dot_general(a, b, (((1,), (1,)), ((), ())), preferred_element_type=F32)


def _resident(shape):
    nd = len(shape)
    return pl.BlockSpec(shape, lambda i: (0,) * nd, pipeline_mode=pl.Buffered(1))


def _prenorm_mod(x, vec):
    r = lax.rsqrt(jnp.mean(x * x, axis=-1, keepdims=True) + EPS)
    gs = vec[3:4] * (1.0 + vec[1:2])
    return x * r * gs + vec[0:1]


def _postnorm_residual(x, y, vec, res):
    r = lax.rsqrt(jnp.mean(y * y, axis=-1, keepdims=True) + EPS)
    return x + (res * vec[2:3]) * (y * r * vec[4:5])


def _ada_kernel(c_ref, w_ref, b_ref, o_ref):
    c = c_ref[...]
    s = c / (1.0 + jnp.exp(-c))
    o_ref[...] = jnp.dot(s, w_ref[...], preferred_element_type=F32,
                         precision=lax.Precision.HIGHEST) + b_ref[...]


def _ada(c, w_ada, b_ada):
    b, d = c.shape
    n = w_ada.shape[1]
    rows = 8
    tn = n // 8
    c_pad = jnp.zeros((rows, d), F32).at[:b].set(c)
    out = pl.pallas_call(
        _ada_kernel,
        grid=(n // tn,),
        in_specs=[pl.BlockSpec((rows, d), lambda j: (0, 0)),
                  pl.BlockSpec((d, tn), lambda j: (0, j)),
                  pl.BlockSpec((1, tn), lambda j: (0, j))],
        out_specs=pl.BlockSpec((rows, tn), lambda j: (0, j)),
        out_shape=jax.ShapeDtypeStruct((rows, n), F32),
        compiler_params=pltpu.CompilerParams(dimension_semantics=("arbitrary",),
                                             vmem_limit_bytes=VMEM_LIMIT_BYTES),
        name="ada_proj",
    )(c_pad, w_ada, b_ada.reshape(1, n))
    return out[:b]


def _ffn_kernel(h_ref, vec_ref, w_in_ref, w_out_ref, o_ref, *, sub, chunks, d_ff, res):
    vec = vec_ref[0]
    starts = list(range(0, h_ref.shape[0], sub))

    def prenorm(r0):
        return _prenorm_mod(h_ref[r0:r0 + sub, :], vec).astype(BF16)

    def finish(r0, y):
        o_ref[r0:r0 + sub, :] = _postnorm_residual(h_ref[r0:r0 + sub, :], y, vec, res)

    n = prenorm(starts[0])
    pending = None
    for idx, r0 in enumerate(starts):
        y = None
        off = 0
        n_next = None
        for ci, width in enumerate(chunks):
            g = _dot(n, w_in_ref[:, off:off + width])
            u = _dot(n, w_in_ref[:, d_ff + off:d_ff + off + width])
            a = (g / (1.0 + jnp.exp(-g)) * u).astype(BF16)
            part = _dot(a, w_out_ref[off:off + width, :])
            y = part if y is None else y + part
            off += width
            if ci == 0 and pending is not None:
                finish(*pending)
                pending = None
            if ci == 1 and idx + 1 < len(starts):
                n_next = prenorm(starts[idx + 1])
        pending = (r0, y)
        n = n_next
    finish(*pending)


def _ffn(h, vec, w_in, w_out, seq_len):
    t, d = h.shape
    tm = min(FFN_TOKEN_TILE, seq_len)
    tiles_per_seq = seq_len // tm
    d_ff = w_out.shape[0]
    kern = functools.partial(_ffn_kernel, sub=min(FFN_SUB_TILE, tm), chunks=FFN_CHUNKS, d_ff=d_ff,
                             res=FFN_RES)
    return pl.pallas_call(
        kern,
        grid=(t // tm,),
        in_specs=[pl.BlockSpec((tm, d), lambda i: (i, 0)),
                  pl.BlockSpec((1, 8, d), lambda i: (i // tiles_per_seq, 0, 0)),
                  _resident(w_in.shape),
                  _resident(w_out.shape)],
        out_specs=pl.BlockSpec((tm, d), lambda i: (i, 0)),
        out_shape=jax.ShapeDtypeStruct((t, d), F32),
        compiler_params=pltpu.CompilerParams(dimension_semantics=("arbitrary",),
                                             vmem_limit_bytes=VMEM_LIMIT_BYTES),
        name="ffn_half_step",
    )(h, vec, w_in, w_out)


def _mixer_kernel(rel_ref, h_ref, vec_ref, bucket_ref, sink_ref, ln_ref, bsp_ref, ws_ref, w_in_ref,
                  wba_ref, wbg_ref, wo_ref, o_ref, carry_ref, attn_ref, sp_ref, bias_ref,
                  *, tm, tiles_per_seq):
    nb = tm // BLOCK
    first = (pl.program_id(0) % tiles_per_seq) == 0

    @pl.when(pl.program_id(0) == 0)
    def _():
        bucket = bucket_ref[...]
        hits = [bucket == bk for bk in range(N_BUCKETS)]
        for hd in range(N_Q_HEADS):
            acc = jnp.full(bucket.shape, NEG, F32)
            for bk in range(N_BUCKETS):
                acc = jnp.where(hits[bk], rel_ref[bk, hd], acc)
            kvh, g = divmod(hd, GROUP)
            bias_ref[2 * kvh + g % 2, (g // 2) * BLOCK:(g // 2 + 1) * BLOCK, :] = acc

    @pl.when(first)
    def _():
        carry_ref[...] = jnp.zeros_like(carry_ref)

    x = h_ref[...]
    vec = vec_ref[0]
    n = _prenorm_mod(x, vec).astype(BF16)

    lo = lax.broadcasted_iota(jnp.int32, (1, LANES), 1) < HEAD_DIM

    q = (_dot(n, w_in_ref[:, 0:Q_W]) * (HEAD_DIM ** -0.5)).astype(BF16)
    w_kv = w_in_ref[:, Q_W:Q_W + 2 * KV_W]
    kv = jnp.concatenate([_dot(n[:tm // 2], w_kv), _dot(n[tm // 2:], w_kv)], axis=0)
    k_f, v_f = kv[:, :KV_W], kv[:, KV_W:]
    k_cur, k_rot = k_f.astype(BF16), pltpu.roll(k_f, HEAD_DIM, 1).astype(BF16)
    v_cur, v_rot = v_f.astype(BF16), pltpu.roll(v_f, HEAD_DIM, 1).astype(BF16)
    k_all = jnp.concatenate([carry_ref[0], k_cur], axis=0)
    k_all_rot = jnp.concatenate([carry_ref[1], k_rot], axis=0)
    v_all = jnp.concatenate([carry_ref[2], v_cur], axis=0)
    v_all_rot = jnp.concatenate([carry_ref[3], v_rot], axis=0)
    carry_ref[0] = k_cur[tm - BLOCK:]
    carry_ref[1] = k_rot[tm - BLOCK:]
    carry_ref[2] = v_cur[tm - BLOCK:]
    carry_ref[3] = v_rot[tm - BLOCK:]
    zero = jnp.zeros((), BF16)
    k_var = ((jnp.where(lo, k_all, zero), jnp.where(lo, zero, k_all_rot)),
             (jnp.where(lo, k_all_rot, zero), jnp.where(lo, zero, k_all)))
    v_var = ((jnp.where(lo, v_all, zero), jnp.where(lo, zero, v_all_rot)),
             (jnp.where(lo, v_all_rot, zero), jnp.where(lo, zero, v_all)))

    col = lax.broadcasted_iota(jnp.int32, (1, 2 * BLOCK), 1)
    no_prev = jnp.where(jnp.logical_and(first, col < BLOCK), NEG, 0.0)

    scores = []
    for b in range(nb):
        r0 = b * BLOCK
        keys = slice(r0, r0 + 2 * BLOCK)
        per_block = []
        for kvh in range(N_KV_HEADS):
            c0 = kvh * GROUP * HEAD_DIM
            q2 = jnp.concatenate([q[r0:r0 + BLOCK, c0:c0 + LANES],
                                  q[r0:r0 + BLOCK, c0 + LANES:c0 + 2 * LANES]], axis=0)
            for par in range(2):
                s = _dot_nt(q2, k_var[kvh][par][keys]) + bias_ref[2 * kvh + par]
                per_block.append(s + no_prev if b == 0 else s)
        scores.append(jnp.stack(per_block))
    sink = sink_ref[...]
    g0 = Q_W + 2 * KV_W + 2 * GMLP_WIDTH

    def softmax_block(b):
        s = scores[b]
        m = jnp.max(s, axis=-1, keepdims=True)
        return jnp.exp(s - m).astype(BF16), jnp.exp(sink - m)

    soft = [None] * nb
    zg = _dot(n, w_in_ref[:, Q_W + 2 * KV_W:g0])
    for b in range(0, nb // 2):
        soft[b] = softmax_block(b)
    ga = _dot(n, w_in_ref[:, g0:g0 + D_MODEL])
    for b in range(nb // 2, nb):
        soft[b] = softmax_block(b)
    gg = _dot(n, w_in_ref[:, g0 + D_MODEL:g0 + 2 * D_MODEL])

    rows2 = lax.broadcasted_iota(jnp.int32, (4 * BLOCK, LANES), 0) < 2 * BLOCK
    ones_cols = jnp.where(rows2 == lo, 1.0, 0.0).astype(BF16)
    for b in range(nb):
        r0 = b * BLOCK
        keys = slice(r0, r0 + 2 * BLOCK)
        probs, sink_term = soft[b]
        for kvh in range(N_KV_HEADS):
            c0 = kvh * GROUP * HEAD_DIM
            p2 = jnp.concatenate([probs[2 * kvh], probs[2 * kvh + 1]], axis=1)
            v2 = jnp.concatenate([v_var[kvh][0][keys], v_var[kvh][1][keys]], axis=0)
            o2 = _dot(p2, jnp.concatenate([v2, ones_cols], axis=1))
            denom = o2[:, LANES:] + jnp.where(lo, sink_term[2 * kvh], sink_term[2 * kvh + 1])
            o2 = o2[:, :LANES] / denom
            attn_ref[r0:r0 + BLOCK, c0:c0 + LANES] = o2[:BLOCK].astype(BF16)
            attn_ref[r0:r0 + BLOCK, c0 + LANES:c0 + 2 * LANES] = o2[BLOCK:].astype(BF16)

    zg = 0.5 * zg * (1.0 + jnp.tanh(math.sqrt(2.0 / math.pi) * (zg + 0.044715 * (zg * zg * zg))))
    u, vr = zg[:, :GMLP_WIDTH], zg[:, GMLP_WIDTH:]
    ln = ln_ref[...]
    mu = jnp.mean(vr, axis=-1, keepdims=True)
    vc = vr - mu
    var = jnp.mean(vc * vc, axis=-1, keepdims=True)
    vg = (vc * lax.rsqrt(var + EPS) * ln[0:1] + ln[1:2]).astype(BF16)
    row = lax.broadcasted_iota(jnp.int32, (BLOCK, 2 * BLOCK), 0)
    col2 = lax.broadcasted_iota(jnp.int32, (BLOCK, 2 * BLOCK), 1)
    causal = jnp.where(col2 < BLOCK, col2, col2 - BLOCK) <= row
    for j in range(GMLP_GROUPS // 2):
        w_pair = jnp.where(causal, ws_ref[j], zero)
        pieces = [vg[c * BLOCK:(c + 1) * BLOCK, j * LANES:(j + 1) * LANES] for c in range(nb)]
        rhs = jnp.concatenate(
            [jnp.concatenate([jnp.where(lo, pc, zero) for pc in pieces], axis=1),
             jnp.concatenate([jnp.where(lo, zero, pc) for pc in pieces], axis=1)], axis=0)
        out = _dot(w_pair, rhs)
        for c in range(nb):
            sp_ref[c * BLOCK:(c + 1) * BLOCK, j * LANES:(j + 1) * LANES] = (
                out[:, c * LANES:(c + 1) * LANES])
    bsp = bsp_ref[...]
    gm = jnp.concatenate(
        [u[c * BLOCK:(c + 1) * BLOCK] * (sp_ref[c * BLOCK:(c + 1) * BLOCK, :] + bsp)
         for c in range(nb)], axis=0).astype(BF16)

    ya = _dot(attn_ref[...], wba_ref[...])
    yg = _dot(gm, wbg_ref[...])
    mix = (ya / (1.0 + jnp.exp(-ga)) + yg / (1.0 + jnp.exp(-gg))).astype(BF16)
    y = _dot(mix, wo_ref[...])
    o_ref[...] = _postnorm_residual(x, y, vec, 1.0)


def _mixer(rel, h, vec, bucket, sink_t, ln, bsp, ws_t, w_in, wba, wbg, wo, seq_len):
    t, d = h.shape
    tm = min(MIX_TOKEN_TILE, seq_len)
    tiles_per_seq = seq_len // tm
    kern = functools.partial(_mixer_kernel, tm=tm, tiles_per_seq=tiles_per_seq)
    return pl.pallas_call(
        kern,
        grid=(t // tm,),
        in_specs=[pl.BlockSpec(memory_space=pltpu.SMEM),
                  pl.BlockSpec((tm, d), lambda i: (i, 0)),
                  pl.BlockSpec((1, 8, d), lambda i: (i // tiles_per_seq, 0, 0)),
                  _resident(bucket.shape), _resident(sink_t.shape), _resident(ln.shape),
                  _resident(bsp.shape), _resident(ws_t.shape), _resident(w_in.shape),
                  _resident(wba.shape), _resident(wbg.shape), _resident(wo.shape)],
        out_specs=pl.BlockSpec((tm, d), lambda i: (i, 0)),
        out_shape=jax.ShapeDtypeStruct((t, d), F32),
        scratch_shapes=[pltpu.VMEM((4, BLOCK, KV_W), BF16),
                        pltpu.VMEM((tm, Q_W), BF16),
                        pltpu.VMEM((tm, GMLP_WIDTH), F32),
                        pltpu.VMEM((4, 2 * BLOCK, 2 * BLOCK), F32)],
        compiler_params=pltpu.CompilerParams(dimension_semantics=("arbitrary",),
                                             vmem_limit_bytes=VMEM_LIMIT_BYTES),
        name="token_mixers",
    )(rel, h, vec, bucket, sink_t, ln, bsp, ws_t, w_in, wba, wbg, wo)


def _t5_bucket_table():
    qi = np.arange(BLOCK, dtype=np.int32)[:, None]
    kj = np.arange(2 * BLOCK, dtype=np.int32)[None, :]
    dist = qi + BLOCK - kj
    in_window = (dist >= 0) & (dist < BLOCK)
    dc = np.maximum(dist, 0)
    max_exact = N_BUCKETS // 2
    d_f = np.maximum(dc, max_exact).astype(np.float32)
    large = max_exact + (np.log(d_f / np.float32(max_exact)).astype(np.float32)
                         / np.float32(math.log(MAX_DISTANCE / max_exact))
                         * np.float32(N_BUCKETS - max_exact)).astype(np.int32)
    large = np.minimum(large, N_BUCKETS - 1)
    return np.where(in_window, np.where(dc < max_exact, dc, large), -1).astype(np.int32)


def _pair_rows(a):
    out = []
    for kvh in range(N_KV_HEADS):
        for par in range(2):
            out.append(jnp.concatenate([a[GROUP * kvh + par], a[GROUP * kvh + 2 + par]], axis=0))
    return jnp.stack(out)


def kernel(x, c, rel_bias, w_ada, b_ada, pre_norm_g, post_norm_g, w_ffn1_in, w_ffn1_out, w_in,
           sinks, gmlp_ln_g, gmlp_ln_b, gmlp_w_s, gmlp_b_s, w_br_attn, w_br_gmlp, w_out,
           w_ffn2_in, w_ffn2_out):
    bsz, seq, d = x.shape
    depth = w_ada.shape[0]
    bucket = jnp.asarray(_t5_bucket_table())
    h = x.reshape(bsz * seq, d)
    for l in range(depth):
        ada = _ada(c, w_ada[l], b_ada[l]).reshape(bsz, N_ADA, d)

        def vec_for(sub):
            g = jnp.broadcast_to(jnp.stack([pre_norm_g[l, sub], post_norm_g[l, sub]])[None],
                                 (bsz, 2, d))
            return jnp.concatenate([ada[:, 3 * sub:3 * sub + 3], g,
                                    jnp.zeros((bsz, 3, d), F32)], axis=1)

        sink_t = _pair_rows(jnp.broadcast_to(sinks[l].astype(F32)[:, None, None],
                                             (N_Q_HEADS, BLOCK, LANES)))
        ln = jnp.stack([gmlp_ln_g[l], gmlp_ln_b[l]])
        bsp = jnp.repeat(jnp.transpose(gmlp_b_s[l]), GMLP_WIDTH // GMLP_GROUPS, axis=1)
        ws_t = jnp.concatenate([gmlp_w_s[l, 0::2], gmlp_w_s[l, 1::2]], axis=2).astype(BF16)

        h = _ffn(h, vec_for(0), w_ffn1_in[l].astype(BF16), w_ffn1_out[l].astype(BF16), seq)
        h = _mixer(rel_bias.astype(F32), h, vec_for(1), bucket, sink_t, ln, bsp, ws_t,
                   w_in[l].astype(BF16),
                   w_br_attn[l].astype(BF16), w_br_gmlp[l].astype(BF16), w_out[l].astype(BF16), seq)
        h = _ffn(h, vec_for(2), w_ffn2_in[l].astype(BF16), w_ffn2_out[l].astype(BF16), seq)
    return h.reshape(bsz, seq, d)
```

```python
import functools
import math

import numpy as np
import jax
import jax.numpy as jnp
from jax import lax
from jax.experimental import pallas as pl
from jax.experimental.pallas import tpu as pltpu

D_MODEL = 1024
N_Q_HEADS = 8
N_KV_HEADS = 2
GROUP = N_Q_HEADS // N_KV_HEADS
HEAD_DIM = 64
BLOCK = 128
Q_W = N_Q_HEADS * HEAD_DIM
KV_W = N_KV_HEADS * HEAD_DIM
N_BUCKETS = 32
MAX_DISTANCE = 128
GMLP_GROUPS = 8
GMLP_WIDTH = 512
D_FF = 2816
FFN_RES = 0.5
N_ADA = 9
EPS = 1e-6
NEG = -1e30

LANES = 128
VMEM_LIMIT_BYTES = 56 * 1024 * 1024
FFN_TOKEN_TILE = 1024
FFN_SUB_TILE = 512
MIX_TOKEN_TILE = 512
FFN_CHUNKS = (1536, 1280)

BF16 = jnp.bfloat16
F32 = jnp.float32


def _dot(a, b):
    return jnp.dot(a, b, preferred_element_type=F32)


def _dot_nt(a, b):
    return lax.dot_general(a, b, (((1,), (1,)), ((), ())), preferred_element_type=F32)


def _resident(shape):
    nd = len(shape)
    return pl.BlockSpec(shape, lambda i: (0,) * nd, pipeline_mode=pl.Buffered(1))


def _prenorm_mod(x, vec):
    r = lax.rsqrt(jnp.mean(x * x, axis=-1, keepdims=True) + EPS)
    gs = vec[3:4] * (1.0 + vec[1:2])
    return x * r * gs + vec[0:1]


def _postnorm_residual(x, y, vec, res):
    r = lax.rsqrt(jnp.mean(y * y, axis=-1, keepdims=True) + EPS)
    return x + (res * vec[2:3]) * (y * r * vec[4:5])


def _ada_kernel(c_ref, w_ref, b_ref, o_ref):
    c = c_ref[...]
    s = c / (1.0 + jnp.exp(-c))
    o_ref[...] = jnp.dot(s, w_ref[...], preferred_element_type=F32,
                         precision=lax.Precision.HIGHEST) + b_ref[...]


def _ada(c, w_ada, b_ada):
    b, d = c.shape
    n = w_ada.shape[1]
    rows = 8
    tn = n // 8
    c_pad = jnp.zeros((rows, d), F32).at[:b].set(c)
    out = pl.pallas_call(
        _ada_kernel,
        grid=(n // tn,),
        in_specs=[pl.BlockSpec((rows, d), lambda j: (0, 0)),
                  pl.BlockSpec((d, tn), lambda j: (0, j)),
                  pl.BlockSpec((1, tn), lambda j: (0, j))],
        out_specs=pl.BlockSpec((rows, tn), lambda j: (0, j)),
        out_shape=jax.ShapeDtypeStruct((rows, n), F32),
        compiler_params=pltpu.CompilerParams(dimension_semantics=("arbitrary",),
                                             vmem_limit_bytes=VMEM_LIMIT_BYTES),
        name="ada_proj",
    )(c_pad, w_ada, b_ada.reshape(1, n))
    return out[:b]


def _ffn_kernel(h_ref, vec_ref, w_in_ref, w_out_ref, o_ref, *, sub, chunks, d_ff, res):
    vec = vec_ref[0]
    starts = list(range(0, h_ref.shape[0], sub))

    def prenorm(r0):
        return _prenorm_mod(h_ref[r0:r0 + sub, :], vec).astype(BF16)

    def finish(r0, y):
        o_ref[r0:r0 + sub, :] = _postnorm_residual(h_ref[r0:r0 + sub, :], y, vec, res)

    n = prenorm(starts[0])
    pending = None
    for idx, r0 in enumerate(starts):
        y = None
        off = 0
        n_next = None
        for ci, width in enumerate(chunks):
            g = _dot(n, w_in_ref[:, off:off + width])
            u = _dot(n, w_in_ref[:, d_ff + off:d_ff + off + width])
            a = (g / (1.0 + jnp.exp(-g)) * u).astype(BF16)
            part = _dot(a, w_out_ref[off:off + width, :])
            y = part if y is None else y + part
            off += width
            if ci == 0 and pending is not None:
                finish(*pending)
                pending = None
            if ci == 1 and idx + 1 < len(starts):
                n_next = prenorm(starts[idx + 1])
        pending = (r0, y)
        n = n_next
    finish(*pending)


def _ffn(h, vec, w_in, w_out, seq_len):
    t, d = h.shape
    tm = min(FFN_TOKEN_TILE, seq_len)
    tiles_per_seq = seq_len // tm
    d_ff = w_out.shape[0]
    kern = functools.partial(_ffn_kernel, sub=min(FFN_SUB_TILE, tm), chunks=FFN_CHUNKS, d_ff=d_ff,
                             res=FFN_RES)
    return pl.pallas_call(
        kern,
        grid=(t // tm,),
        in_specs=[pl.BlockSpec((tm, d), lambda i: (i, 0)),
                  pl.BlockSpec((1, 8, d), lambda i: (i // tiles_per_seq, 0, 0)),
                  _resident(w_in.shape),
                  _resident(w_out.shape)],
        out_specs=pl.BlockSpec((tm, d), lambda i: (i, 0)),
        out_shape=jax.ShapeDtypeStruct((t, d), F32),
        compiler_params=pltpu.CompilerParams(dimension_semantics=("arbitrary",),
                                             vmem_limit_bytes=VMEM_LIMIT_BYTES),
        name="ffn_half_step",
    )(h, vec, w_in, w_out)


def _mixer_kernel(rel_ref, h_ref, vec_ref, bucket_ref, sink_ref, ln_ref, bsp_ref, ws_ref, w_in_ref,
                  wba_ref, wbg_ref, wo_ref, o_ref, carry_ref, attn_ref, sp_ref, bias_ref,
                  *, tm, tiles_per_seq):
    nb = tm // BLOCK
    first = (pl.program_id(0) % tiles_per_seq) == 0

    @pl.when(pl.program_id(0) == 0)
    def _():
        bucket = bucket_ref[...]
        hits = [bucket == bk for bk in range(N_BUCKETS)]
        for hd in range(N_Q_HEADS):
            acc = jnp.full(bucket.shape, NEG, F32)
            for bk in range(N_BUCKETS):
                acc = jnp.where(hits[bk], rel_ref[bk, hd], acc)
            kvh, g = divmod(hd, GROUP)
            bias_ref[2 * kvh + g % 2, (g // 2) * BLOCK:(g // 2 + 1) * BLOCK, :] = acc

    @pl.when(first)
    def _():
        carry_ref[...] = jnp.zeros_like(carry_ref)

    x = h_ref[...]
    vec = vec_ref[0]
    n = _prenorm_mod(x, vec).astype(BF16)

    lo = lax.broadcasted_iota(jnp.int32, (1, LANES), 1) < HEAD_DIM

    q = (_dot(n, w_in_ref[:, 0:Q_W]) * (HEAD_DIM ** -0.5)).astype(BF16)
    w_kv = w_in_ref[:, Q_W:Q_W + 2 * KV_W]
    kv = jnp.concatenate([_dot(n[:tm // 2], w_kv), _dot(n[tm // 2:], w_kv)], axis=0)
    k_f, v_f = kv[:, :KV_W], kv[:, KV_W:]
    k_cur, k_rot = k_f.astype(BF16), pltpu.roll(k_f, HEAD_DIM, 1).astype(BF16)
    v_cur, v_rot = v_f.astype(BF16), pltpu.roll(v_f, HEAD_DIM, 1).astype(BF16)
    k_all = jnp.concatenate([carry_ref[0], k_cur], axis=0)
    k_all_rot = jnp.concatenate([carry_ref[1], k_rot], axis=0)
    v_all = jnp.concatenate([carry_ref[2], v_cur], axis=0)
    v_all_rot = jnp.concatenate([carry_ref[3], v_rot], axis=0)
    carry_ref[0] = k_cur[tm - BLOCK:]
    carry_ref[1] = k_rot[tm - BLOCK:]
    carry_ref[2] = v_cur[tm - BLOCK:]
    carry_ref[3] = v_rot[tm - BLOCK:]
    zero = jnp.zeros((), BF16)
    k_var = ((jnp.where(lo, k_all, zero), jnp.where(lo, zero, k_all_rot)),
             (jnp.where(lo, k_all_rot, zero), jnp.where(lo, zero, k_all)))
    v_var = ((jnp.where(lo, v_all, zero), jnp.where(lo, zero, v_all_rot)),
             (jnp.where(lo, v_all_rot, zero), jnp.where(lo, zero, v_all)))

    col = lax.broadcasted_iota(jnp.int32, (1, 2 * BLOCK), 1)
    no_prev = jnp.where(jnp.logical_and(first, col < BLOCK), NEG, 0.0)

    scores = []
    for b in range(nb):
        r0 = b * BLOCK
        keys = slice(r0, r0 + 2 * BLOCK)
        per_block = []
        for kvh in range(N_KV_HEADS):
            c0 = kvh * GROUP * HEAD_DIM
            q2 = jnp.concatenate([q[r0:r0 + BLOCK, c0:c0 + LANES],
                                  q[r0:r0 + BLOCK, c0 + LANES:c0 + 2 * LANES]], axis=0)
            for par in range(2):
                s = _dot_nt(q2, k_var[kvh][par][keys]) + bias_ref[2 * kvh + par]
                per_block.append(s + no_prev if b == 0 else s)
        scores.append(jnp.stack(per_block))
    sink = sink_ref[...]
    g0 = Q_W + 2 * KV_W + 2 * GMLP_WIDTH

    def softmax_block(b):
        s = scores[b]
        m = jnp.max(s, axis=-1, keepdims=True)
        return jnp.exp(s - m).astype(BF16), jnp.exp(sink - m)

    soft = [None] * nb
    zg = _dot(n, w_in_ref[:, Q_W + 2 * KV_W:g0])
    for b in range(0, nb // 2):
        soft[b] = softmax_block(b)
    ga = _dot(n, w_in_ref[:, g0:g0 + D_MODEL])
    for b in range(nb // 2, nb):
        soft[b] = softmax_block(b)
    gg = _dot(n, w_in_ref[:, g0 + D_MODEL:g0 + 2 * D_MODEL])

    rows2 = lax.broadcasted_iota(jnp.int32, (4 * BLOCK, LANES), 0) < 2 * BLOCK
    ones_cols = jnp.where(rows2 == lo, 1.0, 0.0).astype(BF16)
    for b in range(nb):
        r0 = b * BLOCK
        keys = slice(r0, r0 + 2 * BLOCK)
        probs, sink_term = soft[b]
        for kvh in range(N_KV_HEADS):
            c0 = kvh * GROUP * HEAD_DIM
            p2 = jnp.concatenate([probs[2 * kvh], probs[2 * kvh + 1]], axis=1)
            v2 = jnp.concatenate([v_var[kvh][0][keys], v_var[kvh][1][keys]], axis=0)
            o2 = _dot(p2, jnp.concatenate([v2, ones_cols], axis=1))
            denom = o2[:, LANES:] + jnp.where(lo, sink_term[2 * kvh], sink_term[2 * kvh + 1])
            o2 = o2[:, :LANES] / denom
            attn_ref[r0:r0 + BLOCK, c0:c0 + LANES] = o2[:BLOCK].astype(BF16)
            attn_ref[r0:r0 + BLOCK, c0 + LANES:c0 + 2 * LANES] = o2[BLOCK:].astype(BF16)

    zg = 0.5 * zg * (1.0 + jnp.tanh(math.sqrt(2.0 / math.pi) * (zg + 0.044715 * (zg * zg * zg))))
    u, vr = zg[:, :GMLP_WIDTH], zg[:, GMLP_WIDTH:]
    ln = ln_ref[...]
    mu = jnp.mean(vr, axis=-1, keepdims=True)
    vc = vr - mu
    var = jnp.mean(vc * vc, axis=-1, keepdims=True)
    vg = (vc * lax.rsqrt(var + EPS) * ln[0:1] + ln[1:2]).astype(BF16)
    row = lax.broadcasted_iota(jnp.int32, (BLOCK, 2 * BLOCK), 0)
    col2 = lax.broadcasted_iota(jnp.int32, (BLOCK, 2 * BLOCK), 1)
    causal = jnp.where(col2 < BLOCK, col2, col2 - BLOCK) <= row
    for j in range(GMLP_GROUPS // 2):
        w_pair = jnp.where(causal, ws_ref[j], zero)
        pieces = [vg[c * BLOCK:(c + 1) * BLOCK, j * LANES:(j + 1) * LANES] for c in range(nb)]
        rhs = jnp.concatenate(
            [jnp.concatenate([jnp.where(lo, pc, zero) for pc in pieces], axis=1),
             jnp.concatenate([jnp.where(lo, zero, pc) for pc in pieces], axis=1)], axis=0)
        out = _dot(w_pair, rhs)
        for c in range(nb):
            sp_ref[c * BLOCK:(c + 1) * BLOCK, j * LANES:(j + 1) * LANES] = (
                out[:, c * LANES:(c + 1) * LANES])
    bsp = bsp_ref[...]
    gm = jnp.concatenate(
        [u[c * BLOCK:(c + 1) * BLOCK] * (sp_ref[c * BLOCK:(c + 1) * BLOCK, :] + bsp)
         for c in range(nb)], axis=0).astype(BF16)

    ya = _dot(attn_ref[...], wba_ref[...])
    yg = _dot(gm, wbg_ref[...])
    mix = (ya / (1.0 + jnp.exp(-ga)) + yg / (1.0 + jnp.exp(-gg))).astype(BF16)
    y = _dot(mix, wo_ref[...])
    o_ref[...] = _postnorm_residual(x, y, vec, 1.0)


def _mixer(rel, h, vec, bucket, sink_t, ln, bsp, ws_t, w_in, wba, wbg, wo, seq_len):
    t, d = h.shape
    tm = min(MIX_TOKEN_TILE, seq_len)
    tiles_per_seq = seq_len // tm
    kern = functools.partial(_mixer_kernel, tm=tm, tiles_per_seq=tiles_per_seq)
    return pl.pallas_call(
        kern,
        grid=(t // tm,),
        in_specs=[pl.BlockSpec(memory_space=pltpu.SMEM),
                  pl.BlockSpec((tm, d), lambda i: (i, 0)),
                  pl.BlockSpec((1, 8, d), lambda i: (i // tiles_per_seq, 0, 0)),
                  _resident(bucket.shape), _resident(sink_t.shape), _resident(ln.shape),
                  _resident(bsp.shape), _resident(ws_t.shape), _resident(w_in.shape),
                  _resident(wba.shape), _resident(wbg.shape), _resident(wo.shape)],
        out_specs=pl.BlockSpec((tm, d), lambda i: (i, 0)),
        out_shape=jax.ShapeDtypeStruct((t, d), F32),
        scratch_shapes=[pltpu.VMEM((4, BLOCK, KV_W), BF16),
                        pltpu.VMEM((tm, Q_W), BF16),
                        pltpu.VMEM((tm, GMLP_WIDTH), F32),
                        pltpu.VMEM((4, 2 * BLOCK, 2 * BLOCK), F32)],
        compiler_params=pltpu.CompilerParams(dimension_semantics=("arbitrary",),
                                             vmem_limit_bytes=VMEM_LIMIT_BYTES),
        name="token_mixers",
    )(rel, h, vec, bucket, sink_t, ln, bsp, ws_t, w_in, wba, wbg, wo)


def _t5_bucket_table():
    qi = np.arange(BLOCK, dtype=np.int32)[:, None]
    kj = np.arange(2 * BLOCK, dtype=np.int32)[None, :]
    dist = qi + BLOCK - kj
    in_window = (dist >= 0) & (dist < BLOCK)
    dc = np.maximum(dist, 0)
    max_exact = N_BUCKETS // 2
    d_f = np.maximum(dc, max_exact).astype(np.float32)
    large = max_exact + (np.log(d_f / np.float32(max_exact)).astype(np.float32)
                         / np.float32(math.log(MAX_DISTANCE / max_exact))
                         * np.float32(N_BUCKETS - max_exact)).astype(np.int32)
    large = np.minimum(large, N_BUCKETS - 1)
    return np.where(in_window, np.where(dc < max_exact, dc, large), -1).astype(np.int32)


def _pair_rows(a):
    out = []
    for kvh in range(N_KV_HEADS):
        for par in range(2):
            out.append(jnp.concatenate([a[GROUP * kvh + par], a[GROUP * kvh + 2 + par]], axis=0))
    return jnp.stack(out)


def kernel(x, c, rel_bias, w_ada, b_ada, pre_norm_g, post_norm_g, w_ffn1_in, w_ffn1_out, w_in,
           sinks, gmlp_ln_g, gmlp_ln_b, gmlp_w_s, gmlp_b_s, w_br_attn, w_br_gmlp, w_out,
           w_ffn2_in, w_ffn2_out):
    bsz, seq, d = x.shape
    depth = w_ada.shape[0]
    bucket = jnp.asarray(_t5_bucket_table())
    h = x.reshape(bsz * seq, d)
    for l in range(depth):
        ada = _ada(c, w_ada[l], b_ada[l]).reshape(bsz, N_ADA, d)

        def vec_for(sub):
            g = jnp.broadcast_to(jnp.stack([pre_norm_g[l, sub], post_norm_g[l, sub]])[None],
                                 (bsz, 2, d))
            return jnp.concatenate([ada[:, 3 * sub:3 * sub + 3], g,
                                    jnp.zeros((bsz, 3, d), F32)], axis=1)

        sink_t = _pair_rows(jnp.broadcast_to(sinks[l].astype(F32)[:, None, None],
                                             (N_Q_HEADS, BLOCK, LANES)))
        ln = jnp.stack([gmlp_ln_g[l], gmlp_ln_b[l]])
        bsp = jnp.repeat(jnp.transpose(gmlp_b_s[l]), GMLP_WIDTH // GMLP_GROUPS, axis=1)
        ws_t = jnp.concatenate([gmlp_w_s[l, 0::2], gmlp_w_s[l, 1::2]], axis=2).astype(BF16)

        h = _ffn(h, vec_for(0), w_ffn1_in[l].astype(BF16), w_ffn1_out[l].astype(BF16), seq)
        h = _mixer(rel_bias.astype(F32), h, vec_for(1), bucket, sink_t, ln, bsp, ws_t,
                   w_in[l].astype(BF16),
                   w_br_attn[l].astype(BF16), w_br_gmlp[l].astype(BF16), w_out[l].astype(BF16), seq)
        h = _ffn(h, vec_for(2), w_ffn2_in[l].astype(BF16), w_ffn2_out[l].astype(BF16), seq)
    return h.reshape(bsz, seq, d)
```

```python
import functools
import math

import numpy as np
import jax
import jax.numpy as jnp
from jax import lax
from jax.experimental import pallas as pl
from jax.experimental.pallas import tpu as pltpu

D_MODEL = 1024
N_Q_HEADS = 8
N_KV_HEADS = 2
GROUP = N_Q_HEADS // N_KV_HEADS
HEAD_DIM = 64
BLOCK = 128
Q_W = N_Q_HEADS * HEAD_DIM
KV_W = N_KV_HEADS * HEAD_DIM
N_BUCKETS = 32
MAX_DISTANCE = 128
GMLP_GROUPS = 8
GMLP_WIDTH = 512
D_FF = 2816
FFN_RES = 0.5
N_ADA = 9
EPS = 1e-6
NEG = -1e30

LANES = 128
VMEM_LIMIT_BYTES = 56 * 1024 * 1024
FFN_TOKEN_TILE = 1024
FFN_SUB_TILE = 512
MIX_TOKEN_TILE = 512
FFN_CHUNKS = (512, 512, 512, 512, 512, 256)

BF16 = jnp.bfloat16
F32 = jnp.float32


def _dot(a, b):
    return jnp.dot(a, b, preferred_element_type=F32)


def _dot_nt(a, b):
    return lax.dot_general(a, b, (((1,), (1,)), ((), ())), preferred_element_type=F32)


def _resident(shape):
    nd = len(shape)
    return pl.BlockSpec(shape, lambda i: (0,) * nd, pipeline_mode=pl.Buffered(1))


def _prenorm_mod(x, vec):
    r = lax.rsqrt(jnp.mean(x * x, axis=-1, keepdims=True) + EPS)
    gs = vec[3:4] * (1.0 + vec[1:2])
    return x * r * gs + vec[0:1]


def _postnorm_residual(x, y, vec, res):
    r = lax.rsqrt(jnp.mean(y * y, axis=-1, keepdims=True) + EPS)
    return x + (res * vec[2:3]) * (y * r * vec[4:5])


def _ada_kernel(c_ref, w_ref, b_ref, o_ref):
    c = c_ref[...]
    s = c / (1.0 + jnp.exp(-c))
    w = w_ref[...]
    s_hi = s.astype(BF16)
    s_lo = (s - s_hi.astype(F32)).astype(BF16)
    w_hi = w.astype(BF16)
    w_lo = (w - w_hi.astype(F32)).astype(BF16)
    rows = s.shape[0]
    head = _dot(jnp.concatenate([s_hi, s_lo], axis=0), w_hi)
    o_ref[...] = head[:rows] + head[rows:] + _dot(s_hi, w_lo) + b_ref[...]


def _ada(c, w_ada, b_ada):
    b, d = c.shape
    n = w_ada.shape[1]
    rows = 16
    tn = n // 8
    c_pad = jnp.zeros((rows, d), F32).at[:b].set(c)
    out = pl.pallas_call(
        _ada_kernel,
        grid=(n // tn,),
        in_specs=[pl.BlockSpec((rows, d), lambda j: (0, 0)),
                  pl.BlockSpec((d, tn), lambda j: (0, j)),
                  pl.BlockSpec((1, tn), lambda j: (0, j))],
        out_specs=pl.BlockSpec((rows, tn), lambda j: (0, j)),
        out_shape=jax.ShapeDtypeStruct((rows, n), F32),
        compiler_params=pltpu.CompilerParams(dimension_semantics=("arbitrary",),
                                             vmem_limit_bytes=VMEM_LIMIT_BYTES),
        name="ada_proj",
    )(c_pad, w_ada, b_ada.reshape(1, n))
    return out[:b]


def _ffn_kernel(h_ref, vec_ref, w_in_ref, w_out_ref, o_ref, *, sub, chunks, d_ff, res):
    vec = vec_ref[0]
    starts = list(range(0, h_ref.shape[0], sub))

    def prenorm(r0):
        return _prenorm_mod(h_ref[r0:r0 + sub, :], vec).astype(BF16)

    def finish(r0, y):
        o_ref[r0:r0 + sub, :] = _postnorm_residual(h_ref[r0:r0 + sub, :], y, vec, res)

    n = prenorm(starts[0])
    pending = None
    for idx, r0 in enumerate(starts):
        y = None
        off = 0
        n_next = None
        for ci, width in enumerate(chunks):
            g = _dot(n, w_in_ref[:, off:off + width])
            u = _dot(n, w_in_ref[:, d_ff + off:d_ff + off + width])
            a = (g / (1.0 + jnp.exp(-g)) * u).astype(BF16)
            part = _dot(a, w_out_ref[off:off + width, :])
            y = part if y is None else y + part
            off += width
            if ci == 0 and pending is not None:
                finish(*pending)
                pending = None
            if ci == 1 and idx + 1 < len(starts):
                n_next = prenorm(starts[idx + 1])
        pending = (r0, y)
        n = n_next
    finish(*pending)


def _ffn(h, vec, w_in, w_out, seq_len):
    t, d = h.shape
    tm = min(FFN_TOKEN_TILE, seq_len)
    tiles_per_seq = seq_len // tm
    d_ff = w_out.shape[0]
    kern = functools.partial(_ffn_kernel, sub=min(FFN_SUB_TILE, tm), chunks=FFN_CHUNKS, d_ff=d_ff,
                             res=FFN_RES)
    return pl.pallas_call(
        kern,
        grid=(t // tm,),
        in_specs=[pl.BlockSpec((tm, d), lambda i: (i, 0)),
                  pl.BlockSpec((1, 8, d), lambda i: (i // tiles_per_seq, 0, 0)),
                  _resident(w_in.shape),
                  _resident(w_out.shape)],
        out_specs=pl.BlockSpec((tm, d), lambda i: (i, 0)),
        out_shape=jax.ShapeDtypeStruct((t, d), F32),
        compiler_params=pltpu.CompilerParams(dimension_semantics=("arbitrary",),
                                             vmem_limit_bytes=VMEM_LIMIT_BYTES),
        name="ffn_half_step",
    )(h, vec, w_in, w_out)


def _mixer_kernel(rel_ref, h_ref, vec_ref, bucket_ref, sink_ref, ln_ref, bsp_ref, ws_ref, w_in_ref,
                  wba_ref, wbg_ref, wo_ref, o_ref, carry_ref, attn_ref, sp_ref, bias_ref,
                  *, tm, tiles_per_seq):
    nb = tm // BLOCK
    first = (pl.program_id(0) % tiles_per_seq) == 0

    @pl.when(pl.program_id(0) == 0)
    def _():
        bucket = bucket_ref[...]
        hits = [bucket == bk for bk in range(N_BUCKETS)]
        for hd in range(N_Q_HEADS):
            acc = jnp.full(bucket.shape, NEG, F32)
            for bk in range(N_BUCKETS):
                acc = jnp.where(hits[bk], rel_ref[bk, hd], acc)
            kvh, g = divmod(hd, GROUP)
            bias_ref[2 * kvh + g % 2, (g // 2) * BLOCK:(g // 2 + 1) * BLOCK, :] = acc

    @pl.when(first)
    def _():
        carry_ref[...] = jnp.zeros_like(carry_ref)

    x = h_ref[...]
    vec = vec_ref[0]
    n = _prenorm_mod(x, vec).astype(BF16)

    lo = lax.broadcasted_iota(jnp.int32, (1, LANES), 1) < HEAD_DIM

    q = (_dot(n, w_in_ref[:, 0:Q_W]) * (HEAD_DIM ** -0.5)).astype(BF16)
    w_kv = w_in_ref[:, Q_W:Q_W + 2 * KV_W]
    kv = jnp.concatenate([_dot(n[:tm // 2], w_kv), _dot(n[tm // 2:], w_kv)], axis=0)
    k_f, v_f = kv[:, :KV_W], kv[:, KV_W:]
    k_cur, k_rot = k_f.astype(BF16), pltpu.roll(k_f, HEAD_DIM, 1).astype(BF16)
    v_cur, v_rot = v_f.astype(BF16), pltpu.roll(v_f, HEAD_DIM, 1).astype(BF16)
    k_all = jnp.concatenate([carry_ref[0], k_cur], axis=0)
    k_all_rot = jnp.concatenate([carry_ref[1], k_rot], axis=0)
    v_all = jnp.concatenate([carry_ref[2], v_cur], axis=0)
    v_all_rot = jnp.concatenate([carry_ref[3], v_rot], axis=0)
    carry_ref[0] = k_cur[tm - BLOCK:]
    carry_ref[1] = k_rot[tm - BLOCK:]
    carry_ref[2] = v_cur[tm - BLOCK:]
    carry_ref[3] = v_rot[tm - BLOCK:]
    zero = jnp.zeros((), BF16)
    k_var = ((jnp.where(lo, k_all, zero), jnp.where(lo, zero, k_all_rot)),
             (jnp.where(lo, k_all_rot, zero), jnp.where(lo, zero, k_all)))
    v_var = ((jnp.where(lo, v_all, zero), jnp.where(lo, zero, v_all_rot)),
             (jnp.where(lo, v_all_rot, zero), jnp.where(lo, zero, v_all)))

    col = lax.broadcasted_iota(jnp.int32, (1, 2 * BLOCK), 1)
    no_prev = jnp.where(jnp.logical_and(first, col < BLOCK), NEG, 0.0)

    scores = []
    for b in range(nb):
        r0 = b * BLOCK
        keys = slice(r0, r0 + 2 * BLOCK)
        per_block = []
        for kvh in range(N_KV_HEADS):
            c0 = kvh * GROUP * HEAD_DIM
            q2 = jnp.concatenate([q[r0:r0 + BLOCK, c0:c0 + LANES],
                                  q[r0:r0 + BLOCK, c0 + LANES:c0 + 2 * LANES]], axis=0)
            for par in range(2):
                s = _dot_nt(q2, k_var[kvh][par][keys]) + bias_ref[2 * kvh + par]
                per_block.append(s + no_prev if b == 0 else s)
        scores.append(jnp.stack(per_block))
    sink = sink_ref[...]
    g0 = Q_W + 2 * KV_W + 2 * GMLP_WIDTH

    def softmax_block(b):
        s = scores[b]
        m = jnp.max(s, axis=-1, keepdims=True)
        return jnp.exp(s - m).astype(BF16), jnp.exp(sink - m)

    soft = [None] * nb
    zg = _dot(n, w_in_ref[:, Q_W + 2 * KV_W:g0])
    for b in range(0, nb // 2):
        soft[b] = softmax_block(b)
    ga = _dot(n, w_in_ref[:, g0:g0 + D_MODEL])
    for b in range(nb // 2, nb):
        soft[b] = softmax_block(b)
    gg = _dot(n, w_in_ref[:, g0 + D_MODEL:g0 + 2 * D_MODEL])

    rows2 = lax.broadcasted_iota(jnp.int32, (4 * BLOCK, LANES), 0) < 2 * BLOCK
    ones_cols = jnp.where(rows2 == lo, 1.0, 0.0).astype(BF16)
    for b in range(nb):
        r0 = b * BLOCK
        keys = slice(r0, r0 + 2 * BLOCK)
        probs, sink_term = soft[b]
        for kvh in range(N_KV_HEADS):
            c0 = kvh * GROUP * HEAD_DIM
            p2 = jnp.concatenate([probs[2 * kvh], probs[2 * kvh + 1]], axis=1)
            v2 = jnp.concatenate([v_var[kvh][0][keys], v_var[kvh][1][keys]], axis=0)
            o2 = _dot(p2, jnp.concatenate([v2, ones_cols], axis=1))
            denom = o2[:, LANES:] + jnp.where(lo, sink_term[2 * kvh], sink_term[2 * kvh + 1])
            o2 = o2[:, :LANES] / denom
            attn_ref[r0:r0 + BLOCK, c0:c0 + LANES] = o2[:BLOCK].astype(BF16)
            attn_ref[r0:r0 + BLOCK, c0 + LANES:c0 + 2 * LANES] = o2[BLOCK:].astype(BF16)

    zg = 0.5 * zg * (1.0 + jnp.tanh(math.sqrt(2.0 / math.pi) * (zg + 0.044715 * (zg * zg * zg))))
    u, vr = zg[:, :GMLP_WIDTH], zg[:, GMLP_WIDTH:]
    ln = ln_ref[...]
    mu = jnp.mean(vr, axis=-1, keepdims=True)
    vc = vr - mu
    var = jnp.mean(vc * vc, axis=-1, keepdims=True)
    vg = (vc * lax.rsqrt(var + EPS) * ln[0:1] + ln[1:2]).astype(BF16)
    row = lax.broadcasted_iota(jnp.int32, (BLOCK, 2 * BLOCK), 0)
    col2 = lax.broadcasted_iota(jnp.int32, (BLOCK, 2 * BLOCK), 1)
    causal = jnp.where(col2 < BLOCK, col2, col2 - BLOCK) <= row
    for j in range(GMLP_GROUPS // 2):
        w_pair = jnp.where(causal, ws_ref[j], zero)
        pieces = [vg[c * BLOCK:(c + 1) * BLOCK, j * LANES:(j + 1) * LANES] for c in range(nb)]
        rhs = jnp.concatenate(
            [jnp.concatenate([jnp.where(lo, pc, zero) for pc in pieces], axis=1),
             jnp.concatenate([jnp.where(lo, zero, pc) for pc in pieces], axis=1)], axis=0)
        out = _dot(w_pair, rhs)
        for c in range(nb):
            sp_ref[c * BLOCK:(c + 1) * BLOCK, j * LANES:(j + 1) * LANES] = (
                out[:, c * LANES:(c + 1) * LANES])
    bsp = bsp_ref[...]
    gm = jnp.concatenate(
        [u[c * BLOCK:(c + 1) * BLOCK] * (sp_ref[c * BLOCK:(c + 1) * BLOCK, :] + bsp)
         for c in range(nb)], axis=0).astype(BF16)

    ya = _dot(attn_ref[...], wba_ref[...])
    yg = _dot(gm, wbg_ref[...])
    mix = (ya / (1.0 + jnp.exp(-ga)) + yg / (1.0 + jnp.exp(-gg))).astype(BF16)
    y = _dot(mix, wo_ref[...])
    o_ref[...] = _postnorm_residual(x, y, vec, 1.0)


def _mixer(rel, h, vec, bucket, sink_t, ln, bsp, ws_t, w_in, wba, wbg, wo, seq_len):
    t, d = h.shape
    tm = min(MIX_TOKEN_TILE, seq_len)
    tiles_per_seq = seq_len // tm
    kern = functools.partial(_mixer_kernel, tm=tm, tiles_per_seq=tiles_per_seq)
    return pl.pallas_call(
        kern,
        grid=(t // tm,),
        in_specs=[pl.BlockSpec(memory_space=pltpu.SMEM),
                  pl.BlockSpec((tm, d), lambda i: (i, 0)),
                  pl.BlockSpec((1, 8, d), lambda i: (i // tiles_per_seq, 0, 0)),
                  _resident(bucket.shape), _resident(sink_t.shape), _resident(ln.shape),
                  _resident(bsp.shape), _resident(ws_t.shape), _resident(w_in.shape),
                  _resident(wba.shape), _resident(wbg.shape), _resident(wo.shape)],
        out_specs=pl.BlockSpec((tm, d), lambda i: (i, 0)),
        out_shape=jax.ShapeDtypeStruct((t, d), F32),
        scratch_shapes=[pltpu.VMEM((4, BLOCK, KV_W), BF16),
                        pltpu.VMEM((tm, Q_W), BF16),
                        pltpu.VMEM((tm, GMLP_WIDTH), F32),
                        pltpu.VMEM((4, 2 * BLOCK, 2 * BLOCK), F32)],
        compiler_params=pltpu.CompilerParams(dimension_semantics=("arbitrary",),
                                             vmem_limit_bytes=VMEM_LIMIT_BYTES),
        name="token_mixers",
    )(rel, h, vec, bucket, sink_t, ln, bsp, ws_t, w_in, wba, wbg, wo)


def _t5_bucket_table():
    qi = np.arange(BLOCK, dtype=np.int32)[:, None]
    kj = np.arange(2 * BLOCK, dtype=np.int32)[None, :]
    dist = qi + BLOCK - kj
    in_window = (dist >= 0) & (dist < BLOCK)
    dc = np.maximum(dist, 0)
    max_exact = N_BUCKETS // 2
    d_f = np.maximum(dc, max_exact).astype(np.float32)
    large = max_exact + (np.log(d_f / np.float32(max_exact)).astype(np.float32)
                         / np.float32(math.log(MAX_DISTANCE / max_exact))
                         * np.float32(N_BUCKETS - max_exact)).astype(np.int32)
    large = np.minimum(large, N_BUCKETS - 1)
    return np.where(in_window, np.where(dc < max_exact, dc, large), -1).astype(np.int32)


def _pair_rows(a):
    out = []
    for kvh in range(N_KV_HEADS):
        for par in range(2):
            out.append(jnp.concatenate([a[GROUP * kvh + par], a[GROUP * kvh + 2 + par]], axis=0))
    return jnp.stack(out)


def kernel(x, c, rel_bias, w_ada, b_ada, pre_norm_g, post_norm_g, w_ffn1_in, w_ffn1_out, w_in,
           sinks, gmlp_ln_g, gmlp_ln_b, gmlp_w_s, gmlp_b_s, w_br_attn, w_br_gmlp, w_out,
           w_ffn2_in, w_ffn2_out):
    bsz, seq, d = x.shape
    depth = w_ada.shape[0]
    bucket = jnp.asarray(_t5_bucket_table())
    h = x.reshape(bsz * seq, d)
    for l in range(depth):
        ada = _ada(c, w_ada[l], b_ada[l]).reshape(bsz, N_ADA, d)

        def vec_for(sub):
            g = jnp.broadcast_to(jnp.stack([pre_norm_g[l, sub], post_norm_g[l, sub]])[None],
                                 (bsz, 2, d))
            return jnp.concatenate([ada[:, 3 * sub:3 * sub + 3], g,
                                    jnp.zeros((bsz, 3, d), F32)], axis=1)

        sink_t = _pair_rows(jnp.broadcast_to(sinks[l].astype(F32)[:, None, None],
                                             (N_Q_HEADS, BLOCK, LANES)))
        ln = jnp.stack([gmlp_ln_g[l], gmlp_ln_b[l]])
        bsp = jnp.repeat(jnp.transpose(gmlp_b_s[l]), GMLP_WIDTH // GMLP_GROUPS, axis=1)
        ws_t = jnp.concatenate([gmlp_w_s[l, 0::2], gmlp_w_s[l, 1::2]], axis=2).astype(BF16)

        h = _ffn(h, vec_for(0), w_ffn1_in[l].astype(BF16), w_ffn1_out[l].astype(BF16), seq)
        h = _mixer(rel_bias.astype(F32), h, vec_for(1), bucket, sink_t, ln, bsp, ws_t,
                   w_in[l].astype(BF16),
                   w_br_attn[l].astype(BF16), w_br_gmlp[l].astype(BF16), w_out[l].astype(BF16), seq)
        h = _ffn(h, vec_for(2), w_ffn2_in[l].astype(BF16), w_ffn2_out[l].astype(BF16), seq)
    return h.reshape(bsz, seq, d)
```

```python
import functools
import math

import numpy as np
import jax
import jax.numpy as jnp
from jax import lax
from jax.experimental import pallas as pl
from jax.experimental.pallas import tpu as pltpu

D_MODEL = 1024
N_Q_HEADS = 8
N_KV_HEADS = 2
GROUP = N_Q_HEADS // N_KV_HEADS
HEAD_DIM = 64
BLOCK = 128
Q_W = N_Q_HEADS * HEAD_DIM
KV_W = N_KV_HEADS * HEAD_DIM
N_BUCKETS = 32
MAX_DISTANCE = 128
GMLP_GROUPS = 8
GMLP_WIDTH = 512
D_FF = 2816
FFN_RES = 0.5
N_ADA = 9
EPS = 1e-6
NEG = -1e30

LANES = 128
VMEM_LIMIT_BYTES = 56 * 1024 * 1024
FFN_TOKEN_TILE = 1024
FFN_SUB_TILE = 512
MIX_TOKEN_TILE = 1024
MIX_SUB_TILE = 512
FFN_CHUNKS = (512, 512, 512, 512, 512, 256)

BF16 = jnp.bfloat16
F32 = jnp.float32


def _dot(a, b):
    return jnp.dot(a, b, preferred_element_type=F32)


def _dot_nt(a, b):
    return lax.dot_general(a, b, (((1,), (1,)), ((), ())), preferred_element_type=F32)


def _resident(shape):
    nd = len(shape)
    return pl.BlockSpec(shape, lambda i: (0,) * nd, pipeline_mode=pl.Buffered(1))


def _prenorm_mod(x, vec):
    r = lax.rsqrt(jnp.mean(x * x, axis=-1, keepdims=True) + EPS)
    gs = vec[3:4] * (1.0 + vec[1:2])
    return x * r * gs + vec[0:1]


def _postnorm_residual(x, y, vec, res):
    r = lax.rsqrt(jnp.mean(y * y, axis=-1, keepdims=True) + EPS)
    return x + (res * vec[2:3]) * (y * r * vec[4:5])


def _ada_kernel(c_ref, w_ref, b_ref, o_ref):
    c = c_ref[...]
    s = c / (1.0 + jnp.exp(-c))
    w = w_ref[...]
    s_hi = s.astype(BF16)
    s_lo = (s - s_hi.astype(F32)).astype(BF16)
    w_hi = w.astype(BF16)
    w_lo = (w - w_hi.astype(F32)).astype(BF16)
    rows = s.shape[0]
    head = _dot(jnp.concatenate([s_hi, s_lo], axis=0), w_hi)
    o_ref[...] = head[:rows] + head[rows:] + _dot(s_hi, w_lo) + b_ref[...]


def _ada(c, w_ada, b_ada):
    b, d = c.shape
    n = w_ada.shape[1]
    rows = 16
    tn = n // 8
    c_pad = jnp.zeros((rows, d), F32).at[:b].set(c)
    out = pl.pallas_call(
        _ada_kernel,
        grid=(n // tn,),
        in_specs=[pl.BlockSpec((rows, d), lambda j: (0, 0)),
                  pl.BlockSpec((d, tn), lambda j: (0, j)),
                  pl.BlockSpec((1, tn), lambda j: (0, j))],
        out_specs=pl.BlockSpec((rows, tn), lambda j: (0, j)),
        out_shape=jax.ShapeDtypeStruct((rows, n), F32),
        compiler_params=pltpu.CompilerParams(dimension_semantics=("arbitrary",),
                                             vmem_limit_bytes=VMEM_LIMIT_BYTES),
        name="ada_proj",
    )(c_pad, w_ada, b_ada.reshape(1, n))
    return out[:b]


def _ffn_kernel(h_ref, vec_ref, w_in_ref, w_out_ref, o_ref, *, sub, chunks, d_ff, res):
    vec = vec_ref[0]
    starts = list(range(0, h_ref.shape[0], sub))

    def prenorm(r0):
        return _prenorm_mod(h_ref[r0:r0 + sub, :], vec).astype(BF16)

    def finish(r0, y):
        o_ref[r0:r0 + sub, :] = _postnorm_residual(h_ref[r0:r0 + sub, :], y, vec, res)

    n = prenorm(starts[0])
    pending = None
    for idx, r0 in enumerate(starts):
        y = None
        off = 0
        n_next = None
        for ci, width in enumerate(chunks):
            g = _dot(n, w_in_ref[:, off:off + width])
            u = _dot(n, w_in_ref[:, d_ff + off:d_ff + off + width])
            a = (g / (1.0 + jnp.exp(-g)) * u).astype(BF16)
            part = _dot(a, w_out_ref[off:off + width, :])
            y = part if y is None else y + part
            off += width
            if ci == 0 and pending is not None:
                finish(*pending)
                pending = None
            if ci == 1 and idx + 1 < len(starts):
                n_next = prenorm(starts[idx + 1])
        pending = (r0, y)
        n = n_next
    finish(*pending)


def _ffn(h, vec, w_in, w_out, seq_len):
    t, d = h.shape
    tm = min(FFN_TOKEN_TILE, seq_len)
    tiles_per_seq = seq_len // tm
    d_ff = w_out.shape[0]
    kern = functools.partial(_ffn_kernel, sub=min(FFN_SUB_TILE, tm), chunks=FFN_CHUNKS, d_ff=d_ff,
                             res=FFN_RES)
    return pl.pallas_call(
        kern,
        grid=(t // tm,),
        in_specs=[pl.BlockSpec((tm, d), lambda i: (i, 0)),
                  pl.BlockSpec((1, 8, d), lambda i: (i // tiles_per_seq, 0, 0)),
                  _resident(w_in.shape),
                  _resident(w_out.shape)],
        out_specs=pl.BlockSpec((tm, d), lambda i: (i, 0)),
        out_shape=jax.ShapeDtypeStruct((t, d), F32),
        compiler_params=pltpu.CompilerParams(dimension_semantics=("arbitrary",),
                                             vmem_limit_bytes=VMEM_LIMIT_BYTES),
        name="ffn_half_step",
    )(h, vec, w_in, w_out)


def _mixer_kernel(rel_ref, h_ref, vec_ref, bucket_ref, sink_ref, ln_ref, bsp_ref, ws_ref, w_in_ref,
                  wba_ref, wbg_ref, wo_ref, o_ref, carry_ref, attn_ref, sp_ref, bias_ref,
                  *, tm, sub, tiles_per_seq):
    first = (pl.program_id(0) % tiles_per_seq) == 0

    @pl.when(pl.program_id(0) == 0)
    def _():
        bucket = bucket_ref[...]
        hits = [bucket == bk for bk in range(N_BUCKETS)]
        for hd in range(N_Q_HEADS):
            acc = jnp.full(bucket.shape, NEG, F32)
            for bk in range(N_BUCKETS):
                acc = jnp.where(hits[bk], rel_ref[bk, hd], acc)
            kvh, g = divmod(hd, GROUP)
            bias_ref[2 * kvh + g % 2, (g // 2) * BLOCK:(g // 2 + 1) * BLOCK, :] = acc

    @pl.when(first)
    def _():
        carry_ref[...] = jnp.zeros_like(carry_ref)

    for r in range(0, tm, sub):
        rows = pl.ds(r, sub)
        _mixer_tile(first if r == 0 else None, h_ref.at[rows], vec_ref, sink_ref, ln_ref, bsp_ref,
                    ws_ref, w_in_ref, wba_ref, wbg_ref, wo_ref, o_ref.at[rows], carry_ref,
                    attn_ref.at[rows], sp_ref.at[rows], bias_ref, tm=sub)


def _mixer_tile(first, h_ref, vec_ref, sink_ref, ln_ref, bsp_ref, ws_ref, w_in_ref, wba_ref, wbg_ref,
                wo_ref, o_ref, carry_ref, attn_ref, sp_ref, bias_ref, *, tm):
    nb = tm // BLOCK
    x = h_ref[...]
    vec = vec_ref[0]
    n = _prenorm_mod(x, vec).astype(BF16)

    lo = lax.broadcasted_iota(jnp.int32, (1, LANES), 1) < HEAD_DIM

    q = (_dot(n, w_in_ref[:, 0:Q_W]) * (HEAD_DIM ** -0.5)).astype(BF16)
    w_kv = w_in_ref[:, Q_W:Q_W + 2 * KV_W]
    kv = jnp.concatenate([_dot(n[:tm // 2], w_kv), _dot(n[tm // 2:], w_kv)], axis=0)
    k_f, v_f = kv[:, :KV_W], kv[:, KV_W:]
    k_cur, k_rot = k_f.astype(BF16), pltpu.roll(k_f, HEAD_DIM, 1).astype(BF16)
    v_cur, v_rot = v_f.astype(BF16), pltpu.roll(v_f, HEAD_DIM, 1).astype(BF16)
    k_all = jnp.concatenate([carry_ref[0], k_cur], axis=0)
    k_all_rot = jnp.concatenate([carry_ref[1], k_rot], axis=0)
    v_all = jnp.concatenate([carry_ref[2], v_cur], axis=0)
    v_all_rot = jnp.concatenate([carry_ref[3], v_rot], axis=0)
    carry_ref[0] = k_cur[tm - BLOCK:]
    carry_ref[1] = k_rot[tm - BLOCK:]
    carry_ref[2] = v_cur[tm - BLOCK:]
    carry_ref[3] = v_rot[tm - BLOCK:]
    zero = jnp.zeros((), BF16)
    k_var = ((jnp.where(lo, k_all, zero), jnp.where(lo, zero, k_all_rot)),
             (jnp.where(lo, k_all_rot, zero), jnp.where(lo, zero, k_all)))
    v_var = ((jnp.where(lo, v_all, zero), jnp.where(lo, zero, v_all_rot)),
             (jnp.where(lo, v_all_rot, zero), jnp.where(lo, zero, v_all)))

    col = lax.broadcasted_iota(jnp.int32, (1, 2 * BLOCK), 1)
    no_prev = None if first is None else jnp.where(jnp.logical_and(first, col < BLOCK), NEG, 0.0)

    scores = []
    for b in range(nb):
        r0 = b * BLOCK
        keys = slice(r0, r0 + 2 * BLOCK)
        per_block = []
        for kvh in range(N_KV_HEADS):
            c0 = kvh * GROUP * HEAD_DIM
            q2 = jnp.concatenate([q[r0:r0 + BLOCK, c0:c0 + LANES],
                                  q[r0:r0 + BLOCK, c0 + LANES:c0 + 2 * LANES]], axis=0)
            for par in range(2):
                s = _dot_nt(q2, k_var[kvh][par][keys]) + bias_ref[2 * kvh + par]
                per_block.append(s + no_prev if (b == 0 and no_prev is not None) else s)
        scores.append(jnp.stack(per_block))
    sink = sink_ref[...]
    g0 = Q_W + 2 * KV_W + 2 * GMLP_WIDTH

    def softmax_block(b):
        s = scores[b]
        m = jnp.max(s, axis=-1, keepdims=True)
        return jnp.exp(s - m).astype(BF16), jnp.exp(sink - m)

    soft = [None] * nb
    zg = _dot(n, w_in_ref[:, Q_W + 2 * KV_W:g0])
    for b in range(0, nb // 2):
        soft[b] = softmax_block(b)
    ga = _dot(n, w_in_ref[:, g0:g0 + D_MODEL])
    for b in range(nb // 2, nb):
        soft[b] = softmax_block(b)
    gg = _dot(n, w_in_ref[:, g0 + D_MODEL:g0 + 2 * D_MODEL])

    rows2 = lax.broadcasted_iota(jnp.int32, (4 * BLOCK, LANES), 0) < 2 * BLOCK
    ones_cols = jnp.where(rows2 == lo, 1.0, 0.0).astype(BF16)
    for b in range(nb):
        r0 = b * BLOCK
        keys = slice(r0, r0 + 2 * BLOCK)
        probs, sink_term = soft[b]
        for kvh in range(N_KV_HEADS):
            c0 = kvh * GROUP * HEAD_DIM
            p2 = jnp.concatenate([probs[2 * kvh], probs[2 * kvh + 1]], axis=1)
            v2 = jnp.concatenate([v_var[kvh][0][keys], v_var[kvh][1][keys]], axis=0)
            o2 = _dot(p2, jnp.concatenate([v2, ones_cols], axis=1))
            denom = o2[:, LANES:] + jnp.where(lo, sink_term[2 * kvh], sink_term[2 * kvh + 1])
            o2 = o2[:, :LANES] / denom
            attn_ref[r0:r0 + BLOCK, c0:c0 + LANES] = o2[:BLOCK].astype(BF16)
            attn_ref[r0:r0 + BLOCK, c0 + LANES:c0 + 2 * LANES] = o2[BLOCK:].astype(BF16)

    zg = 0.5 * zg * (1.0 + jnp.tanh(math.sqrt(2.0 / math.pi) * (zg + 0.044715 * (zg * zg * zg))))
    u, vr = zg[:, :GMLP_WIDTH], zg[:, GMLP_WIDTH:]
    ln = ln_ref[...]
    mu = jnp.mean(vr, axis=-1, keepdims=True)
    vc = vr - mu
    var = jnp.mean(vc * vc, axis=-1, keepdims=True)
    vg = (vc * lax.rsqrt(var + EPS) * ln[0:1] + ln[1:2]).astype(BF16)
    row = lax.broadcasted_iota(jnp.int32, (BLOCK, 2 * BLOCK), 0)
    col2 = lax.broadcasted_iota(jnp.int32, (BLOCK, 2 * BLOCK), 1)
    causal = jnp.where(col2 < BLOCK, col2, col2 - BLOCK) <= row
    for j in range(GMLP_GROUPS // 2):
        w_pair = jnp.where(causal, ws_ref[j], zero)
        pieces = [vg[c * BLOCK:(c + 1) * BLOCK, j * LANES:(j + 1) * LANES] for c in range(nb)]
        rhs = jnp.concatenate(
            [jnp.concatenate([jnp.where(lo, pc, zero) for pc in pieces], axis=1),
             jnp.concatenate([jnp.where(lo, zero, pc) for pc in pieces], axis=1)], axis=0)
        out = _dot(w_pair, rhs)
        for c in range(nb):
            sp_ref[c * BLOCK:(c + 1) * BLOCK, j * LANES:(j + 1) * LANES] = (
                out[:, c * LANES:(c + 1) * LANES])
    bsp = bsp_ref[...]
    gm = jnp.concatenate(
        [u[c * BLOCK:(c + 1) * BLOCK] * (sp_ref[c * BLOCK:(c + 1) * BLOCK, :] + bsp)
         for c in range(nb)], axis=0).astype(BF16)

    ya = _dot(attn_ref[...], wba_ref[...])
    yg = _dot(gm, wbg_ref[...])
    mix = (ya / (1.0 + jnp.exp(-ga)) + yg / (1.0 + jnp.exp(-gg))).astype(BF16)
    y = _dot(mix, wo_ref[...])
    o_ref[...] = _postnorm_residual(x, y, vec, 1.0)


def _mixer(rel, h, vec, bucket, sink_t, ln, bsp, ws_t, w_in, wba, wbg, wo, seq_len):
    t, d = h.shape
    tm = min(MIX_TOKEN_TILE, seq_len)
    tiles_per_seq = seq_len // tm
    kern = functools.partial(_mixer_kernel, tm=tm, sub=min(MIX_SUB_TILE, tm),
                             tiles_per_seq=tiles_per_seq)
    return pl.pallas_call(
        kern,
        grid=(t // tm,),
        in_specs=[pl.BlockSpec(memory_space=pltpu.SMEM),
                  pl.BlockSpec((tm, d), lambda i: (i, 0)),
                  pl.BlockSpec((1, 8, d), lambda i: (i // tiles_per_seq, 0, 0)),
                  _resident(bucket.shape), _resident(sink_t.shape), _resident(ln.shape),
                  _resident(bsp.shape), _resident(ws_t.shape), _resident(w_in.shape),
                  _resident(wba.shape), _resident(wbg.shape), _resident(wo.shape)],
        out_specs=pl.BlockSpec((tm, d), lambda i: (i, 0)),
        out_shape=jax.ShapeDtypeStruct((t, d), F32),
        scratch_shapes=[pltpu.VMEM((4, BLOCK, KV_W), BF16),
                        pltpu.VMEM((tm, Q_W), BF16),
                        pltpu.VMEM((tm, GMLP_WIDTH), F32),
                        pltpu.VMEM((4, 2 * BLOCK, 2 * BLOCK), F32)],
        compiler_params=pltpu.CompilerParams(dimension_semantics=("arbitrary",),
                                             vmem_limit_bytes=VMEM_LIMIT_BYTES),
        name="token_mixers",
    )(rel, h, vec, bucket, sink_t, ln, bsp, ws_t, w_in, wba, wbg, wo)


def _t5_bucket_table():
    qi = np.arange(BLOCK, dtype=np.int32)[:, None]
    kj = np.arange(2 * BLOCK, dtype=np.int32)[None, :]
    dist = qi + BLOCK - kj
    in_window = (dist >= 0) & (dist < BLOCK)
    dc = np.maximum(dist, 0)
    max_exact = N_BUCKETS // 2
    d_f = np.maximum(dc, max_exact).astype(np.float32)
    large = max_exact + (np.log(d_f / np.float32(max_exact)).astype(np.float32)
                         / np.float32(math.log(MAX_DISTANCE / max_exact))
                         * np.float32(N_BUCKETS - max_exact)).astype(np.int32)
    large = np.minimum(large, N_BUCKETS - 1)
    return np.where(in_window, np.where(dc < max_exact, dc, large), -1).astype(np.int32)


def _pair_rows(a):
    out = []
    for kvh in range(N_KV_HEADS):
        for par in range(2):
            out.append(jnp.concatenate([a[GROUP * kvh + par], a[GROUP * kvh + 2 + par]], axis=0))
    return jnp.stack(out)


def kernel(x, c, rel_bias, w_ada, b_ada, pre_norm_g, post_norm_g, w_ffn1_in, w_ffn1_out, w_in,
           sinks, gmlp_ln_g, gmlp_ln_b, gmlp_w_s, gmlp_b_s, w_br_attn, w_br_gmlp, w_out,
           w_ffn2_in, w_ffn2_out):
    bsz, seq, d = x.shape
    depth = w_ada.shape[0]
    bucket = jnp.asarray(_t5_bucket_table())
    h = x.reshape(bsz * seq, d)
    for l in range(depth):
        ada = _ada(c, w_ada[l], b_ada[l]).reshape(bsz, N_ADA, d)

        def vec_for(sub):
            g = jnp.broadcast_to(jnp.stack([pre_norm_g[l, sub], post_norm_g[l, sub]])[None],
                                 (bsz, 2, d))
            return jnp.concatenate([ada[:, 3 * sub:3 * sub + 3], g,
                                    jnp.zeros((bsz, 3, d), F32)], axis=1)

        sink_t = _pair_rows(jnp.broadcast_to(sinks[l].astype(F32)[:, None, None],
                                             (N_Q_HEADS, BLOCK, LANES)))
        ln = jnp.stack([gmlp_ln_g[l], gmlp_ln_b[l]])
        bsp = jnp.repeat(jnp.transpose(gmlp_b_s[l]), GMLP_WIDTH // GMLP_GROUPS, axis=1)
        ws_t = jnp.concatenate([gmlp_w_s[l, 0::2], gmlp_w_s[l, 1::2]], axis=2).astype(BF16)

        h = _ffn(h, vec_for(0), w_ffn1_in[l].astype(BF16), w_ffn1_out[l].astype(BF16), seq)
        h = _mixer(rel_bias.astype(F32), h, vec_for(1), bucket, sink_t, ln, bsp, ws_t,
                   w_in[l].astype(BF16),
                   w_br_attn[l].astype(BF16), w_br_gmlp[l].astype(BF16), w_out[l].astype(BF16), seq)
        h = _ffn(h, vec_for(2), w_ffn2_in[l].astype(BF16), w_ffn2_out[l].astype(BF16), seq)
    return h.reshape(bsz, seq, d)
```

```python
import functools
import math

import numpy as np
import jax
import jax.numpy as jnp
from jax import lax
from jax.experimental import pallas as pl
from jax.experimental.pallas import tpu as pltpu

D_MODEL = 1024
N_Q_HEADS = 8
N_KV_HEADS = 2
GROUP = N_Q_HEADS // N_KV_HEADS
HEAD_DIM = 64
BLOCK = 128
Q_W = N_Q_HEADS * HEAD_DIM
KV_W = N_KV_HEADS * HEAD_DIM
N_BUCKETS = 32
MAX_DISTANCE = 128
GMLP_GROUPS = 8
GMLP_WIDTH = 512
D_FF = 2816
FFN_RES = 0.5
N_ADA = 9
EPS = 1e-6
NEG = -1e30

LANES = 128
VMEM_LIMIT_BYTES = 56 * 1024 * 1024
FFN_TOKEN_TILE = 1024
FFN_SUB_TILE = 512
W_IN_STAGE_COLS = 512
W_OUT_STAGE_ROWS = 256
MIX_TOKEN_TILE = 1024
MIX_SUB_TILE = 512
FFN_CHUNKS = (512, 512, 512, 512, 512, 256)

BF16 = jnp.bfloat16
F32 = jnp.float32


def _dot(a, b):
    return jnp.dot(a, b, preferred_element_type=F32)


def _dot_nt(a, b):
    return lax.dot_general(a, b, (((1,), (1,)), ((), ())), preferred_element_type=F32)


def _resident(shape):
    nd = len(shape)
    return pl.BlockSpec(shape, lambda i: (0,) * nd, pipeline_mode=pl.Buffered(1))


def _prenorm_mod(x, vec):
    r = lax.rsqrt(jnp.mean(x * x, axis=-1, keepdims=True) + EPS)
    gs = vec[3:4] * (1.0 + vec[1:2])
    return x * r * gs + vec[0:1]


def _postnorm_residual(x, y, vec, res):
    r = lax.rsqrt(jnp.mean(y * y, axis=-1, keepdims=True) + EPS)
    return x + (res * vec[2:3]) * (y * r * vec[4:5])


def _ada_kernel(c_ref, w_ref, b_ref, o_ref):
    c = c_ref[...]
    s = c / (1.0 + jnp.exp(-c))
    w = w_ref[...]
    s_hi = s.astype(BF16)
    s_lo = (s - s_hi.astype(F32)).astype(BF16)
    w_hi = w.astype(BF16)
    w_lo = (w - w_hi.astype(F32)).astype(BF16)
    rows = s.shape[0]
    head = _dot(jnp.concatenate([s_hi, s_lo], axis=0), w_hi)
    o_ref[...] = head[:rows] + head[rows:] + _dot(s_hi, w_lo) + b_ref[...]


def _ada(c, w_ada, b_ada):
    b, d = c.shape
    n = w_ada.shape[1]
    rows = 16
    tn = n // 8
    c_pad = jnp.zeros((rows, d), F32).at[:b].set(c)
    out = pl.pallas_call(
        _ada_kernel,
        grid=(n // tn,),
        in_specs=[pl.BlockSpec((rows, d), lambda j: (0, 0)),
                  pl.BlockSpec((d, tn), lambda j: (0, j)),
                  pl.BlockSpec((1, tn), lambda j: (0, j))],
        out_specs=pl.BlockSpec((rows, tn), lambda j: (0, j)),
        out_shape=jax.ShapeDtypeStruct((rows, n), F32),
        compiler_params=pltpu.CompilerParams(dimension_semantics=("arbitrary",),
                                             vmem_limit_bytes=VMEM_LIMIT_BYTES),
        name="ada_proj",
    )(c_pad, w_ada, b_ada.reshape(1, n))
    return out[:b]


def _load_as_bf16(src_hbm, dst_ref, stage_ref, sem, index_of):
    def copy(c):
        return pltpu.make_async_copy(src_hbm.at[index_of[c]], stage_ref.at[c % 2], sem.at[c % 2])

    copy(0).start()
    for c in range(len(index_of)):
        if c + 1 < len(index_of):
            copy(c + 1).start()
        copy(c).wait()
        dst_ref[index_of[c]] = stage_ref[c % 2].astype(BF16)


def _ffn_kernel(h_ref, vec_ref, w_in_hbm, w_out_hbm, o_ref, w_in_ref, w_out_ref, stage_in, stage_out,
                sem_in, sem_out, *, sub, chunks, d_ff, res):
    @pl.when(pl.program_id(0) == 0)
    def _():
        cols, rows = stage_in.shape[2], stage_out.shape[1]
        _load_as_bf16(w_in_hbm, w_in_ref, stage_in, sem_in,
                      [(slice(None), pl.ds(c, cols)) for c in range(0, w_in_ref.shape[1], cols)])
        _load_as_bf16(w_out_hbm, w_out_ref, stage_out, sem_out,
                      [(pl.ds(r, rows), slice(None)) for r in range(0, w_out_ref.shape[0], rows)])

    vec = vec_ref[0]
    starts = list(range(0, h_ref.shape[0], sub))

    def prenorm(r0):
        return _prenorm_mod(h_ref[r0:r0 + sub, :], vec).astype(BF16)

    def finish(r0, y):
        o_ref[r0:r0 + sub, :] = _postnorm_residual(h_ref[r0:r0 + sub, :], y, vec, res)

    n = prenorm(starts[0])
    pending = None
    for idx, r0 in enumerate(starts):
        y = None
        off = 0
        n_next = None
        for ci, width in enumerate(chunks):
            g = _dot(n, w_in_ref[:, off:off + width])
            u = _dot(n, w_in_ref[:, d_ff + off:d_ff + off + width])
            a = (g / (1.0 + jnp.exp(-g)) * u).astype(BF16)
            part = _dot(a, w_out_ref[off:off + width, :])
            y = part if y is None else y + part
            off += width
            if ci == 0 and pending is not None:
                finish(*pending)
                pending = None
            if ci == 1 and idx + 1 < len(starts):
                n_next = prenorm(starts[idx + 1])
        pending = (r0, y)
        n = n_next
    finish(*pending)


def _ffn(h, vec, w_in, w_out, seq_len):
    t, d = h.shape
    tm = min(FFN_TOKEN_TILE, seq_len)
    tiles_per_seq = seq_len // tm
    d_ff = w_out.shape[0]
    kern = functools.partial(_ffn_kernel, sub=min(FFN_SUB_TILE, tm), chunks=FFN_CHUNKS, d_ff=d_ff,
                             res=FFN_RES)
    return pl.pallas_call(
        kern,
        grid=(t // tm,),
        in_specs=[pl.BlockSpec((tm, d), lambda i: (i, 0)),
                  pl.BlockSpec((1, 8, d), lambda i: (i // tiles_per_seq, 0, 0)),
                  pl.BlockSpec(memory_space=pl.ANY),
                  pl.BlockSpec(memory_space=pl.ANY)],
        out_specs=pl.BlockSpec((tm, d), lambda i: (i, 0)),
        out_shape=jax.ShapeDtypeStruct((t, d), F32),
        scratch_shapes=[pltpu.VMEM(w_in.shape, BF16), pltpu.VMEM(w_out.shape, BF16),
                        pltpu.VMEM((2, w_in.shape[0], W_IN_STAGE_COLS), F32),
                        pltpu.VMEM((2, W_OUT_STAGE_ROWS, w_out.shape[1]), F32),
                        pltpu.SemaphoreType.DMA((2,)), pltpu.SemaphoreType.DMA((2,))],
        compiler_params=pltpu.CompilerParams(dimension_semantics=("arbitrary",),
                                             vmem_limit_bytes=VMEM_LIMIT_BYTES),
        name="ffn_half_step",
    )(h, vec, w_in, w_out)


def _mixer_kernel(rel_ref, h_ref, vec_ref, bucket_ref, sink_ref, ln_ref, bsp_ref, ws_ref, w_in_ref,
                  wba_ref, wbg_ref, wo_ref, o_ref, carry_ref, attn_ref, sp_ref, bias_ref,
                  *, tm, sub, tiles_per_seq):
    first = (pl.program_id(0) % tiles_per_seq) == 0

    @pl.when(pl.program_id(0) == 0)
    def _():
        bucket = bucket_ref[...]
        hits = [bucket == bk for bk in range(N_BUCKETS)]
        for hd in range(N_Q_HEADS):
            acc = jnp.full(bucket.shape, NEG, F32)
            for bk in range(N_BUCKETS):
                acc = jnp.where(hits[bk], rel_ref[bk, hd], acc)
            kvh, g = divmod(hd, GROUP)
            bias_ref[2 * kvh + g % 2, (g // 2) * BLOCK:(g // 2 + 1) * BLOCK, :] = acc

    @pl.when(first)
    def _():
        carry_ref[...] = jnp.zeros_like(carry_ref)

    for r in range(0, tm, sub):
        rows = pl.ds(r, sub)
        _mixer_tile(first if r == 0 else None, h_ref.at[rows], vec_ref, sink_ref, ln_ref, bsp_ref,
                    ws_ref, w_in_ref, wba_ref, wbg_ref, wo_ref, o_ref.at[rows], carry_ref,
                    attn_ref.at[rows], sp_ref.at[rows], bias_ref, tm=sub)


def _mixer_tile(first, h_ref, vec_ref, sink_ref, ln_ref, bsp_ref, ws_ref, w_in_ref, wba_ref, wbg_ref,
                wo_ref, o_ref, carry_ref, attn_ref, sp_ref, bias_ref, *, tm):
    nb = tm // BLOCK
    x = h_ref[...]
    vec = vec_ref[0]
    n = _prenorm_mod(x, vec).astype(BF16)

    lo = lax.broadcasted_iota(jnp.int32, (1, LANES), 1) < HEAD_DIM

    q = (_dot(n, w_in_ref[:, 0:Q_W]) * (HEAD_DIM ** -0.5)).astype(BF16)
    w_kv = w_in_ref[:, Q_W:Q_W + 2 * KV_W]
    kv = jnp.concatenate([_dot(n[:tm // 2], w_kv), _dot(n[tm // 2:], w_kv)], axis=0)
    k_f, v_f = kv[:, :KV_W], kv[:, KV_W:]
    k_cur, k_rot = k_f.astype(BF16), pltpu.roll(k_f, HEAD_DIM, 1).astype(BF16)
    v_cur, v_rot = v_f.astype(BF16), pltpu.roll(v_f, HEAD_DIM, 1).astype(BF16)
    k_all = jnp.concatenate([carry_ref[0], k_cur], axis=0)
    k_all_rot = jnp.concatenate([carry_ref[1], k_rot], axis=0)
    v_all = jnp.concatenate([carry_ref[2], v_cur], axis=0)
    v_all_rot = jnp.concatenate([carry_ref[3], v_rot], axis=0)
    carry_ref[0] = k_cur[tm - BLOCK:]
    carry_ref[1] = k_rot[tm - BLOCK:]
    carry_ref[2] = v_cur[tm - BLOCK:]
    carry_ref[3] = v_rot[tm - BLOCK:]
    zero = jnp.zeros((), BF16)
    k_var = ((jnp.where(lo, k_all, zero), jnp.where(lo, zero, k_all_rot)),
             (jnp.where(lo, k_all_rot, zero), jnp.where(lo, zero, k_all)))
    v_var = ((jnp.where(lo, v_all, zero), jnp.where(lo, zero, v_all_rot)),
             (jnp.where(lo, v_all_rot, zero), jnp.where(lo, zero, v_all)))

    col = lax.broadcasted_iota(jnp.int32, (1, 2 * BLOCK), 1)
    no_prev = None if first is None else jnp.where(jnp.logical_and(first, col < BLOCK), NEG, 0.0)

    scores = []
    for b in range(nb):
        r0 = b * BLOCK
        keys = slice(r0, r0 + 2 * BLOCK)
        per_block = []
        for kvh in range(N_KV_HEADS):
            c0 = kvh * GROUP * HEAD_DIM
            q2 = jnp.concatenate([q[r0:r0 + BLOCK, c0:c0 + LANES],
                                  q[r0:r0 + BLOCK, c0 + LANES:c0 + 2 * LANES]], axis=0)
            for par in range(2):
                s = _dot_nt(q2, k_var[kvh][par][keys]) + bias_ref[2 * kvh + par]
                per_block.append(s + no_prev if (b == 0 and no_prev is not None) else s)
        scores.append(jnp.stack(per_block))
    sink = sink_ref[...]
    g0 = Q_W + 2 * KV_W + 2 * GMLP_WIDTH

    def softmax_block(b):
        s = scores[b]
        m = jnp.max(s, axis=-1, keepdims=True)
        return jnp.exp(s - m).astype(BF16), jnp.exp(sink - m)

    soft = [None] * nb
    zg = _dot(n, w_in_ref[:, Q_W + 2 * KV_W:g0])
    for b in range(0, nb // 2):
        soft[b] = softmax_block(b)
    ga = _dot(n, w_in_ref[:, g0:g0 + D_MODEL])
    for b in range(nb // 2, nb):
        soft[b] = softmax_block(b)
    gg = _dot(n, w_in_ref[:, g0 + D_MODEL:g0 + 2 * D_MODEL])

    rows2 = lax.broadcasted_iota(jnp.int32, (4 * BLOCK, LANES), 0) < 2 * BLOCK
    ones_cols = jnp.where(rows2 == lo, 1.0, 0.0).astype(BF16)
    for b in range(nb):
        r0 = b * BLOCK
        keys = slice(r0, r0 + 2 * BLOCK)
        probs, sink_term = soft[b]
        for kvh in range(N_KV_HEADS):
            c0 = kvh * GROUP * HEAD_DIM
            p2 = jnp.concatenate([probs[2 * kvh], probs[2 * kvh + 1]], axis=1)
            v2 = jnp.concatenate([v_var[kvh][0][keys], v_var[kvh][1][keys]], axis=0)
            o2 = _dot(p2, jnp.concatenate([v2, ones_cols], axis=1))
            denom = o2[:, LANES:] + jnp.where(lo, sink_term[2 * kvh], sink_term[2 * kvh + 1])
            o2 = o2[:, :LANES] / denom
            attn_ref[r0:r0 + BLOCK, c0:c0 + LANES] = o2[:BLOCK].astype(BF16)
            attn_ref[r0:r0 + BLOCK, c0 + LANES:c0 + 2 * LANES] = o2[BLOCK:].astype(BF16)

    zg = 0.5 * zg * (1.0 + jnp.tanh(math.sqrt(2.0 / math.pi) * (zg + 0.044715 * (zg * zg * zg))))
    u, vr = zg[:, :GMLP_WIDTH], zg[:, GMLP_WIDTH:]
    ln = ln_ref[...]
    mu = jnp.mean(vr, axis=-1, keepdims=True)
    vc = vr - mu
    var = jnp.mean(vc * vc, axis=-1, keepdims=True)
    vg = (vc * lax.rsqrt(var + EPS) * ln[0:1] + ln[1:2]).astype(BF16)
    row = lax.broadcasted_iota(jnp.int32, (BLOCK, 2 * BLOCK), 0)
    col2 = lax.broadcasted_iota(jnp.int32, (BLOCK, 2 * BLOCK), 1)
    causal = jnp.where(col2 < BLOCK, col2, col2 - BLOCK) <= row
    for j in range(GMLP_GROUPS // 2):
        w_pair = jnp.where(causal, ws_ref[j], zero)
        pieces = [vg[c * BLOCK:(c + 1) * BLOCK, j * LANES:(j + 1) * LANES] for c in range(nb)]
        rhs = jnp.concatenate(
            [jnp.concatenate([jnp.where(lo, pc, zero) for pc in pieces], axis=1),
             jnp.concatenate([jnp.where(lo, zero, pc) for pc in pieces], axis=1)], axis=0)
        out = _dot(w_pair, rhs)
        for c in range(nb):
            sp_ref[c * BLOCK:(c + 1) * BLOCK, j * LANES:(j + 1) * LANES] = (
                out[:, c * LANES:(c + 1) * LANES])
    bsp = bsp_ref[...]
    gm = jnp.concatenate(
        [u[c * BLOCK:(c + 1) * BLOCK] * (sp_ref[c * BLOCK:(c + 1) * BLOCK, :] + bsp)
         for c in range(nb)], axis=0).astype(BF16)

    ya = _dot(attn_ref[...], wba_ref[...])
    yg = _dot(gm, wbg_ref[...])
    mix = (ya / (1.0 + jnp.exp(-ga)) + yg / (1.0 + jnp.exp(-gg))).astype(BF16)
    y = _dot(mix, wo_ref[...])
    o_ref[...] = _postnorm_residual(x, y, vec, 1.0)


def _mixer(rel, h, vec, bucket, sink_t, ln, bsp, ws_t, w_in, wba, wbg, wo, seq_len):
    t, d = h.shape
    tm = min(MIX_TOKEN_TILE, seq_len)
    tiles_per_seq = seq_len // tm
    kern = functools.partial(_mixer_kernel, tm=tm, sub=min(MIX_SUB_TILE, tm),
                             tiles_per_seq=tiles_per_seq)
    return pl.pallas_call(
        kern,
        grid=(t // tm,),
        in_specs=[pl.BlockSpec(memory_space=pltpu.SMEM),
                  pl.BlockSpec((tm, d), lambda i: (i, 0)),
                  pl.BlockSpec((1, 8, d), lambda i: (i // tiles_per_seq, 0, 0)),
                  _resident(bucket.shape), _resident(sink_t.shape), _resident(ln.shape),
                  _resident(bsp.shape), _resident(ws_t.shape), _resident(w_in.shape),
                  _resident(wba.shape), _resident(wbg.shape), _resident(wo.shape)],
        out_specs=pl.BlockSpec((tm, d), lambda i: (i, 0)),
        out_shape=jax.ShapeDtypeStruct((t, d), F32),
        scratch_shapes=[pltpu.VMEM((4, BLOCK, KV_W), BF16),
                        pltpu.VMEM((tm, Q_W), BF16),
                        pltpu.VMEM((tm, GMLP_WIDTH), F32),
                        pltpu.VMEM((4, 2 * BLOCK, 2 * BLOCK), F32)],
        compiler_params=pltpu.CompilerParams(dimension_semantics=("arbitrary",),
                                             vmem_limit_bytes=VMEM_LIMIT_BYTES),
        name="token_mixers",
    )(rel, h, vec, bucket, sink_t, ln, bsp, ws_t, w_in, wba, wbg, wo)


def _t5_bucket_table():
    qi = np.arange(BLOCK, dtype=np.int32)[:, None]
    kj = np.arange(2 * BLOCK, dtype=np.int32)[None, :]
    dist = qi + BLOCK - kj
    in_window = (dist >= 0) & (dist < BLOCK)
    dc = np.maximum(dist, 0)
    max_exact = N_BUCKETS // 2
    d_f = np.maximum(dc, max_exact).astype(np.float32)
    large = max_exact + (np.log(d_f / np.float32(max_exact)).astype(np.float32)
                         / np.float32(math.log(MAX_DISTANCE / max_exact))
                         * np.float32(N_BUCKETS - max_exact)).astype(np.int32)
    large = np.minimum(large, N_BUCKETS - 1)
    return np.where(in_window, np.where(dc < max_exact, dc, large), -1).astype(np.int32)


def _pair_rows(a):
    out = []
    for kvh in range(N_KV_HEADS):
        for par in range(2):
            out.append(jnp.concatenate([a[GROUP * kvh + par], a[GROUP * kvh + 2 + par]], axis=0))
    return jnp.stack(out)


def kernel(x, c, rel_bias, w_ada, b_ada, pre_norm_g, post_norm_g, w_ffn1_in, w_ffn1_out, w_in,
           sinks, gmlp_ln_g, gmlp_ln_b, gmlp_w_s, gmlp_b_s, w_br_attn, w_br_gmlp, w_out,
           w_ffn2_in, w_ffn2_out):
    bsz, seq, d = x.shape
    depth = w_ada.shape[0]
    bucket = jnp.asarray(_t5_bucket_table())
    h = x.reshape(bsz * seq, d)
    for l in range(depth):
        ada = _ada(c, w_ada[l], b_ada[l]).reshape(bsz, N_ADA, d)

        def vec_for(sub):
            g = jnp.broadcast_to(jnp.stack([pre_norm_g[l, sub], post_norm_g[l, sub]])[None],
                                 (bsz, 2, d))
            return jnp.concatenate([ada[:, 3 * sub:3 * sub + 3], g,
                                    jnp.zeros((bsz, 3, d), F32)], axis=1)

        sink_t = _pair_rows(jnp.broadcast_to(sinks[l].astype(F32)[:, None, None],
                                             (N_Q_HEADS, BLOCK, LANES)))
        ln = jnp.stack([gmlp_ln_g[l], gmlp_ln_b[l]])
        bsp = jnp.repeat(jnp.transpose(gmlp_b_s[l]), GMLP_WIDTH // GMLP_GROUPS, axis=1)
        ws_t = jnp.concatenate([gmlp_w_s[l, 0::2], gmlp_w_s[l, 1::2]], axis=2).astype(BF16)

        h = _ffn(h, vec_for(0), w_ffn1_in[l], w_ffn1_out[l], seq)
        h = _mixer(rel_bias.astype(F32), h, vec_for(1), bucket, sink_t, ln, bsp, ws_t,
                   w_in[l].astype(BF16),
                   w_br_attn[l].astype(BF16), w_br_gmlp[l].astype(BF16), w_out[l].astype(BF16), seq)
        h = _ffn(h, vec_for(2), w_ffn2_in[l], w_ffn2_out[l], seq)
    return h.reshape(bsz, seq, d)
```

```python
import functools
import math

import numpy as np
import jax
import jax.numpy as jnp
from jax import lax
from jax.experimental import pallas as pl
from jax.experimental.pallas import tpu as pltpu

D_MODEL = 1024
N_Q_HEADS = 8
N_KV_HEADS = 2
GROUP = N_Q_HEADS // N_KV_HEADS
HEAD_DIM = 64
BLOCK = 128
Q_W = N_Q_HEADS * HEAD_DIM
KV_W = N_KV_HEADS * HEAD_DIM
N_BUCKETS = 32
MAX_DISTANCE = 128
GMLP_GROUPS = 8
GMLP_WIDTH = 512
D_FF = 2816
FFN_RES = 0.5
N_ADA = 9
EPS = 1e-6
NEG = -1e30

LANES = 128
VMEM_LIMIT_BYTES = 56 * 1024 * 1024
FFN_TOKEN_TILE = 1024
FFN_SUB_TILE = 512
W_IN_STAGE_COLS = 512
W_OUT_STAGE_ROWS = 256
MIX_STAGE = 256
STAGE_SLOTS = 3
MIX_TOKEN_TILE = 1024
MIX_SUB_TILE = 512
FFN_CHUNKS = (512, 512, 512, 512, 512, 256)

BF16 = jnp.bfloat16
F32 = jnp.float32


def _dot(a, b):
    return jnp.dot(a, b, preferred_element_type=F32)


def _dot_nt(a, b):
    return lax.dot_general(a, b, (((1,), (1,)), ((), ())), preferred_element_type=F32)


def _resident(shape):
    nd = len(shape)
    return pl.BlockSpec(shape, lambda i: (0,) * nd, pipeline_mode=pl.Buffered(1))


def _prenorm_mod(x, vec):
    r = lax.rsqrt(jnp.mean(x * x, axis=-1, keepdims=True) + EPS)
    gs = vec[3:4] * (1.0 + vec[1:2])
    return x * r * gs + vec[0:1]


def _postnorm_residual(x, y, vec, res):
    r = lax.rsqrt(jnp.mean(y * y, axis=-1, keepdims=True) + EPS)
    return x + (res * vec[2:3]) * (y * r * vec[4:5])


def _ada_kernel(c_ref, w_ref, b_ref, o_ref):
    c = c_ref[...]
    s = c / (1.0 + jnp.exp(-c))
    w = w_ref[...]
    s_hi = s.astype(BF16)
    s_lo = (s - s_hi.astype(F32)).astype(BF16)
    w_hi = w.astype(BF16)
    w_lo = (w - w_hi.astype(F32)).astype(BF16)
    rows = s.shape[0]
    head = _dot(jnp.concatenate([s_hi, s_lo], axis=0), w_hi)
    o_ref[...] = head[:rows] + head[rows:] + _dot(s_hi, w_lo) + b_ref[...]


def _ada(c, w_ada, b_ada):
    b, d = c.shape
    n = w_ada.shape[1]
    rows = 16
    tn = n // 8
    c_pad = jnp.zeros((rows, d), F32).at[:b].set(c)
    out = pl.pallas_call(
        _ada_kernel,
        grid=(n // tn,),
        in_specs=[pl.BlockSpec((rows, d), lambda j: (0, 0)),
                  pl.BlockSpec((d, tn), lambda j: (0, j)),
                  pl.BlockSpec((1, tn), lambda j: (0, j))],
        out_specs=pl.BlockSpec((rows, tn), lambda j: (0, j)),
        out_shape=jax.ShapeDtypeStruct((rows, n), F32),
        compiler_params=pltpu.CompilerParams(dimension_semantics=("arbitrary",),
                                             vmem_limit_bytes=VMEM_LIMIT_BYTES),
        name="ada_proj",
    )(c_pad, w_ada, b_ada.reshape(1, n))
    return out[:b]


def _load_as_bf16(src_hbm, dst_ref, stage_ref, sem, index_of):
    slots = stage_ref.shape[0]

    def copy(c):
        return pltpu.make_async_copy(src_hbm.at[index_of[c]], stage_ref.at[c % slots],
                                     sem.at[c % slots])

    for c in range(min(slots - 1, len(index_of))):
        copy(c).start()
    for c in range(len(index_of)):
        if c + slots - 1 < len(index_of):
            copy(c + slots - 1).start()
        copy(c).wait()
        dst_ref[index_of[c]] = stage_ref[c % slots].astype(BF16)


def _ffn_kernel(h_ref, vec_ref, w_in_hbm, w_out_hbm, o_ref, w_in_ref, w_out_ref, stage_in, stage_out,
                sem_in, sem_out, *, sub, chunks, d_ff, res):
    @pl.when(pl.program_id(0) == 0)
    def _():
        cols, rows = stage_in.shape[2], stage_out.shape[1]
        _load_as_bf16(w_in_hbm, w_in_ref, stage_in, sem_in,
                      [(slice(None), pl.ds(c, cols)) for c in range(0, w_in_ref.shape[1], cols)])
        _load_as_bf16(w_out_hbm, w_out_ref, stage_out, sem_out,
                      [(pl.ds(r, rows), slice(None)) for r in range(0, w_out_ref.shape[0], rows)])

    vec = vec_ref[0]
    starts = list(range(0, h_ref.shape[0], sub))

    def prenorm(r0):
        return _prenorm_mod(h_ref[r0:r0 + sub, :], vec).astype(BF16)

    def finish(r0, y):
        o_ref[r0:r0 + sub, :] = _postnorm_residual(h_ref[r0:r0 + sub, :], y, vec, res)

    n = prenorm(starts[0])
    pending = None
    for idx, r0 in enumerate(starts):
        y = None
        off = 0
        n_next = None
        for ci, width in enumerate(chunks):
            g = _dot(n, w_in_ref[:, off:off + width])
            u = _dot(n, w_in_ref[:, d_ff + off:d_ff + off + width])
            a = (g / (1.0 + jnp.exp(-g)) * u).astype(BF16)
            part = _dot(a, w_out_ref[off:off + width, :])
            y = part if y is None else y + part
            off += width
            if ci == 0 and pending is not None:
                finish(*pending)
                pending = None
            if ci == 1 and idx + 1 < len(starts):
                n_next = prenorm(starts[idx + 1])
        pending = (r0, y)
        n = n_next
    finish(*pending)


def _ffn(h, vec, w_in, w_out, seq_len):
    t, d = h.shape
    tm = min(FFN_TOKEN_TILE, seq_len)
    tiles_per_seq = seq_len // tm
    d_ff = w_out.shape[0]
    kern = functools.partial(_ffn_kernel, sub=min(FFN_SUB_TILE, tm), chunks=FFN_CHUNKS, d_ff=d_ff,
                             res=FFN_RES)
    return pl.pallas_call(
        kern,
        grid=(t // tm,),
        in_specs=[pl.BlockSpec((tm, d), lambda i: (i, 0)),
                  pl.BlockSpec((1, 8, d), lambda i: (i // tiles_per_seq, 0, 0)),
                  pl.BlockSpec(memory_space=pl.ANY),
                  pl.BlockSpec(memory_space=pl.ANY)],
        out_specs=pl.BlockSpec((tm, d), lambda i: (i, 0)),
        out_shape=jax.ShapeDtypeStruct((t, d), F32),
        scratch_shapes=[pltpu.VMEM(w_in.shape, BF16), pltpu.VMEM(w_out.shape, BF16),
                        pltpu.VMEM((STAGE_SLOTS, w_in.shape[0], W_IN_STAGE_COLS), F32),
                        pltpu.VMEM((STAGE_SLOTS, W_OUT_STAGE_ROWS, w_out.shape[1]), F32),
                        pltpu.SemaphoreType.DMA((STAGE_SLOTS,)),
                        pltpu.SemaphoreType.DMA((STAGE_SLOTS,))],
        compiler_params=pltpu.CompilerParams(dimension_semantics=("arbitrary",),
                                             vmem_limit_bytes=VMEM_LIMIT_BYTES),
        name="ffn_half_step",
    )(h, vec, w_in, w_out)


def _mixer_kernel(rel_ref, h_ref, vec_ref, bucket_ref, sink_ref, ln_ref, bsp_ref, ws_ref, w_in_hbm,
                  wba_hbm, wbg_hbm, wo_hbm, o_ref, carry_ref, attn_ref, sp_ref, bias_ref,
                  w_in_ref, wba_ref, wbg_ref, wo_ref, stage_cols, stage_rows, sem_cols, sem_rows,
                  *, tm, sub, tiles_per_seq):
    first = (pl.program_id(0) % tiles_per_seq) == 0

    @pl.when(pl.program_id(0) == 0)
    def _():
        step = stage_cols.shape[2]
        _load_as_bf16(w_in_hbm, w_in_ref, stage_cols, sem_cols,
                      [(slice(None), pl.ds(c, step)) for c in range(0, w_in_ref.shape[1], step)])
        for src, dst in ((wba_hbm, wba_ref), (wbg_hbm, wbg_ref), (wo_hbm, wo_ref)):
            _load_as_bf16(src, dst, stage_rows, sem_rows,
                          [(pl.ds(r, step), slice(None)) for r in range(0, dst.shape[0], step)])

        bucket = bucket_ref[...]
        hits = [bucket == bk for bk in range(N_BUCKETS)]
        for hd in range(N_Q_HEADS):
            acc = jnp.full(bucket.shape, NEG, F32)
            for bk in range(N_BUCKETS):
                acc = jnp.where(hits[bk], rel_ref[bk, hd], acc)
            kvh, g = divmod(hd, GROUP)
            bias_ref[2 * kvh + g % 2, (g // 2) * BLOCK:(g // 2 + 1) * BLOCK, :] = acc

    @pl.when(first)
    def _():
        carry_ref[...] = jnp.zeros_like(carry_ref)

    for r in range(0, tm, sub):
        rows = pl.ds(r, sub)
        _mixer_tile(first if r == 0 else None, h_ref.at[rows], vec_ref, sink_ref, ln_ref, bsp_ref,
                    ws_ref, w_in_ref, wba_ref, wbg_ref, wo_ref, o_ref.at[rows], carry_ref,
                    attn_ref.at[rows], sp_ref.at[rows], bias_ref, tm=sub)


def _mixer_tile(first, h_ref, vec_ref, sink_ref, ln_ref, bsp_ref, ws_ref, w_in_ref, wba_ref, wbg_ref,
                wo_ref, o_ref, carry_ref, attn_ref, sp_ref, bias_ref, *, tm):
    nb = tm // BLOCK
    x = h_ref[...]
    vec = vec_ref[0]
    n = _prenorm_mod(x, vec).astype(BF16)

    lo = lax.broadcasted_iota(jnp.int32, (1, LANES), 1) < HEAD_DIM

    q = (_dot(n, w_in_ref[:, 0:Q_W]) * (HEAD_DIM ** -0.5)).astype(BF16)
    w_kv = w_in_ref[:, Q_W:Q_W + 2 * KV_W]
    kv = jnp.concatenate([_dot(n[:tm // 2], w_kv), _dot(n[tm // 2:], w_kv)], axis=0)
    k_f, v_f = kv[:, :KV_W], kv[:, KV_W:]
    k_cur, k_rot = k_f.astype(BF16), pltpu.roll(k_f, HEAD_DIM, 1).astype(BF16)
    v_cur, v_rot = v_f.astype(BF16), pltpu.roll(v_f, HEAD_DIM, 1).astype(BF16)
    k_all = jnp.concatenate([carry_ref[0], k_cur], axis=0)
    k_all_rot = jnp.concatenate([carry_ref[1], k_rot], axis=0)
    v_all = jnp.concatenate([carry_ref[2], v_cur], axis=0)
    v_all_rot = jnp.concatenate([carry_ref[3], v_rot], axis=0)
    carry_ref[0] = k_cur[tm - BLOCK:]
    carry_ref[1] = k_rot[tm - BLOCK:]
    carry_ref[2] = v_cur[tm - BLOCK:]
    carry_ref[3] = v_rot[tm - BLOCK:]
    zero = jnp.zeros((), BF16)
    k_var = ((jnp.where(lo, k_all, zero), jnp.where(lo, zero, k_all_rot)),
             (jnp.where(lo, k_all_rot, zero), jnp.where(lo, zero, k_all)))
    v_var = ((jnp.where(lo, v_all, zero), jnp.where(lo, zero, v_all_rot)),
             (jnp.where(lo, v_all_rot, zero), jnp.where(lo, zero, v_all)))

    col = lax.broadcasted_iota(jnp.int32, (1, 2 * BLOCK), 1)
    no_prev = None if first is None else jnp.where(jnp.logical_and(first, col < BLOCK), NEG, 0.0)

    scores = []
    for b in range(nb):
        r0 = b * BLOCK
        keys = slice(r0, r0 + 2 * BLOCK)
        per_block = []
        for kvh in range(N_KV_HEADS):
            c0 = kvh * GROUP * HEAD_DIM
            q2 = jnp.concatenate([q[r0:r0 + BLOCK, c0:c0 + LANES],
                                  q[r0:r0 + BLOCK, c0 + LANES:c0 + 2 * LANES]], axis=0)
            for par in range(2):
                s = _dot_nt(q2, k_var[kvh][par][keys]) + bias_ref[2 * kvh + par]
                per_block.append(s + no_prev if (b == 0 and no_prev is not None) else s)
        scores.append(jnp.stack(per_block))
    sink = sink_ref[...]
    g0 = Q_W + 2 * KV_W + 2 * GMLP_WIDTH

    def softmax_block(b):
        s = scores[b]
        m = jnp.max(s, axis=-1, keepdims=True)
        return jnp.exp(s - m).astype(BF16), jnp.exp(sink - m)

    soft = [None] * nb
    zg = _dot(n, w_in_ref[:, Q_W + 2 * KV_W:g0])
    for b in range(0, nb // 2):
        soft[b] = softmax_block(b)
    ga = _dot(n, w_in_ref[:, g0:g0 + D_MODEL])
    for b in range(nb // 2, nb):
        soft[b] = softmax_block(b)
    gg = _dot(n, w_in_ref[:, g0 + D_MODEL:g0 + 2 * D_MODEL])

    rows2 = lax.broadcasted_iota(jnp.int32, (4 * BLOCK, LANES), 0) < 2 * BLOCK
    ones_cols = jnp.where(rows2 == lo, 1.0, 0.0).astype(BF16)
    for b in range(nb):
        r0 = b * BLOCK
        keys = slice(r0, r0 + 2 * BLOCK)
        probs, sink_term = soft[b]
        for kvh in range(N_KV_HEADS):
            c0 = kvh * GROUP * HEAD_DIM
            p2 = jnp.concatenate([probs[2 * kvh], probs[2 * kvh + 1]], axis=1)
            v2 = jnp.concatenate([v_var[kvh][0][keys], v_var[kvh][1][keys]], axis=0)
            o2 = _dot(p2, jnp.concatenate([v2, ones_cols], axis=1))
            denom = o2[:, LANES:] + jnp.where(lo, sink_term[2 * kvh], sink_term[2 * kvh + 1])
            o2 = o2[:, :LANES] / denom
            attn_ref[r0:r0 + BLOCK, c0:c0 + LANES] = o2[:BLOCK].astype(BF16)
            attn_ref[r0:r0 + BLOCK, c0 + LANES:c0 + 2 * LANES] = o2[BLOCK:].astype(BF16)

    zg = 0.5 * zg * (1.0 + jnp.tanh(math.sqrt(2.0 / math.pi) * (zg + 0.044715 * (zg * zg * zg))))
    u, vr = zg[:, :GMLP_WIDTH], zg[:, GMLP_WIDTH:]
    ln = ln_ref[...]
    mu = jnp.mean(vr, axis=-1, keepdims=True)
    vc = vr - mu
    var = jnp.mean(vc * vc, axis=-1, keepdims=True)
    vg = (vc * lax.rsqrt(var + EPS) * ln[0:1] + ln[1:2]).astype(BF16)
    row = lax.broadcasted_iota(jnp.int32, (BLOCK, 2 * BLOCK), 0)
    col2 = lax.broadcasted_iota(jnp.int32, (BLOCK, 2 * BLOCK), 1)
    causal = jnp.where(col2 < BLOCK, col2, col2 - BLOCK) <= row
    for j in range(GMLP_GROUPS // 2):
        w_pair = jnp.where(causal, ws_ref[j], zero)
        pieces = [vg[c * BLOCK:(c + 1) * BLOCK, j * LANES:(j + 1) * LANES] for c in range(nb)]
        rhs = jnp.concatenate(
            [jnp.concatenate([jnp.where(lo, pc, zero) for pc in pieces], axis=1),
             jnp.concatenate([jnp.where(lo, zero, pc) for pc in pieces], axis=1)], axis=0)
        out = _dot(w_pair, rhs)
        for c in range(nb):
            sp_ref[c * BLOCK:(c + 1) * BLOCK, j * LANES:(j + 1) * LANES] = (
                out[:, c * LANES:(c + 1) * LANES])
    bsp = bsp_ref[...]
    gm = jnp.concatenate(
        [u[c * BLOCK:(c + 1) * BLOCK] * (sp_ref[c * BLOCK:(c + 1) * BLOCK, :] + bsp)
         for c in range(nb)], axis=0).astype(BF16)

    ya = _dot(attn_ref[...], wba_ref[...])
    yg = _dot(gm, wbg_ref[...])
    mix = (ya / (1.0 + jnp.exp(-ga)) + yg / (1.0 + jnp.exp(-gg))).astype(BF16)
    y = _dot(mix, wo_ref[...])
    o_ref[...] = _postnorm_residual(x, y, vec, 1.0)


def _mixer(rel, h, vec, bucket, sink_t, ln, bsp, ws_t, w_in, wba, wbg, wo, seq_len):
    t, d = h.shape
    tm = min(MIX_TOKEN_TILE, seq_len)
    tiles_per_seq = seq_len // tm
    kern = functools.partial(_mixer_kernel, tm=tm, sub=min(MIX_SUB_TILE, tm),
                             tiles_per_seq=tiles_per_seq)
    return pl.pallas_call(
        kern,
        grid=(t // tm,),
        in_specs=[pl.BlockSpec(memory_space=pltpu.SMEM),
                  pl.BlockSpec((tm, d), lambda i: (i, 0)),
                  pl.BlockSpec((1, 8, d), lambda i: (i // tiles_per_seq, 0, 0)),
                  _resident(bucket.shape), _resident(sink_t.shape), _resident(ln.shape),
                  _resident(bsp.shape), _resident(ws_t.shape)] + [pl.BlockSpec(memory_space=pl.ANY)] * 4,
        out_specs=pl.BlockSpec((tm, d), lambda i: (i, 0)),
        out_shape=jax.ShapeDtypeStruct((t, d), F32),
        scratch_shapes=[pltpu.VMEM((4, BLOCK, KV_W), BF16),
                        pltpu.VMEM((tm, Q_W), BF16),
                        pltpu.VMEM((tm, GMLP_WIDTH), F32),
                        pltpu.VMEM((4, 2 * BLOCK, 2 * BLOCK), F32),
                        pltpu.VMEM(w_in.shape, BF16), pltpu.VMEM(wba.shape, BF16),
                        pltpu.VMEM(wbg.shape, BF16), pltpu.VMEM(wo.shape, BF16),
                        pltpu.VMEM((STAGE_SLOTS, w_in.shape[0], MIX_STAGE), F32),
                        pltpu.VMEM((STAGE_SLOTS, MIX_STAGE, d), F32),
                        pltpu.SemaphoreType.DMA((STAGE_SLOTS,)),
                        pltpu.SemaphoreType.DMA((STAGE_SLOTS,))],
        compiler_params=pltpu.CompilerParams(dimension_semantics=("arbitrary",),
                                             vmem_limit_bytes=VMEM_LIMIT_BYTES),
        name="token_mixers",
    )(rel, h, vec, bucket, sink_t, ln, bsp, ws_t, w_in, wba, wbg, wo)


def _t5_bucket_table():
    qi = np.arange(BLOCK, dtype=np.int32)[:, None]
    kj = np.arange(2 * BLOCK, dtype=np.int32)[None, :]
    dist = qi + BLOCK - kj
    in_window = (dist >= 0) & (dist < BLOCK)
    dc = np.maximum(dist, 0)
    max_exact = N_BUCKETS // 2
    d_f = np.maximum(dc, max_exact).astype(np.float32)
    large = max_exact + (np.log(d_f / np.float32(max_exact)).astype(np.float32)
                         / np.float32(math.log(MAX_DISTANCE / max_exact))
                         * np.float32(N_BUCKETS - max_exact)).astype(np.int32)
    large = np.minimum(large, N_BUCKETS - 1)
    return np.where(in_window, np.where(dc < max_exact, dc, large), -1).astype(np.int32)


def _pair_rows(a):
    out = []
    for kvh in range(N_KV_HEADS):
        for par in range(2):
            out.append(jnp.concatenate([a[GROUP * kvh + par], a[GROUP * kvh + 2 + par]], axis=0))
    return jnp.stack(out)


def kernel(x, c, rel_bias, w_ada, b_ada, pre_norm_g, post_norm_g, w_ffn1_in, w_ffn1_out, w_in,
           sinks, gmlp_ln_g, gmlp_ln_b, gmlp_w_s, gmlp_b_s, w_br_attn, w_br_gmlp, w_out,
           w_ffn2_in, w_ffn2_out):
    bsz, seq, d = x.shape
    depth = w_ada.shape[0]
    bucket = jnp.asarray(_t5_bucket_table())
    h = x.reshape(bsz * seq, d)
    for l in range(depth):
        ada = _ada(c, w_ada[l], b_ada[l]).reshape(bsz, N_ADA, d)

        def vec_for(sub):
            g = jnp.broadcast_to(jnp.stack([pre_norm_g[l, sub], post_norm_g[l, sub]])[None],
                                 (bsz, 2, d))
            return jnp.concatenate([ada[:, 3 * sub:3 * sub + 3], g,
                                    jnp.zeros((bsz, 3, d), F32)], axis=1)

        sink_t = _pair_rows(jnp.broadcast_to(sinks[l].astype(F32)[:, None, None],
                                             (N_Q_HEADS, BLOCK, LANES)))
        ln = jnp.stack([gmlp_ln_g[l], gmlp_ln_b[l]])
        bsp = jnp.repeat(jnp.transpose(gmlp_b_s[l]), GMLP_WIDTH // GMLP_GROUPS, axis=1)
        ws_t = jnp.concatenate([gmlp_w_s[l, 0::2], gmlp_w_s[l, 1::2]], axis=2).astype(BF16)

        h = _ffn(h, vec_for(0), w_ffn1_in[l], w_ffn1_out[l], seq)
        h = _mixer(rel_bias.astype(F32), h, vec_for(1), bucket, sink_t, ln, bsp, ws_t,
                   w_in[l], w_br_attn[l], w_br_gmlp[l], w_out[l], seq)
        h = _ffn(h, vec_for(2), w_ffn2_in[l], w_ffn2_out[l], seq)
    return h.reshape(bsz, seq, d)
```

```python
import functools
import math

import numpy as np
import jax
import jax.numpy as jnp
from jax import lax
from jax.experimental import pallas as pl
from jax.experimental.pallas import tpu as pltpu

D_MODEL = 1024
N_Q_HEADS = 8
N_KV_HEADS = 2
GROUP = N_Q_HEADS // N_KV_HEADS
HEAD_DIM = 64
BLOCK = 128
Q_W = N_Q_HEADS * HEAD_DIM
KV_W = N_KV_HEADS * HEAD_DIM
N_BUCKETS = 32
MAX_DISTANCE = 128
GMLP_GROUPS = 8
GMLP_WIDTH = 512
D_FF = 2816
FFN_RES = 0.5
N_ADA = 9
EPS = 1e-6
NEG = -1e30

LANES = 128
VMEM_LIMIT_BYTES = 56 * 1024 * 1024
FFN_TOKEN_TILE = 1024
FFN_SUB_TILE = 512
WIDE_STAGE_ROWS = 64
STAGE_ROWS = 256
STAGE_SLOTS = 3
MIX_TOKEN_TILE = 1024
MIX_SUB_TILE = 512
FFN_CHUNKS = (512, 512, 512, 512, 512, 256)

BF16 = jnp.bfloat16
F32 = jnp.float32


def _dot(a, b):
    return jnp.dot(a, b, preferred_element_type=F32)


def _dot_nt(a, b):
    return lax.dot_general(a, b, (((1,), (1,)), ((), ())), preferred_element_type=F32)


def _resident(shape):
    nd = len(shape)
    return pl.BlockSpec(shape, lambda i: (0,) * nd, pipeline_mode=pl.Buffered(1))


def _prenorm_mod(x, vec):
    r = lax.rsqrt(jnp.mean(x * x, axis=-1, keepdims=True) + EPS)
    gs = vec[3:4] * (1.0 + vec[1:2])
    return x * r * gs + vec[0:1]


def _postnorm_residual(x, y, vec, res):
    r = lax.rsqrt(jnp.mean(y * y, axis=-1, keepdims=True) + EPS)
    return x + (res * vec[2:3]) * (y * r * vec[4:5])


def _ada_kernel(c_ref, w_ref, b_ref, o_ref):
    c = c_ref[...]
    s = c / (1.0 + jnp.exp(-c))
    w = w_ref[...]
    s_hi = s.astype(BF16)
    s_lo = (s - s_hi.astype(F32)).astype(BF16)
    w_hi = w.astype(BF16)
    w_lo = (w - w_hi.astype(F32)).astype(BF16)
    rows = s.shape[0]
    head = _dot(jnp.concatenate([s_hi, s_lo], axis=0), w_hi)
    o_ref[...] = head[:rows] + head[rows:] + _dot(s_hi, w_lo) + b_ref[...]


def _ada(c, w_ada, b_ada):
    b, d = c.shape
    n = w_ada.shape[1]
    rows = 16
    tn = n // 8
    c_pad = jnp.zeros((rows, d), F32).at[:b].set(c)
    out = pl.pallas_call(
        _ada_kernel,
        grid=(n // tn,),
        in_specs=[pl.BlockSpec((rows, d), lambda j: (0, 0)),
                  pl.BlockSpec((d, tn), lambda j: (0, j)),
                  pl.BlockSpec((1, tn), lambda j: (0, j))],
        out_specs=pl.BlockSpec((rows, tn), lambda j: (0, j)),
        out_shape=jax.ShapeDtypeStruct((rows, n), F32),
        compiler_params=pltpu.CompilerParams(dimension_semantics=("arbitrary",),
                                             vmem_limit_bytes=VMEM_LIMIT_BYTES),
        name="ada_proj",
    )(c_pad, w_ada, b_ada.reshape(1, n))
    return out[:b]


def _load_as_bf16(src_hbm, dst_ref, stage_ref, sem):
    slots, rows = stage_ref.shape[0], stage_ref.shape[1]
    n_chunks = dst_ref.shape[0] // rows

    def copy(c):
        return pltpu.make_async_copy(src_hbm.at[pl.ds(c * rows, rows)], stage_ref.at[c % slots],
                                     sem.at[c % slots])

    for c in range(min(slots - 1, n_chunks)):
        copy(c).start()
    for c in range(n_chunks):
        if c + slots - 1 < n_chunks:
            copy(c + slots - 1).start()
        copy(c).wait()
        dst_ref[pl.ds(c * rows, rows), :] = stage_ref[c % slots].astype(BF16)


def _ffn_kernel(h_ref, vec_ref, w_in_hbm, w_out_hbm, o_ref, w_in_ref, w_out_ref, stage_in, stage_out,
                sem_in, sem_out, *, sub, chunks, d_ff, res):
    @pl.when(pl.program_id(0) == 0)
    def _():
        _load_as_bf16(w_in_hbm, w_in_ref, stage_in, sem_in)
        _load_as_bf16(w_out_hbm, w_out_ref, stage_out, sem_out)

    vec = vec_ref[0]
    starts = list(range(0, h_ref.shape[0], sub))

    def prenorm(r0):
        return _prenorm_mod(h_ref[r0:r0 + sub, :], vec).astype(BF16)

    def finish(r0, y):
        o_ref[r0:r0 + sub, :] = _postnorm_residual(h_ref[r0:r0 + sub, :], y, vec, res)

    n = prenorm(starts[0])
    pending = None
    for idx, r0 in enumerate(starts):
        y = None
        off = 0
        n_next = None
        for ci, width in enumerate(chunks):
            g = _dot(n, w_in_ref[:, off:off + width])
            u = _dot(n, w_in_ref[:, d_ff + off:d_ff + off + width])
            a = (g / (1.0 + jnp.exp(-g)) * u).astype(BF16)
            part = _dot(a, w_out_ref[off:off + width, :])
            y = part if y is None else y + part
            off += width
            if ci == 0 and pending is not None:
                finish(*pending)
                pending = None
            if ci == 1 and idx + 1 < len(starts):
                n_next = prenorm(starts[idx + 1])
        pending = (r0, y)
        n = n_next
    finish(*pending)


def _ffn(h, vec, w_in, w_out, seq_len):
    t, d = h.shape
    tm = min(FFN_TOKEN_TILE, seq_len)
    tiles_per_seq = seq_len // tm
    d_ff = w_out.shape[0]
    kern = functools.partial(_ffn_kernel, sub=min(FFN_SUB_TILE, tm), chunks=FFN_CHUNKS, d_ff=d_ff,
                             res=FFN_RES)
    return pl.pallas_call(
        kern,
        grid=(t // tm,),
        in_specs=[pl.BlockSpec((tm, d), lambda i: (i, 0)),
                  pl.BlockSpec((1, 8, d), lambda i: (i // tiles_per_seq, 0, 0)),
                  pl.BlockSpec(memory_space=pl.ANY),
                  pl.BlockSpec(memory_space=pl.ANY)],
        out_specs=pl.BlockSpec((tm, d), lambda i: (i, 0)),
        out_shape=jax.ShapeDtypeStruct((t, d), F32),
        scratch_shapes=[pltpu.VMEM(w_in.shape, BF16), pltpu.VMEM(w_out.shape, BF16),
                        pltpu.VMEM((STAGE_SLOTS, WIDE_STAGE_ROWS, w_in.shape[1]), F32),
                        pltpu.VMEM((STAGE_SLOTS, STAGE_ROWS, w_out.shape[1]), F32),
                        pltpu.SemaphoreType.DMA((STAGE_SLOTS,)),
                        pltpu.SemaphoreType.DMA((STAGE_SLOTS,))],
        compiler_params=pltpu.CompilerParams(dimension_semantics=("arbitrary",),
                                             vmem_limit_bytes=VMEM_LIMIT_BYTES),
        name="ffn_half_step",
    )(h, vec, w_in, w_out)


def _mixer_kernel(rel_ref, h_ref, vec_ref, bucket_ref, sink_ref, ln_ref, bsp_ref, ws_ref, w_in_hbm,
                  wba_hbm, wbg_hbm, wo_hbm, o_ref, carry_ref, attn_ref, sp_ref, bias_ref,
                  w_in_ref, wba_ref, wbg_ref, wo_ref, stage_wide, stage_rows, sem_wide, sem_rows,
                  *, tm, sub, tiles_per_seq):
    first = (pl.program_id(0) % tiles_per_seq) == 0

    @pl.when(pl.program_id(0) == 0)
    def _():
        _load_as_bf16(w_in_hbm, w_in_ref, stage_wide, sem_wide)
        for src, dst in ((wba_hbm, wba_ref), (wbg_hbm, wbg_ref), (wo_hbm, wo_ref)):
            _load_as_bf16(src, dst, stage_rows, sem_rows)

        bucket = bucket_ref[...]
        hits = [bucket == bk for bk in range(N_BUCKETS)]
        for hd in range(N_Q_HEADS):
            acc = jnp.full(bucket.shape, NEG, F32)
            for bk in range(N_BUCKETS):
                acc = jnp.where(hits[bk], rel_ref[bk, hd], acc)
            kvh, g = divmod(hd, GROUP)
            bias_ref[2 * kvh + g % 2, (g // 2) * BLOCK:(g // 2 + 1) * BLOCK, :] = acc

    @pl.when(first)
    def _():
        carry_ref[...] = jnp.zeros_like(carry_ref)

    for r in range(0, tm, sub):
        rows = pl.ds(r, sub)
        _mixer_tile(first if r == 0 else None, h_ref.at[rows], vec_ref, sink_ref, ln_ref, bsp_ref,
                    ws_ref, w_in_ref, wba_ref, wbg_ref, wo_ref, o_ref.at[rows], carry_ref,
                    attn_ref.at[rows], sp_ref.at[rows], bias_ref, tm=sub)


def _mixer_tile(first, h_ref, vec_ref, sink_ref, ln_ref, bsp_ref, ws_ref, w_in_ref, wba_ref, wbg_ref,
                wo_ref, o_ref, carry_ref, attn_ref, sp_ref, bias_ref, *, tm):
    nb = tm // BLOCK
    x = h_ref[...]
    vec = vec_ref[0]
    n = _prenorm_mod(x, vec).astype(BF16)

    lo = lax.broadcasted_iota(jnp.int32, (1, LANES), 1) < HEAD_DIM

    q = (_dot(n, w_in_ref[:, 0:Q_W]) * (HEAD_DIM ** -0.5)).astype(BF16)
    w_kv = w_in_ref[:, Q_W:Q_W + 2 * KV_W]
    kv = jnp.concatenate([_dot(n[:tm // 2], w_kv), _dot(n[tm // 2:], w_kv)], axis=0)
    k_f, v_f = kv[:, :KV_W], kv[:, KV_W:]
    k_cur, k_rot = k_f.astype(BF16), pltpu.roll(k_f, HEAD_DIM, 1).astype(BF16)
    v_cur, v_rot = v_f.astype(BF16), pltpu.roll(v_f, HEAD_DIM, 1).astype(BF16)
    k_all = jnp.concatenate([carry_ref[0], k_cur], axis=0)
    k_all_rot = jnp.concatenate([carry_ref[1], k_rot], axis=0)
    v_all = jnp.concatenate([carry_ref[2], v_cur], axis=0)
    v_all_rot = jnp.concatenate([carry_ref[3], v_rot], axis=0)
    carry_ref[0] = k_cur[tm - BLOCK:]
    carry_ref[1] = k_rot[tm - BLOCK:]
    carry_ref[2] = v_cur[tm - BLOCK:]
    carry_ref[3] = v_rot[tm - BLOCK:]
    zero = jnp.zeros((), BF16)
    k_var = ((jnp.where(lo, k_all, zero), jnp.where(lo, zero, k_all_rot)),
             (jnp.where(lo, k_all_rot, zero), jnp.where(lo, zero, k_all)))
    v_var = ((jnp.where(lo, v_all, zero), jnp.where(lo, zero, v_all_rot)),
             (jnp.where(lo, v_all_rot, zero), jnp.where(lo, zero, v_all)))

    col = lax.broadcasted_iota(jnp.int32, (1, 2 * BLOCK), 1)
    no_prev = None if first is None else jnp.where(jnp.logical_and(first, col < BLOCK), NEG, 0.0)

    scores = []
    for b in range(nb):
        r0 = b * BLOCK
        keys = slice(r0, r0 + 2 * BLOCK)
        per_block = []
        for kvh in range(N_KV_HEADS):
            c0 = kvh * GROUP * HEAD_DIM
            q2 = jnp.concatenate([q[r0:r0 + BLOCK, c0:c0 + LANES],
                                  q[r0:r0 + BLOCK, c0 + LANES:c0 + 2 * LANES]], axis=0)
            for par in range(2):
                s = _dot_nt(q2, k_var[kvh][par][keys]) + bias_ref[2 * kvh + par]
                per_block.append(s + no_prev if (b == 0 and no_prev is not None) else s)
        scores.append(jnp.stack(per_block))
    sink = sink_ref[...]
    g0 = Q_W + 2 * KV_W + 2 * GMLP_WIDTH

    def softmax_block(b):
        s = scores[b]
        m = jnp.max(s, axis=-1, keepdims=True)
        return jnp.exp(s - m).astype(BF16), jnp.exp(sink - m)

    soft = [None] * nb
    zg = _dot(n, w_in_ref[:, Q_W + 2 * KV_W:g0])
    for b in range(0, nb // 2):
        soft[b] = softmax_block(b)
    ga = _dot(n, w_in_ref[:, g0:g0 + D_MODEL])
    for b in range(nb // 2, nb):
        soft[b] = softmax_block(b)
    gg = _dot(n, w_in_ref[:, g0 + D_MODEL:g0 + 2 * D_MODEL])

    rows2 = lax.broadcasted_iota(jnp.int32, (4 * BLOCK, LANES), 0) < 2 * BLOCK
    ones_cols = jnp.where(rows2 == lo, 1.0, 0.0).astype(BF16)
    for b in range(nb):
        r0 = b * BLOCK
        keys = slice(r0, r0 + 2 * BLOCK)
        probs, sink_term = soft[b]
        for kvh in range(N_KV_HEADS):
            c0 = kvh * GROUP * HEAD_DIM
            p2 = jnp.concatenate([probs[2 * kvh], probs[2 * kvh + 1]], axis=1)
            v2 = jnp.concatenate([v_var[kvh][0][keys], v_var[kvh][1][keys]], axis=0)
            o2 = _dot(p2, jnp.concatenate([v2, ones_cols], axis=1))
            denom = o2[:, LANES:] + jnp.where(lo, sink_term[2 * kvh], sink_term[2 * kvh + 1])
            o2 = o2[:, :LANES] / denom
            attn_ref[r0:r0 + BLOCK, c0:c0 + LANES] = o2[:BLOCK].astype(BF16)
            attn_ref[r0:r0 + BLOCK, c0 + LANES:c0 + 2 * LANES] = o2[BLOCK:].astype(BF16)

    zg = 0.5 * zg * (1.0 + jnp.tanh(math.sqrt(2.0 / math.pi) * (zg + 0.044715 * (zg * zg * zg))))
    u, vr = zg[:, :GMLP_WIDTH], zg[:, GMLP_WIDTH:]
    ln = ln_ref[...]
    mu = jnp.mean(vr, axis=-1, keepdims=True)
    vc = vr - mu
    var = jnp.mean(vc * vc, axis=-1, keepdims=True)
    vg = (vc * lax.rsqrt(var + EPS) * ln[0:1] + ln[1:2]).astype(BF16)
    row = lax.broadcasted_iota(jnp.int32, (BLOCK, 2 * BLOCK), 0)
    col2 = lax.broadcasted_iota(jnp.int32, (BLOCK, 2 * BLOCK), 1)
    causal = jnp.where(col2 < BLOCK, col2, col2 - BLOCK) <= row
    for j in range(GMLP_GROUPS // 2):
        w_pair = jnp.where(causal, ws_ref[j], zero)
        pieces = [vg[c * BLOCK:(c + 1) * BLOCK, j * LANES:(j + 1) * LANES] for c in range(nb)]
        rhs = jnp.concatenate(
            [jnp.concatenate([jnp.where(lo, pc, zero) for pc in pieces], axis=1),
             jnp.concatenate([jnp.where(lo, zero, pc) for pc in pieces], axis=1)], axis=0)
        out = _dot(w_pair, rhs)
        for c in range(nb):
            sp_ref[c * BLOCK:(c + 1) * BLOCK, j * LANES:(j + 1) * LANES] = (
                out[:, c * LANES:(c + 1) * LANES])
    bsp = bsp_ref[...]
    gm = jnp.concatenate(
        [u[c * BLOCK:(c + 1) * BLOCK] * (sp_ref[c * BLOCK:(c + 1) * BLOCK, :] + bsp)
         for c in range(nb)], axis=0).astype(BF16)

    ya = _dot(attn_ref[...], wba_ref[...])
    yg = _dot(gm, wbg_ref[...])
    mix = (ya / (1.0 + jnp.exp(-ga)) + yg / (1.0 + jnp.exp(-gg))).astype(BF16)
    y = _dot(mix, wo_ref[...])
    o_ref[...] = _postnorm_residual(x, y, vec, 1.0)


def _mixer(rel, h, vec, bucket, sink_t, ln, bsp, ws_t, w_in, wba, wbg, wo, seq_len):
    t, d = h.shape
    tm = min(MIX_TOKEN_TILE, seq_len)
    tiles_per_seq = seq_len // tm
    kern = functools.partial(_mixer_kernel, tm=tm, sub=min(MIX_SUB_TILE, tm),
                             tiles_per_seq=tiles_per_seq)
    return pl.pallas_call(
        kern,
        grid=(t // tm,),
        in_specs=[pl.BlockSpec(memory_space=pltpu.SMEM),
                  pl.BlockSpec((tm, d), lambda i: (i, 0)),
                  pl.BlockSpec((1, 8, d), lambda i: (i // tiles_per_seq, 0, 0)),
                  _resident(bucket.shape), _resident(sink_t.shape), _resident(ln.shape),
                  _resident(bsp.shape), _resident(ws_t.shape)] + [pl.BlockSpec(memory_space=pl.ANY)] * 4,
        out_specs=pl.BlockSpec((tm, d), lambda i: (i, 0)),
        out_shape=jax.ShapeDtypeStruct((t, d), F32),
        scratch_shapes=[pltpu.VMEM((4, BLOCK, KV_W), BF16),
                        pltpu.VMEM((tm, Q_W), BF16),
                        pltpu.VMEM((tm, GMLP_WIDTH), F32),
                        pltpu.VMEM((4, 2 * BLOCK, 2 * BLOCK), F32),
                        pltpu.VMEM(w_in.shape, BF16), pltpu.VMEM(wba.shape, BF16),
                        pltpu.VMEM(wbg.shape, BF16), pltpu.VMEM(wo.shape, BF16),
                        pltpu.VMEM((STAGE_SLOTS, WIDE_STAGE_ROWS, w_in.shape[1]), F32),
                        pltpu.VMEM((STAGE_SLOTS, STAGE_ROWS, d), F32),
                        pltpu.SemaphoreType.DMA((STAGE_SLOTS,)),
                        pltpu.SemaphoreType.DMA((STAGE_SLOTS,))],
        compiler_params=pltpu.CompilerParams(dimension_semantics=("arbitrary",),
                                             vmem_limit_bytes=VMEM_LIMIT_BYTES),
        name="token_mixers",
    )(rel, h, vec, bucket, sink_t, ln, bsp, ws_t, w_in, wba, wbg, wo)


def _t5_bucket_table():
    qi = np.arange(BLOCK, dtype=np.int32)[:, None]
    kj = np.arange(2 * BLOCK, dtype=np.int32)[None, :]
    dist = qi + BLOCK - kj
    in_window = (dist >= 0) & (dist < BLOCK)
    dc = np.maximum(dist, 0)
    max_exact = N_BUCKETS // 2
    d_f = np.maximum(dc, max_exact).astype(np.float32)
    large = max_exact + (np.log(d_f / np.float32(max_exact)).astype(np.float32)
                         / np.float32(math.log(MAX_DISTANCE / max_exact))
                         * np.float32(N_BUCKETS - max_exact)).astype(np.int32)
    large = np.minimum(large, N_BUCKETS - 1)
    return np.where(in_window, np.where(dc < max_exact, dc, large), -1).astype(np.int32)


def _pair_rows(a):
    out = []
    for kvh in range(N_KV_HEADS):
        for par in range(2):
            out.append(jnp.concatenate([a[GROUP * kvh + par], a[GROUP * kvh + 2 + par]], axis=0))
    return jnp.stack(out)


def kernel(x, c, rel_bias, w_ada, b_ada, pre_norm_g, post_norm_g, w_ffn1_in, w_ffn1_out, w_in,
           sinks, gmlp_ln_g, gmlp_ln_b, gmlp_w_s, gmlp_b_s, w_br_attn, w_br_gmlp, w_out,
           w_ffn2_in, w_ffn2_out):
    bsz, seq, d = x.shape
    depth = w_ada.shape[0]
    bucket = jnp.asarray(_t5_bucket_table())
    h = x.reshape(bsz * seq, d)
    for l in range(depth):
        ada = _ada(c, w_ada[l], b_ada[l]).reshape(bsz, N_ADA, d)

        def vec_for(sub):
            g = jnp.broadcast_to(jnp.stack([pre_norm_g[l, sub], post_norm_g[l, sub]])[None],
                                 (bsz, 2, d))
            return jnp.concatenate([ada[:, 3 * sub:3 * sub + 3], g,
                                    jnp.zeros((bsz, 3, d), F32)], axis=1)

        sink_t = _pair_rows(jnp.broadcast_to(sinks[l].astype(F32)[:, None, None],
                                             (N_Q_HEADS, BLOCK, LANES)))
        ln = jnp.stack([gmlp_ln_g[l], gmlp_ln_b[l]])
        bsp = jnp.repeat(jnp.transpose(gmlp_b_s[l]), GMLP_WIDTH // GMLP_GROUPS, axis=1)
        ws_t = jnp.concatenate([gmlp_w_s[l, 0::2], gmlp_w_s[l, 1::2]], axis=2).astype(BF16)

        h = _ffn(h, vec_for(0), w_ffn1_in[l], w_ffn1_out[l], seq)
        h = _mixer(rel_bias.astype(F32), h, vec_for(1), bucket, sink_t, ln, bsp, ws_t,
                   w_in[l], w_br_attn[l], w_br_gmlp[l], w_out[l], seq)
        h = _ffn(h, vec_for(2), w_ffn2_in[l], w_ffn2_out[l], seq)
    return h.reshape(bsz, seq, d)
```

```python
import functools
import math

import numpy as np
import jax
import jax.numpy as jnp
from jax import lax
from jax.experimental import pallas as pl
from jax.experimental.pallas import tpu as pltpu

D_MODEL = 1024
N_Q_HEADS = 8
N_KV_HEADS = 2
GROUP = N_Q_HEADS // N_KV_HEADS
HEAD_DIM = 64
BLOCK = 128
Q_W = N_Q_HEADS * HEAD_DIM
KV_W = N_KV_HEADS * HEAD_DIM
N_BUCKETS = 32
MAX_DISTANCE = 128
GMLP_GROUPS = 8
GMLP_WIDTH = 512
D_FF = 2816
FFN_RES = 0.5
N_ADA = 9
EPS = 1e-6
NEG = -1e30

LANES = 128
VMEM_LIMIT_BYTES = 56 * 1024 * 1024
FFN_TOKEN_TILE = 1024
FFN_SUB_TILE = 512
WIDE_STAGE_ROWS = 64
STAGE_ROWS = 256
STAGE_SLOTS = 3
MIX_TOKEN_TILE = 1024
MIX_SUB_TILE = 512
FFN_CHUNKS = (512, 512, 512, 512, 512, 256)

BF16 = jnp.bfloat16
F32 = jnp.float32


def _dot(a, b):
    return jnp.dot(a, b, preferred_element_type=F32)


def _dot_nt(a, b):
    return lax.dot_general(a, b, (((1,), (1,)), ((), ())), preferred_element_type=F32)


def _resident(shape):
    nd = len(shape)
    return pl.BlockSpec(shape, lambda i: (0,) * nd, pipeline_mode=pl.Buffered(1))


def _prenorm_mod(x, vec):
    r = lax.rsqrt(jnp.mean(x * x, axis=-1, keepdims=True) + EPS)
    gs = vec[3:4] * (1.0 + vec[1:2])
    return x * r * gs + vec[0:1]


def _postnorm_residual(x, y, vec, res):
    r = lax.rsqrt(jnp.mean(y * y, axis=-1, keepdims=True) + EPS)
    return x + (res * vec[2:3]) * (y * r * vec[4:5])


def _ada_kernel(c_ref, w_ref, b_ref, o_ref):
    c = c_ref[...]
    s = c / (1.0 + jnp.exp(-c))
    w = w_ref[...]
    s_hi = s.astype(BF16)
    s_lo = (s - s_hi.astype(F32)).astype(BF16)
    w_hi = w.astype(BF16)
    w_lo = (w - w_hi.astype(F32)).astype(BF16)
    rows = s.shape[0]
    head = _dot(jnp.concatenate([s_hi, s_lo], axis=0), w_hi)
    o_ref[...] = head[:rows] + head[rows:] + _dot(s_hi, w_lo) + b_ref[...]


def _ada(c, w_ada, b_ada):
    b, d = c.shape
    n = w_ada.shape[1]
    rows = 16
    tn = n // 8
    c_pad = jnp.zeros((rows, d), F32).at[:b].set(c)
    out = pl.pallas_call(
        _ada_kernel,
        grid=(n // tn,),
        in_specs=[pl.BlockSpec((rows, d), lambda j: (0, 0)),
                  pl.BlockSpec((d, tn), lambda j: (0, j)),
                  pl.BlockSpec((1, tn), lambda j: (0, j))],
        out_specs=pl.BlockSpec((rows, tn), lambda j: (0, j)),
        out_shape=jax.ShapeDtypeStruct((rows, n), F32),
        compiler_params=pltpu.CompilerParams(dimension_semantics=("arbitrary",),
                                             vmem_limit_bytes=VMEM_LIMIT_BYTES),
        name="ada_proj",
    )(c_pad, w_ada, b_ada.reshape(1, n))
    return out[:b]


def _load_as_bf16(src_hbm, dst_ref, stage_ref, sem):
    slots, rows = stage_ref.shape[0], stage_ref.shape[1]
    n_chunks = dst_ref.shape[0] // rows

    def copy(c):
        return pltpu.make_async_copy(src_hbm.at[pl.ds(c * rows, rows)], stage_ref.at[c % slots],
                                     sem.at[c % slots])

    for c in range(min(slots - 1, n_chunks)):
        copy(c).start()
    for c in range(n_chunks):
        if c + slots - 1 < n_chunks:
            copy(c + slots - 1).start()
        copy(c).wait()
        dst_ref[pl.ds(c * rows, rows), :] = stage_ref[c % slots].astype(BF16)


def _ffn_kernel(h_ref, vec_ref, w_in_hbm, w_out_hbm, o_ref, w_in_ref, w_out_ref, stage_in, stage_out,
                sem_in, sem_out, *, sub, chunks, d_ff, res):
    @pl.when(pl.program_id(0) == 0)
    def _():
        _load_as_bf16(w_in_hbm, w_in_ref, stage_in, sem_in)
        _load_as_bf16(w_out_hbm, w_out_ref, stage_out, sem_out)

    vec = vec_ref[0]
    starts = list(range(0, h_ref.shape[0], sub))

    def prenorm(r0):
        return _prenorm_mod(h_ref[r0:r0 + sub, :], vec).astype(BF16)

    def finish(r0, y):
        o_ref[r0:r0 + sub, :] = _postnorm_residual(h_ref[r0:r0 + sub, :], y, vec, res)

    n = prenorm(starts[0])
    pending = None
    for idx, r0 in enumerate(starts):
        y = None
        off = 0
        n_next = None
        for ci, width in enumerate(chunks):
            g = _dot(n, w_in_ref[:, off:off + width])
            u = _dot(n, w_in_ref[:, d_ff + off:d_ff + off + width])
            a = (g / (1.0 + jnp.exp(-g)) * u).astype(BF16)
            part = _dot(a, w_out_ref[off:off + width, :])
            y = part if y is None else y + part
            off += width
            if ci == 0 and pending is not None:
                finish(*pending)
                pending = None
            if ci == 1 and idx + 1 < len(starts):
                n_next = prenorm(starts[idx + 1])
        pending = (r0, y)
        n = n_next
    finish(*pending)


def _ffn(h, vec, w_in, w_out, seq_len):
    t, d = h.shape
    tm = min(FFN_TOKEN_TILE, seq_len)
    tiles_per_seq = seq_len // tm
    d_ff = w_out.shape[0]
    kern = functools.partial(_ffn_kernel, sub=min(FFN_SUB_TILE, tm), chunks=FFN_CHUNKS, d_ff=d_ff,
                             res=FFN_RES)
    return pl.pallas_call(
        kern,
        grid=(t // tm,),
        in_specs=[pl.BlockSpec((tm, d), lambda i: (i, 0)),
                  pl.BlockSpec((1, 8, d), lambda i: (i // tiles_per_seq, 0, 0)),
                  pl.BlockSpec(memory_space=pl.ANY),
                  pl.BlockSpec(memory_space=pl.ANY)],
        out_specs=pl.BlockSpec((tm, d), lambda i: (i, 0)),
        out_shape=jax.ShapeDtypeStruct((t, d), F32),
        scratch_shapes=[pltpu.VMEM(w_in.shape, BF16), pltpu.VMEM(w_out.shape, BF16),
                        pltpu.VMEM((STAGE_SLOTS, WIDE_STAGE_ROWS, w_in.shape[1]), F32),
                        pltpu.VMEM((STAGE_SLOTS, STAGE_ROWS, w_out.shape[1]), F32),
                        pltpu.SemaphoreType.DMA((STAGE_SLOTS,)),
                        pltpu.SemaphoreType.DMA((STAGE_SLOTS,))],
        compiler_params=pltpu.CompilerParams(dimension_semantics=("arbitrary",),
                                             vmem_limit_bytes=VMEM_LIMIT_BYTES),
        name="ffn_half_step",
    )(h, vec, w_in, w_out)


def _mixer_kernel(rel_ref, h_ref, vec_ref, bucket_ref, sink_ref, ln_ref, bsp_ref, ws_ref, w_in_ref,
                  wba_ref, wbg_ref, wo_ref, o_ref, carry_ref, attn_ref, sp_ref, bias_ref,
                  *, tm, sub, tiles_per_seq):
    first = (pl.program_id(0) % tiles_per_seq) == 0

    @pl.when(pl.program_id(0) == 0)
    def _():
        bucket = bucket_ref[...]
        hits = [bucket == bk for bk in range(N_BUCKETS)]
        for hd in range(N_Q_HEADS):
            acc = jnp.full(bucket.shape, NEG, F32)
            for bk in range(N_BUCKETS):
                acc = jnp.where(hits[bk], rel_ref[bk, hd], acc)
            kvh, g = divmod(hd, GROUP)
            bias_ref[2 * kvh + g % 2, (g // 2) * BLOCK:(g // 2 + 1) * BLOCK, :] = acc

    @pl.when(first)
    def _():
        carry_ref[...] = jnp.zeros_like(carry_ref)

    for r in range(0, tm, sub):
        rows = pl.ds(r, sub)
        _mixer_tile(first if r == 0 else None, h_ref.at[rows], vec_ref, sink_ref, ln_ref, bsp_ref,
                    ws_ref, w_in_ref, wba_ref, wbg_ref, wo_ref, o_ref.at[rows], carry_ref,
                    attn_ref.at[rows], sp_ref.at[rows], bias_ref, tm=sub)


def _mixer_tile(first, h_ref, vec_ref, sink_ref, ln_ref, bsp_ref, ws_ref, w_in_ref, wba_ref, wbg_ref,
                wo_ref, o_ref, carry_ref, attn_ref, sp_ref, bias_ref, *, tm):
    nb = tm // BLOCK
    x = h_ref[...]
    vec = vec_ref[0]
    n = _prenorm_mod(x, vec).astype(BF16)

    lo = lax.broadcasted_iota(jnp.int32, (1, LANES), 1) < HEAD_DIM

    q = (_dot(n, w_in_ref[:, 0:Q_W]) * (HEAD_DIM ** -0.5)).astype(BF16)
    w_kv = w_in_ref[:, Q_W:Q_W + 2 * KV_W]
    kv = jnp.concatenate([_dot(n[:tm // 2], w_kv), _dot(n[tm // 2:], w_kv)], axis=0)
    k_f, v_f = kv[:, :KV_W], kv[:, KV_W:]
    k_cur, k_rot = k_f.astype(BF16), pltpu.roll(k_f, HEAD_DIM, 1).astype(BF16)
    v_cur, v_rot = v_f.astype(BF16), pltpu.roll(v_f, HEAD_DIM, 1).astype(BF16)
    k_all = jnp.concatenate([carry_ref[0], k_cur], axis=0)
    k_all_rot = jnp.concatenate([carry_ref[1], k_rot], axis=0)
    v_all = jnp.concatenate([carry_ref[2], v_cur], axis=0)
    v_all_rot = jnp.concatenate([carry_ref[3], v_rot], axis=0)
    carry_ref[0] = k_cur[tm - BLOCK:]
    carry_ref[1] = k_rot[tm - BLOCK:]
    carry_ref[2] = v_cur[tm - BLOCK:]
    carry_ref[3] = v_rot[tm - BLOCK:]
    zero = jnp.zeros((), BF16)
    k_var = ((jnp.where(lo, k_all, zero), jnp.where(lo, zero, k_all_rot)),
             (jnp.where(lo, k_all_rot, zero), jnp.where(lo, zero, k_all)))
    v_var = ((jnp.where(lo, v_all, zero), jnp.where(lo, zero, v_all_rot)),
             (jnp.where(lo, v_all_rot, zero), jnp.where(lo, zero, v_all)))

    col = lax.broadcasted_iota(jnp.int32, (1, 2 * BLOCK), 1)
    no_prev = None if first is None else jnp.where(jnp.logical_and(first, col < BLOCK), NEG, 0.0)

    scores = []
    for b in range(nb):
        r0 = b * BLOCK
        keys = slice(r0, r0 + 2 * BLOCK)
        per_block = []
        for kvh in range(N_KV_HEADS):
            c0 = kvh * GROUP * HEAD_DIM
            q2 = jnp.concatenate([q[r0:r0 + BLOCK, c0:c0 + LANES],
                                  q[r0:r0 + BLOCK, c0 + LANES:c0 + 2 * LANES]], axis=0)
            for par in range(2):
                s = _dot_nt(q2, k_var[kvh][par][keys]) + bias_ref[2 * kvh + par]
                per_block.append(s + no_prev if (b == 0 and no_prev is not None) else s)
        scores.append(jnp.stack(per_block))
    sink = sink_ref[...]
    g0 = Q_W + 2 * KV_W + 2 * GMLP_WIDTH

    def softmax_block(b):
        s = scores[b]
        m = jnp.max(s, axis=-1, keepdims=True)
        return jnp.exp(s - m).astype(BF16), jnp.exp(sink - m)

    soft = [None] * nb
    zg = _dot(n, w_in_ref[:, Q_W + 2 * KV_W:g0])
    for b in range(0, nb // 2):
        soft[b] = softmax_block(b)
    ga = _dot(n, w_in_ref[:, g0:g0 + D_MODEL])
    for b in range(nb // 2, nb):
        soft[b] = softmax_block(b)
    gg = _dot(n, w_in_ref[:, g0 + D_MODEL:g0 + 2 * D_MODEL])

    rows2 = lax.broadcasted_iota(jnp.int32, (4 * BLOCK, LANES), 0) < 2 * BLOCK
    ones_cols = jnp.where(rows2 == lo, 1.0, 0.0).astype(BF16)
    for b in range(nb):
        r0 = b * BLOCK
        keys = slice(r0, r0 + 2 * BLOCK)
        probs, sink_term = soft[b]
        for kvh in range(N_KV_HEADS):
            c0 = kvh * GROUP * HEAD_DIM
            p2 = jnp.concatenate([probs[2 * kvh], probs[2 * kvh + 1]], axis=1)
            v2 = jnp.concatenate([v_var[kvh][0][keys], v_var[kvh][1][keys]], axis=0)
            o2 = _dot(p2, jnp.concatenate([v2, ones_cols], axis=1))
            denom = o2[:, LANES:] + jnp.where(lo, sink_term[2 * kvh], sink_term[2 * kvh + 1])
            o2 = o2[:, :LANES] / denom
            attn_ref[r0:r0 + BLOCK, c0:c0 + LANES] = o2[:BLOCK].astype(BF16)
            attn_ref[r0:r0 + BLOCK, c0 + LANES:c0 + 2 * LANES] = o2[BLOCK:].astype(BF16)

    zg = 0.5 * zg * (1.0 + jnp.tanh(math.sqrt(2.0 / math.pi) * (zg + 0.044715 * (zg * zg * zg))))
    u, vr = zg[:, :GMLP_WIDTH], zg[:, GMLP_WIDTH:]
    ln = ln_ref[...]
    mu = jnp.mean(vr, axis=-1, keepdims=True)
    vc = vr - mu
    var = jnp.mean(vc * vc, axis=-1, keepdims=True)
    vg = (vc * lax.rsqrt(var + EPS) * ln[0:1] + ln[1:2]).astype(BF16)
    row = lax.broadcasted_iota(jnp.int32, (BLOCK, 2 * BLOCK), 0)
    col2 = lax.broadcasted_iota(jnp.int32, (BLOCK, 2 * BLOCK), 1)
    causal = jnp.where(col2 < BLOCK, col2, col2 - BLOCK) <= row
    for j in range(GMLP_GROUPS // 2):
        w_pair = jnp.where(causal, ws_ref[j], zero)
        pieces = [vg[c * BLOCK:(c + 1) * BLOCK, j * LANES:(j + 1) * LANES] for c in range(nb)]
        rhs = jnp.concatenate(
            [jnp.concatenate([jnp.where(lo, pc, zero) for pc in pieces], axis=1),
             jnp.concatenate([jnp.where(lo, zero, pc) for pc in pieces], axis=1)], axis=0)
        out = _dot(w_pair, rhs)
        for c in range(nb):
            sp_ref[c * BLOCK:(c + 1) * BLOCK, j * LANES:(j + 1) * LANES] = (
                out[:, c * LANES:(c + 1) * LANES])
    bsp = bsp_ref[...]
    gm = jnp.concatenate(
        [u[c * BLOCK:(c + 1) * BLOCK] * (sp_ref[c * BLOCK:(c + 1) * BLOCK, :] + bsp)
         for c in range(nb)], axis=0).astype(BF16)

    ya = _dot(attn_ref[...], wba_ref[...])
    yg = _dot(gm, wbg_ref[...])
    mix = (ya / (1.0 + jnp.exp(-ga)) + yg / (1.0 + jnp.exp(-gg))).astype(BF16)
    y = _dot(mix, wo_ref[...])
    o_ref[...] = _postnorm_residual(x, y, vec, 1.0)


def _mixer(rel, h, vec, bucket, sink_t, ln, bsp, ws_t, w_in, wba, wbg, wo, seq_len):
    t, d = h.shape
    tm = min(MIX_TOKEN_TILE, seq_len)
    tiles_per_seq = seq_len // tm
    kern = functools.partial(_mixer_kernel, tm=tm, sub=min(MIX_SUB_TILE, tm),
                             tiles_per_seq=tiles_per_seq)
    return pl.pallas_call(
        kern,
        grid=(t // tm,),
        in_specs=[pl.BlockSpec(memory_space=pltpu.SMEM),
                  pl.BlockSpec((tm, d), lambda i: (i, 0)),
                  pl.BlockSpec((1, 8, d), lambda i: (i // tiles_per_seq, 0, 0)),
                  _resident(bucket.shape), _resident(sink_t.shape), _resident(ln.shape),
                  _resident(bsp.shape), _resident(ws_t.shape), _resident(w_in.shape),
                  _resident(wba.shape), _resident(wbg.shape), _resident(wo.shape)],
        out_specs=pl.BlockSpec((tm, d), lambda i: (i, 0)),
        out_shape=jax.ShapeDtypeStruct((t, d), F32),
        scratch_shapes=[pltpu.VMEM((4, BLOCK, KV_W), BF16),
                        pltpu.VMEM((tm, Q_W), BF16),
                        pltpu.VMEM((tm, GMLP_WIDTH), F32),
                        pltpu.VMEM((4, 2 * BLOCK, 2 * BLOCK), F32)],
        compiler_params=pltpu.CompilerParams(dimension_semantics=("arbitrary",),
                                             vmem_limit_bytes=VMEM_LIMIT_BYTES),
        name="token_mixers",
    )(rel, h, vec, bucket, sink_t, ln, bsp, ws_t, w_in, wba, wbg, wo)


def _t5_bucket_table():
    qi = np.arange(BLOCK, dtype=np.int32)[:, None]
    kj = np.arange(2 * BLOCK, dtype=np.int32)[None, :]
    dist = qi + BLOCK - kj
    in_window = (dist >= 0) & (dist < BLOCK)
    dc = np.maximum(dist, 0)
    max_exact = N_BUCKETS // 2
    d_f = np.maximum(dc, max_exact).astype(np.float32)
    large = max_exact + (np.log(d_f / np.float32(max_exact)).astype(np.float32)
                         / np.float32(math.log(MAX_DISTANCE / max_exact))
                         * np.float32(N_BUCKETS - max_exact)).astype(np.int32)
    large = np.minimum(large, N_BUCKETS - 1)
    return np.where(in_window, np.where(dc < max_exact, dc, large), -1).astype(np.int32)


def _pair_rows(a):
    out = []
    for kvh in range(N_KV_HEADS):
        for par in range(2):
            out.append(jnp.concatenate([a[GROUP * kvh + par], a[GROUP * kvh + 2 + par]], axis=0))
    return jnp.stack(out)


def kernel(x, c, rel_bias, w_ada, b_ada, pre_norm_g, post_norm_g, w_ffn1_in, w_ffn1_out, w_in,
           sinks, gmlp_ln_g, gmlp_ln_b, gmlp_w_s, gmlp_b_s, w_br_attn, w_br_gmlp, w_out,
           w_ffn2_in, w_ffn2_out):
    bsz, seq, d = x.shape
    depth = w_ada.shape[0]
    bucket = jnp.asarray(_t5_bucket_table())
    h = x.reshape(bsz * seq, d)
    for l in range(depth):
        ada = _ada(c, w_ada[l], b_ada[l]).reshape(bsz, N_ADA, d)

        def vec_for(sub):
            g = jnp.broadcast_to(jnp.stack([pre_norm_g[l, sub], post_norm_g[l, sub]])[None],
                                 (bsz, 2, d))
            return jnp.concatenate([ada[:, 3 * sub:3 * sub + 3], g,
                                    jnp.zeros((bsz, 3, d), F32)], axis=1)

        sink_t = _pair_rows(jnp.broadcast_to(sinks[l].astype(F32)[:, None, None],
                                             (N_Q_HEADS, BLOCK, LANES)))
        ln = jnp.stack([gmlp_ln_g[l], gmlp_ln_b[l]])
        bsp = jnp.repeat(jnp.transpose(gmlp_b_s[l]), GMLP_WIDTH // GMLP_GROUPS, axis=1)
        ws_t = jnp.concatenate([gmlp_w_s[l, 0::2], gmlp_w_s[l, 1::2]], axis=2).astype(BF16)

        h = _ffn(h, vec_for(0), w_ffn1_in[l], w_ffn1_out[l], seq)
        h = _mixer(rel_bias.astype(F32), h, vec_for(1), bucket, sink_t, ln, bsp, ws_t,
                   w_in[l].astype(BF16), w_br_attn[l].astype(BF16), w_br_gmlp[l].astype(BF16),
                   w_out[l].astype(BF16), seq)
        h = _ffn(h, vec_for(2), w_ffn2_in[l], w_ffn2_out[l], seq)
    return h.reshape(bsz, seq, d)
```

```python
import functools
import math

import numpy as np
import jax
import jax.numpy as jnp
from jax import lax
from jax.experimental import pallas as pl
from jax.experimental.pallas import tpu as pltpu

D_MODEL = 1024
N_Q_HEADS = 8
N_KV_HEADS = 2
GROUP = N_Q_HEADS // N_KV_HEADS
HEAD_DIM = 64
BLOCK = 128
Q_W = N_Q_HEADS * HEAD_DIM
KV_W = N_KV_HEADS * HEAD_DIM
N_BUCKETS = 32
MAX_DISTANCE = 128
GMLP_GROUPS = 8
GMLP_WIDTH = 512
D_FF = 2816
FFN_RES = 0.5
N_ADA = 9
EPS = 1e-6
NEG = -1e30

LANES = 128
VMEM_LIMIT_BYTES = 56 * 1024 * 1024
FFN_TOKEN_TILE = 1024
FFN_SUB_TILE = 1024
WIDE_STAGE_ROWS = 64
STAGE_ROWS = 256
STAGE_SLOTS = 3
MIX_TOKEN_TILE = 1024
MIX_SUB_TILE = 512
FFN_CHUNKS = (512, 512, 512, 512, 512, 256)

BF16 = jnp.bfloat16
F32 = jnp.float32


def _dot(a, b):
    return jnp.dot(a, b, preferred_element_type=F32)


def _dot_nt(a, b):
    return lax.dot_general(a, b, (((1,), (1,)), ((), ())), preferred_element_type=F32)


def _resident(shape):
    nd = len(shape)
    return pl.BlockSpec(shape, lambda i: (0,) * nd, pipeline_mode=pl.Buffered(1))


def _prenorm_mod(x, vec):
    r = lax.rsqrt(jnp.mean(x * x, axis=-1, keepdims=True) + EPS)
    gs = vec[3:4] * (1.0 + vec[1:2])
    return x * r * gs + vec[0:1]


def _postnorm_residual(x, y, vec, res):
    r = lax.rsqrt(jnp.mean(y * y, axis=-1, keepdims=True) + EPS)
    return x + (res * vec[2:3]) * (y * r * vec[4:5])


def _ada_kernel(c_ref, w_ref, b_ref, o_ref):
    c = c_ref[...]
    s = c / (1.0 + jnp.exp(-c))
    w = w_ref[...]
    s_hi = s.astype(BF16)
    s_lo = (s - s_hi.astype(F32)).astype(BF16)
    w_hi = w.astype(BF16)
    w_lo = (w - w_hi.astype(F32)).astype(BF16)
    rows = s.shape[0]
    head = _dot(jnp.concatenate([s_hi, s_lo], axis=0), w_hi)
    o_ref[...] = head[:rows] + head[rows:] + _dot(s_hi, w_lo) + b_ref[...]


def _ada(c, w_ada, b_ada):
    b, d = c.shape
    n = w_ada.shape[1]
    rows = 16
    tn = n // 8
    c_pad = jnp.zeros((rows, d), F32).at[:b].set(c)
    out = pl.pallas_call(
        _ada_kernel,
        grid=(n // tn,),
        in_specs=[pl.BlockSpec((rows, d), lambda j: (0, 0)),
                  pl.BlockSpec((d, tn), lambda j: (0, j)),
                  pl.BlockSpec((1, tn), lambda j: (0, j))],
        out_specs=pl.BlockSpec((rows, tn), lambda j: (0, j)),
        out_shape=jax.ShapeDtypeStruct((rows, n), F32),
        compiler_params=pltpu.CompilerParams(dimension_semantics=("arbitrary",),
                                             vmem_limit_bytes=VMEM_LIMIT_BYTES),
        name="ada_proj",
    )(c_pad, w_ada, b_ada.reshape(1, n))
    return out[:b]


def _load_as_bf16(src_hbm, dst_ref, stage_ref, sem):
    slots, rows = stage_ref.shape[0], stage_ref.shape[1]
    n_chunks = dst_ref.shape[0] // rows

    def copy(c):
        return pltpu.make_async_copy(src_hbm.at[pl.ds(c * rows, rows)], stage_ref.at[c % slots],
                                     sem.at[c % slots])

    for c in range(min(slots - 1, n_chunks)):
        copy(c).start()
    for c in range(n_chunks):
        if c + slots - 1 < n_chunks:
            copy(c + slots - 1).start()
        copy(c).wait()
        dst_ref[pl.ds(c * rows, rows), :] = stage_ref[c % slots].astype(BF16)


def _ffn_kernel(h_ref, vec_ref, w_in_hbm, w_out_hbm, o_ref, w_in_ref, w_out_ref, stage_in, stage_out,
                sem_in, sem_out, *, sub, chunks, d_ff, res):
    @pl.when(pl.program_id(0) == 0)
    def _():
        _load_as_bf16(w_in_hbm, w_in_ref, stage_in, sem_in)
        _load_as_bf16(w_out_hbm, w_out_ref, stage_out, sem_out)

    vec = vec_ref[0]
    starts = list(range(0, h_ref.shape[0], sub))

    def prenorm(r0):
        return _prenorm_mod(h_ref[r0:r0 + sub, :], vec).astype(BF16)

    def finish(r0, y):
        o_ref[r0:r0 + sub, :] = _postnorm_residual(h_ref[r0:r0 + sub, :], y, vec, res)

    n = prenorm(starts[0])
    pending = None
    for idx, r0 in enumerate(starts):
        y = None
        off = 0
        n_next = None
        for ci, width in enumerate(chunks):
            g = _dot(n, w_in_ref[:, off:off + width])
            u = _dot(n, w_in_ref[:, d_ff + off:d_ff + off + width])
            a = (g / (1.0 + jnp.exp(-g)) * u).astype(BF16)
            part = _dot(a, w_out_ref[off:off + width, :])
            y = part if y is None else y + part
            off += width
            if ci == 0 and pending is not None:
                finish(*pending)
                pending = None
            if ci == 1 and idx + 1 < len(starts):
                n_next = prenorm(starts[idx + 1])
        pending = (r0, y)
        n = n_next
    finish(*pending)


def _ffn(h, vec, w_in, w_out, seq_len):
    t, d = h.shape
    tm = min(FFN_TOKEN_TILE, seq_len)
    tiles_per_seq = seq_len // tm
    d_ff = w_out.shape[0]
    kern = functools.partial(_ffn_kernel, sub=min(FFN_SUB_TILE, tm), chunks=FFN_CHUNKS, d_ff=d_ff,
                             res=FFN_RES)
    return pl.pallas_call(
        kern,
        grid=(t // tm,),
        in_specs=[pl.BlockSpec((tm, d), lambda i: (i, 0)),
                  pl.BlockSpec((1, 8, d), lambda i: (i // tiles_per_seq, 0, 0)),
                  pl.BlockSpec(memory_space=pl.ANY),
                  pl.BlockSpec(memory_space=pl.ANY)],
        out_specs=pl.BlockSpec((tm, d), lambda i: (i, 0)),
        out_shape=jax.ShapeDtypeStruct((t, d), F32),
        scratch_shapes=[pltpu.VMEM(w_in.shape, BF16), pltpu.VMEM(w_out.shape, BF16),
                        pltpu.VMEM((STAGE_SLOTS, WIDE_STAGE_ROWS, w_in.shape[1]), F32),
                        pltpu.VMEM((STAGE_SLOTS, STAGE_ROWS, w_out.shape[1]), F32),
                        pltpu.SemaphoreType.DMA((STAGE_SLOTS,)),
                        pltpu.SemaphoreType.DMA((STAGE_SLOTS,))],
        compiler_params=pltpu.CompilerParams(dimension_semantics=("arbitrary",),
                                             vmem_limit_bytes=VMEM_LIMIT_BYTES),
        name="ffn_half_step",
    )(h, vec, w_in, w_out)


def _mixer_kernel(rel_ref, h_ref, vec_ref, bucket_ref, sink_ref, ln_ref, bsp_ref, ws_ref, w_in_ref,
                  wba_ref, wbg_ref, wo_ref, o_ref, carry_ref, attn_ref, sp_ref, bias_ref,
                  *, tm, sub, tiles_per_seq):
    first = (pl.program_id(0) % tiles_per_seq) == 0

    @pl.when(pl.program_id(0) == 0)
    def _():
        bucket = bucket_ref[...]
        hits = [bucket == bk for bk in range(N_BUCKETS)]
        for hd in range(N_Q_HEADS):
            acc = jnp.full(bucket.shape, NEG, F32)
            for bk in range(N_BUCKETS):
                acc = jnp.where(hits[bk], rel_ref[bk, hd], acc)
            kvh, g = divmod(hd, GROUP)
            bias_ref[2 * kvh + g % 2, (g // 2) * BLOCK:(g // 2 + 1) * BLOCK, :] = acc

    @pl.when(first)
    def _():
        carry_ref[...] = jnp.zeros_like(carry_ref)

    for r in range(0, tm, sub):
        rows = pl.ds(r, sub)
        _mixer_tile(first if r == 0 else None, h_ref.at[rows], vec_ref, sink_ref, ln_ref, bsp_ref,
                    ws_ref, w_in_ref, wba_ref, wbg_ref, wo_ref, o_ref.at[rows], carry_ref,
                    attn_ref.at[rows], sp_ref.at[rows], bias_ref, tm=sub)


def _mixer_tile(first, h_ref, vec_ref, sink_ref, ln_ref, bsp_ref, ws_ref, w_in_ref, wba_ref, wbg_ref,
                wo_ref, o_ref, carry_ref, attn_ref, sp_ref, bias_ref, *, tm):
    nb = tm // BLOCK
    x = h_ref[...]
    vec = vec_ref[0]
    n = _prenorm_mod(x, vec).astype(BF16)

    lo = lax.broadcasted_iota(jnp.int32, (1, LANES), 1) < HEAD_DIM

    q = (_dot(n, w_in_ref[:, 0:Q_W]) * (HEAD_DIM ** -0.5)).astype(BF16)
    w_kv = w_in_ref[:, Q_W:Q_W + 2 * KV_W]
    kv = jnp.concatenate([_dot(n[:tm // 2], w_kv), _dot(n[tm // 2:], w_kv)], axis=0)
    k_f, v_f = kv[:, :KV_W], kv[:, KV_W:]
    k_cur, k_rot = k_f.astype(BF16), pltpu.roll(k_f, HEAD_DIM, 1).astype(BF16)
    v_cur, v_rot = v_f.astype(BF16), pltpu.roll(v_f, HEAD_DIM, 1).astype(BF16)
    k_all = jnp.concatenate([carry_ref[0], k_cur], axis=0)
    k_all_rot = jnp.concatenate([carry_ref[1], k_rot], axis=0)
    v_all = jnp.concatenate([carry_ref[2], v_cur], axis=0)
    v_all_rot = jnp.concatenate([carry_ref[3], v_rot], axis=0)
    carry_ref[0] = k_cur[tm - BLOCK:]
    carry_ref[1] = k_rot[tm - BLOCK:]
    carry_ref[2] = v_cur[tm - BLOCK:]
    carry_ref[3] = v_rot[tm - BLOCK:]
    zero = jnp.zeros((), BF16)
    k_var = ((jnp.where(lo, k_all, zero), jnp.where(lo, zero, k_all_rot)),
             (jnp.where(lo, k_all_rot, zero), jnp.where(lo, zero, k_all)))
    v_var = ((jnp.where(lo, v_all, zero), jnp.where(lo, zero, v_all_rot)),
             (jnp.where(lo, v_all_rot, zero), jnp.where(lo, zero, v_all)))

    col = lax.broadcasted_iota(jnp.int32, (1, 2 * BLOCK), 1)
    no_prev = None if first is None else jnp.where(jnp.logical_and(first, col < BLOCK), NEG, 0.0)

    scores = []
    for b in range(nb):
        r0 = b * BLOCK
        keys = slice(r0, r0 + 2 * BLOCK)
        per_block = []
        for kvh in range(N_KV_HEADS):
            c0 = kvh * GROUP * HEAD_DIM
            q2 = jnp.concatenate([q[r0:r0 + BLOCK, c0:c0 + LANES],
                                  q[r0:r0 + BLOCK, c0 + LANES:c0 + 2 * LANES]], axis=0)
            for par in range(2):
                s = _dot_nt(q2, k_var[kvh][par][keys]) + bias_ref[2 * kvh + par]
                per_block.append(s + no_prev if (b == 0 and no_prev is not None) else s)
        scores.append(jnp.stack(per_block))
    sink = sink_ref[...]
    g0 = Q_W + 2 * KV_W + 2 * GMLP_WIDTH

    def softmax_block(b):
        s = scores[b]
        m = jnp.max(s, axis=-1, keepdims=True)
        return jnp.exp(s - m).astype(BF16), jnp.exp(sink - m)

    soft = [None] * nb
    zg = _dot(n, w_in_ref[:, Q_W + 2 * KV_W:g0])
    for b in range(0, nb // 2):
        soft[b] = softmax_block(b)
    ga = _dot(n, w_in_ref[:, g0:g0 + D_MODEL])
    for b in range(nb // 2, nb):
        soft[b] = softmax_block(b)
    gg = _dot(n, w_in_ref[:, g0 + D_MODEL:g0 + 2 * D_MODEL])

    rows2 = lax.broadcasted_iota(jnp.int32, (4 * BLOCK, LANES), 0) < 2 * BLOCK
    ones_cols = jnp.where(rows2 == lo, 1.0, 0.0).astype(BF16)
    for b in range(nb):
        r0 = b * BLOCK
        keys = slice(r0, r0 + 2 * BLOCK)
        probs, sink_term = soft[b]
        for kvh in range(N_KV_HEADS):
            c0 = kvh * GROUP * HEAD_DIM
            p2 = jnp.concatenate([probs[2 * kvh], probs[2 * kvh + 1]], axis=1)
            v2 = jnp.concatenate([v_var[kvh][0][keys], v_var[kvh][1][keys]], axis=0)
            o2 = _dot(p2, jnp.concatenate([v2, ones_cols], axis=1))
            denom = o2[:, LANES:] + jnp.where(lo, sink_term[2 * kvh], sink_term[2 * kvh + 1])
            o2 = o2[:, :LANES] / denom
            attn_ref[r0:r0 + BLOCK, c0:c0 + LANES] = o2[:BLOCK].astype(BF16)
            attn_ref[r0:r0 + BLOCK, c0 + LANES:c0 + 2 * LANES] = o2[BLOCK:].astype(BF16)

    zg = 0.5 * zg * (1.0 + jnp.tanh(math.sqrt(2.0 / math.pi) * (zg + 0.044715 * (zg * zg * zg))))
    u, vr = zg[:, :GMLP_WIDTH], zg[:, GMLP_WIDTH:]
    ln = ln_ref[...]
    mu = jnp.mean(vr, axis=-1, keepdims=True)
    vc = vr - mu
    var = jnp.mean(vc * vc, axis=-1, keepdims=True)
    vg = (vc * lax.rsqrt(var + EPS) * ln[0:1] + ln[1:2]).astype(BF16)
    row = lax.broadcasted_iota(jnp.int32, (BLOCK, 2 * BLOCK), 0)
    col2 = lax.broadcasted_iota(jnp.int32, (BLOCK, 2 * BLOCK), 1)
    causal = jnp.where(col2 < BLOCK, col2, col2 - BLOCK) <= row
    for j in range(GMLP_GROUPS // 2):
        w_pair = jnp.where(causal, ws_ref[j], zero)
        pieces = [vg[c * BLOCK:(c + 1) * BLOCK, j * LANES:(j + 1) * LANES] for c in range(nb)]
        rhs = jnp.concatenate(
            [jnp.concatenate([jnp.where(lo, pc, zero) for pc in pieces], axis=1),
             jnp.concatenate([jnp.where(lo, zero, pc) for pc in pieces], axis=1)], axis=0)
        out = _dot(w_pair, rhs)
        for c in range(nb):
            sp_ref[c * BLOCK:(c + 1) * BLOCK, j * LANES:(j + 1) * LANES] = (
                out[:, c * LANES:(c + 1) * LANES])
    bsp = bsp_ref[...]
    gm = jnp.concatenate(
        [u[c * BLOCK:(c + 1) * BLOCK] * (sp_ref[c * BLOCK:(c + 1) * BLOCK, :] + bsp)
         for c in range(nb)], axis=0).astype(BF16)

    ya = _dot(attn_ref[...], wba_ref[...])
    yg = _dot(gm, wbg_ref[...])
    mix = (ya / (1.0 + jnp.exp(-ga)) + yg / (1.0 + jnp.exp(-gg))).astype(BF16)
    y = _dot(mix, wo_ref[...])
    o_ref[...] = _postnorm_residual(x, y, vec, 1.0)


def _mixer(rel, h, vec, bucket, sink_t, ln, bsp, ws_t, w_in, wba, wbg, wo, seq_len):
    t, d = h.shape
    tm = min(MIX_TOKEN_TILE, seq_len)
    tiles_per_seq = seq_len // tm
    kern = functools.partial(_mixer_kernel, tm=tm, sub=min(MIX_SUB_TILE, tm),
                             tiles_per_seq=tiles_per_seq)
    return pl.pallas_call(
        kern,
        grid=(t // tm,),
        in_specs=[pl.BlockSpec(memory_space=pltpu.SMEM),
                  pl.BlockSpec((tm, d), lambda i: (i, 0)),
                  pl.BlockSpec((1, 8, d), lambda i: (i // tiles_per_seq, 0, 0)),
                  _resident(bucket.shape), _resident(sink_t.shape), _resident(ln.shape),
                  _resident(bsp.shape), _resident(ws_t.shape), _resident(w_in.shape),
                  _resident(wba.shape), _resident(wbg.shape), _resident(wo.shape)],
        out_specs=pl.BlockSpec((tm, d), lambda i: (i, 0)),
        out_shape=jax.ShapeDtypeStruct((t, d), F32),
        scratch_shapes=[pltpu.VMEM((4, BLOCK, KV_W), BF16),
                        pltpu.VMEM((tm, Q_W), BF16),
                        pltpu.VMEM((tm, GMLP_WIDTH), F32),
                        pltpu.VMEM((4, 2 * BLOCK, 2 * BLOCK), F32)],
        compiler_params=pltpu.CompilerParams(dimension_semantics=("arbitrary",),
                                             vmem_limit_bytes=VMEM_LIMIT_BYTES),
        name="token_mixers",
    )(rel, h, vec, bucket, sink_t, ln, bsp, ws_t, w_in, wba, wbg, wo)


def _t5_bucket_table():
    qi = np.arange(BLOCK, dtype=np.int32)[:, None]
    kj = np.arange(2 * BLOCK, dtype=np.int32)[None, :]
    dist = qi + BLOCK - kj
    in_window = (dist >= 0) & (dist < BLOCK)
    dc = np.maximum(dist, 0)
    max_exact = N_BUCKETS // 2
    d_f = np.maximum(dc, max_exact).astype(np.float32)
    large = max_exact + (np.log(d_f / np.float32(max_exact)).astype(np.float32)
                         / np.float32(math.log(MAX_DISTANCE / max_exact))
                         * np.float32(N_BUCKETS - max_exact)).astype(np.int32)
    large = np.minimum(large, N_BUCKETS - 1)
    return np.where(in_window, np.where(dc < max_exact, dc, large), -1).astype(np.int32)


def _pair_rows(a):
    out = []
    for kvh in range(N_KV_HEADS):
        for par in range(2):
            out.append(jnp.concatenate([a[GROUP * kvh + par], a[GROUP * kvh + 2 + par]], axis=0))
    return jnp.stack(out)


def kernel(x, c, rel_bias, w_ada, b_ada, pre_norm_g, post_norm_g, w_ffn1_in, w_ffn1_out, w_in,
           sinks, gmlp_ln_g, gmlp_ln_b, gmlp_w_s, gmlp_b_s, w_br_attn, w_br_gmlp, w_out,
           w_ffn2_in, w_ffn2_out):
    bsz, seq, d = x.shape
    depth = w_ada.shape[0]
    bucket = jnp.asarray(_t5_bucket_table())
    h = x.reshape(bsz * seq, d)
    for l in range(depth):
        ada = _ada(c, w_ada[l], b_ada[l]).reshape(bsz, N_ADA, d)

        def vec_for(sub):
            g = jnp.broadcast_to(jnp.stack([pre_norm_g[l, sub], post_norm_g[l, sub]])[None],
                                 (bsz, 2, d))
            return jnp.concatenate([ada[:, 3 * sub:3 * sub + 3], g,
                                    jnp.zeros((bsz, 3, d), F32)], axis=1)

        sink_t = _pair_rows(jnp.broadcast_to(sinks[l].astype(F32)[:, None, None],
                                             (N_Q_HEADS, BLOCK, LANES)))
        ln = jnp.stack([gmlp_ln_g[l], gmlp_ln_b[l]])
        bsp = jnp.repeat(jnp.transpose(gmlp_b_s[l]), GMLP_WIDTH // GMLP_GROUPS, axis=1)
        ws_t = jnp.concatenate([gmlp_w_s[l, 0::2], gmlp_w_s[l, 1::2]], axis=2).astype(BF16)

        h = _ffn(h, vec_for(0), w_ffn1_in[l], w_ffn1_out[l], seq)
        h = _mixer(rel_bias.astype(F32), h, vec_for(1), bucket, sink_t, ln, bsp, ws_t,
                   w_in[l].astype(BF16), w_br_attn[l].astype(BF16), w_br_gmlp[l].astype(BF16),
                   w_out[l].astype(BF16), seq)
        h = _ffn(h, vec_for(2), w_ffn2_in[l], w_ffn2_out[l], seq)
    return h.reshape(bsz, seq, d)
```

```python
import functools
import math

import numpy as np
import jax
import jax.numpy as jnp
from jax import lax
from jax.experimental import pallas as pl
from jax.experimental.pallas import tpu as pltpu

D_MODEL = 1024
N_Q_HEADS = 8
N_KV_HEADS = 2
GROUP = N_Q_HEADS // N_KV_HEADS
HEAD_DIM = 64
BLOCK = 128
Q_W = N_Q_HEADS * HEAD_DIM
KV_W = N_KV_HEADS * HEAD_DIM
N_BUCKETS = 32
MAX_DISTANCE = 128
GMLP_GROUPS = 8
GMLP_WIDTH = 512
D_FF = 2816
FFN_RES = 0.5
N_ADA = 9
EPS = 1e-6
NEG = -1e30

LANES = 128
VMEM_LIMIT_BYTES = 56 * 1024 * 1024
FFN_TOKEN_TILE = 1024
FFN_SUB_TILE = 256
WIDE_STAGE_ROWS = 64
STAGE_ROWS = 256
STAGE_SLOTS = 3
MIX_TOKEN_TILE = 1024
MIX_SUB_TILE = 512
FFN_CHUNKS = (512, 512, 512, 512, 512, 256)

BF16 = jnp.bfloat16
F32 = jnp.float32


def _dot(a, b):
    return jnp.dot(a, b, preferred_element_type=F32)


def _dot_nt(a, b):
    return lax.dot_general(a, b, (((1,), (1,)), ((), ())), preferred_element_type=F32)


def _resident(shape):
    nd = len(shape)
    return pl.BlockSpec(shape, lambda i: (0,) * nd, pipeline_mode=pl.Buffered(1))


def _prenorm_mod(x, vec):
    r = lax.rsqrt(jnp.mean(x * x, axis=-1, keepdims=True) + EPS)
    gs = vec[3:4] * (1.0 + vec[1:2])
    return x * r * gs + vec[0:1]


def _postnorm_residual(x, y, vec, res):
    r = lax.rsqrt(jnp.mean(y * y, axis=-1, keepdims=True) + EPS)
    return x + (res * vec[2:3]) * (y * r * vec[4:5])


def _ada_kernel(c_ref, w_ref, b_ref, o_ref):
    c = c_ref[...]
    s = c / (1.0 + jnp.exp(-c))
    w = w_ref[...]
    s_hi = s.astype(BF16)
    s_lo = (s - s_hi.astype(F32)).astype(BF16)
    w_hi = w.astype(BF16)
    w_lo = (w - w_hi.astype(F32)).astype(BF16)
    rows = s.shape[0]
    head = _dot(jnp.concatenate([s_hi, s_lo], axis=0), w_hi)
    o_ref[...] = head[:rows] + head[rows:] + _dot(s_hi, w_lo) + b_ref[...]


def _ada(c, w_ada, b_ada):
    b, d = c.shape
    n = w_ada.shape[1]
    rows = 16
    tn = n // 8
    c_pad = jnp.zeros((rows, d), F32).at[:b].set(c)
    out = pl.pallas_call(
        _ada_kernel,
        grid=(n // tn,),
        in_specs=[pl.BlockSpec((rows, d), lambda j: (0, 0)),
                  pl.BlockSpec((d, tn), lambda j: (0, j)),
                  pl.BlockSpec((1, tn), lambda j: (0, j))],
        out_specs=pl.BlockSpec((rows, tn), lambda j: (0, j)),
        out_shape=jax.ShapeDtypeStruct((rows, n), F32),
        compiler_params=pltpu.CompilerParams(dimension_semantics=("arbitrary",),
                                             vmem_limit_bytes=VMEM_LIMIT_BYTES),
        name="ada_proj",
    )(c_pad, w_ada, b_ada.reshape(1, n))
    return out[:b]


def _load_as_bf16(src_hbm, dst_ref, stage_ref, sem):
    slots, rows = stage_ref.shape[0], stage_ref.shape[1]
    n_chunks = dst_ref.shape[0] // rows

    def copy(c):
        return pltpu.make_async_copy(src_hbm.at[pl.ds(c * rows, rows)], stage_ref.at[c % slots],
                                     sem.at[c % slots])

    for c in range(min(slots - 1, n_chunks)):
        copy(c).start()
    for c in range(n_chunks):
        if c + slots - 1 < n_chunks:
            copy(c + slots - 1).start()
        copy(c).wait()
        dst_ref[pl.ds(c * rows, rows), :] = stage_ref[c % slots].astype(BF16)


def _ffn_kernel(h_ref, vec_ref, w_in_hbm, w_out_hbm, o_ref, w_in_ref, w_out_ref, stage_in, stage_out,
                sem_in, sem_out, *, sub, chunks, d_ff, res):
    @pl.when(pl.program_id(0) == 0)
    def _():
        _load_as_bf16(w_in_hbm, w_in_ref, stage_in, sem_in)
        _load_as_bf16(w_out_hbm, w_out_ref, stage_out, sem_out)

    vec = vec_ref[0]
    starts = list(range(0, h_ref.shape[0], sub))

    def prenorm(r0):
        return _prenorm_mod(h_ref[r0:r0 + sub, :], vec).astype(BF16)

    def finish(r0, y):
        o_ref[r0:r0 + sub, :] = _postnorm_residual(h_ref[r0:r0 + sub, :], y, vec, res)

    n = prenorm(starts[0])
    pending = None
    for idx, r0 in enumerate(starts):
        y = None
        off = 0
        n_next = None
        for ci, width in enumerate(chunks):
            g = _dot(n, w_in_ref[:, off:off + width])
            u = _dot(n, w_in_ref[:, d_ff + off:d_ff + off + width])
            a = (g / (1.0 + jnp.exp(-g)) * u).astype(BF16)
            part = _dot(a, w_out_ref[off:off + width, :])
            y = part if y is None else y + part
            off += width
            if ci == 0 and pending is not None:
                finish(*pending)
                pending = None
            if ci == 1 and idx + 1 < len(starts):
                n_next = prenorm(starts[idx + 1])
        pending = (r0, y)
        n = n_next
    finish(*pending)


def _ffn(h, vec, w_in, w_out, seq_len):
    t, d = h.shape
    tm = min(FFN_TOKEN_TILE, seq_len)
    tiles_per_seq = seq_len // tm
    d_ff = w_out.shape[0]
    kern = functools.partial(_ffn_kernel, sub=min(FFN_SUB_TILE, tm), chunks=FFN_CHUNKS, d_ff=d_ff,
                             res=FFN_RES)
    return pl.pallas_call(
        kern,
        grid=(t // tm,),
        in_specs=[pl.BlockSpec((tm, d), lambda i: (i, 0)),
                  pl.BlockSpec((1, 8, d), lambda i: (i // tiles_per_seq, 0, 0)),
                  pl.BlockSpec(memory_space=pl.ANY),
                  pl.BlockSpec(memory_space=pl.ANY)],
        out_specs=pl.BlockSpec((tm, d), lambda i: (i, 0)),
        out_shape=jax.ShapeDtypeStruct((t, d), F32),
        scratch_shapes=[pltpu.VMEM(w_in.shape, BF16), pltpu.VMEM(w_out.shape, BF16),
                        pltpu.VMEM((STAGE_SLOTS, WIDE_STAGE_ROWS, w_in.shape[1]), F32),
                        pltpu.VMEM((STAGE_SLOTS, STAGE_ROWS, w_out.shape[1]), F32),
                        pltpu.SemaphoreType.DMA((STAGE_SLOTS,)),
                        pltpu.SemaphoreType.DMA((STAGE_SLOTS,))],
        compiler_params=pltpu.CompilerParams(dimension_semantics=("arbitrary",),
                                             vmem_limit_bytes=VMEM_LIMIT_BYTES),
        name="ffn_half_step",
    )(h, vec, w_in, w_out)


def _mixer_kernel(rel_ref, h_ref, vec_ref, bucket_ref, sink_ref, ln_ref, bsp_ref, ws_ref, w_in_ref,
                  wba_ref, wbg_ref, wo_ref, o_ref, carry_ref, attn_ref, sp_ref, bias_ref,
                  *, tm, sub, tiles_per_seq):
    first = (pl.program_id(0) % tiles_per_seq) == 0

    @pl.when(pl.program_id(0) == 0)
    def _():
        bucket = bucket_ref[...]
        hits = [bucket == bk for bk in range(N_BUCKETS)]
        for hd in range(N_Q_HEADS):
            acc = jnp.full(bucket.shape, NEG, F32)
            for bk in range(N_BUCKETS):
                acc = jnp.where(hits[bk], rel_ref[bk, hd], acc)
            kvh, g = divmod(hd, GROUP)
            bias_ref[2 * kvh + g % 2, (g // 2) * BLOCK:(g // 2 + 1) * BLOCK, :] = acc

    @pl.when(first)
    def _():
        carry_ref[...] = jnp.zeros_like(carry_ref)

    for r in range(0, tm, sub):
        rows = pl.ds(r, sub)
        _mixer_tile(first if r == 0 else None, h_ref.at[rows], vec_ref, sink_ref, ln_ref, bsp_ref,
                    ws_ref, w_in_ref, wba_ref, wbg_ref, wo_ref, o_ref.at[rows], carry_ref,
                    attn_ref.at[rows], sp_ref.at[rows], bias_ref, tm=sub)


def _mixer_tile(first, h_ref, vec_ref, sink_ref, ln_ref, bsp_ref, ws_ref, w_in_ref, wba_ref, wbg_ref,
                wo_ref, o_ref, carry_ref, attn_ref, sp_ref, bias_ref, *, tm):
    nb = tm // BLOCK
    x = h_ref[...]
    vec = vec_ref[0]
    n = _prenorm_mod(x, vec).astype(BF16)

    lo = lax.broadcasted_iota(jnp.int32, (1, LANES), 1) < HEAD_DIM

    q = (_dot(n, w_in_ref[:, 0:Q_W]) * (HEAD_DIM ** -0.5)).astype(BF16)
    w_kv = w_in_ref[:, Q_W:Q_W + 2 * KV_W]
    kv = jnp.concatenate([_dot(n[:tm // 2], w_kv), _dot(n[tm // 2:], w_kv)], axis=0)
    k_f, v_f = kv[:, :KV_W], kv[:, KV_W:]
    k_cur, k_rot = k_f.astype(BF16), pltpu.roll(k_f, HEAD_DIM, 1).astype(BF16)
    v_cur, v_rot = v_f.astype(BF16), pltpu.roll(v_f, HEAD_DIM, 1).astype(BF16)
    k_all = jnp.concatenate([carry_ref[0], k_cur], axis=0)
    k_all_rot = jnp.concatenate([carry_ref[1], k_rot], axis=0)
    v_all = jnp.concatenate([carry_ref[2], v_cur], axis=0)
    v_all_rot = jnp.concatenate([carry_ref[3], v_rot], axis=0)
    carry_ref[0] = k_cur[tm - BLOCK:]
    carry_ref[1] = k_rot[tm - BLOCK:]
    carry_ref[2] = v_cur[tm - BLOCK:]
    carry_ref[3] = v_rot[tm - BLOCK:]
    zero = jnp.zeros((), BF16)
    k_var = ((jnp.where(lo, k_all, zero), jnp.where(lo, zero, k_all_rot)),
             (jnp.where(lo, k_all_rot, zero), jnp.where(lo, zero, k_all)))
    v_var = ((jnp.where(lo, v_all, zero), jnp.where(lo, zero, v_all_rot)),
             (jnp.where(lo, v_all_rot, zero), jnp.where(lo, zero, v_all)))

    col = lax.broadcasted_iota(jnp.int32, (1, 2 * BLOCK), 1)
    no_prev = None if first is None else jnp.where(jnp.logical_and(first, col < BLOCK), NEG, 0.0)

    scores = []
    for b in range(nb):
        r0 = b * BLOCK
        keys = slice(r0, r0 + 2 * BLOCK)
        per_block = []
        for kvh in range(N_KV_HEADS):
            c0 = kvh * GROUP * HEAD_DIM
            q2 = jnp.concatenate([q[r0:r0 + BLOCK, c0:c0 + LANES],
                                  q[r0:r0 + BLOCK, c0 + LANES:c0 + 2 * LANES]], axis=0)
            for par in range(2):
                s = _dot_nt(q2, k_var[kvh][par][keys]) + bias_ref[2 * kvh + par]
                per_block.append(s + no_prev if (b == 0 and no_prev is not None) else s)
        scores.append(jnp.stack(per_block))
    sink = sink_ref[...]
    g0 = Q_W + 2 * KV_W + 2 * GMLP_WIDTH

    def softmax_block(b):
        s = scores[b]
        m = jnp.max(s, axis=-1, keepdims=True)
        return jnp.exp(s - m).astype(BF16), jnp.exp(sink - m)

    soft = [None] * nb
    zg = _dot(n, w_in_ref[:, Q_W + 2 * KV_W:g0])
    for b in range(0, nb // 2):
        soft[b] = softmax_block(b)
    ga = _dot(n, w_in_ref[:, g0:g0 + D_MODEL])
    for b in range(nb // 2, nb):
        soft[b] = softmax_block(b)
    gg = _dot(n, w_in_ref[:, g0 + D_MODEL:g0 + 2 * D_MODEL])

    rows2 = lax.broadcasted_iota(jnp.int32, (4 * BLOCK, LANES), 0) < 2 * BLOCK
    ones_cols = jnp.where(rows2 == lo, 1.0, 0.0).astype(BF16)
    for b in range(nb):
        r0 = b * BLOCK
        keys = slice(r0, r0 + 2 * BLOCK)
        probs, sink_term = soft[b]
        for kvh in range(N_KV_HEADS):
            c0 = kvh * GROUP * HEAD_DIM
            p2 = jnp.concatenate([probs[2 * kvh], probs[2 * kvh + 1]], axis=1)
            v2 = jnp.concatenate([v_var[kvh][0][keys], v_var[kvh][1][keys]], axis=0)
            o2 = _dot(p2, jnp.concatenate([v2, ones_cols], axis=1))
            denom = o2[:, LANES:] + jnp.where(lo, sink_term[2 * kvh], sink_term[2 * kvh + 1])
            o2 = o2[:, :LANES] / denom
            attn_ref[r0:r0 + BLOCK, c0:c0 + LANES] = o2[:BLOCK].astype(BF16)
            attn_ref[r0:r0 + BLOCK, c0 + LANES:c0 + 2 * LANES] = o2[BLOCK:].astype(BF16)

    zg = 0.5 * zg * (1.0 + jnp.tanh(math.sqrt(2.0 / math.pi) * (zg + 0.044715 * (zg * zg * zg))))
    u, vr = zg[:, :GMLP_WIDTH], zg[:, GMLP_WIDTH:]
    ln = ln_ref[...]
    mu = jnp.mean(vr, axis=-1, keepdims=True)
    vc = vr - mu
    var = jnp.mean(vc * vc, axis=-1, keepdims=True)
    vg = (vc * lax.rsqrt(var + EPS) * ln[0:1] + ln[1:2]).astype(BF16)
    row = lax.broadcasted_iota(jnp.int32, (BLOCK, 2 * BLOCK), 0)
    col2 = lax.broadcasted_iota(jnp.int32, (BLOCK, 2 * BLOCK), 1)
    causal = jnp.where(col2 < BLOCK, col2, col2 - BLOCK) <= row
    for j in range(GMLP_GROUPS // 2):
        w_pair = jnp.where(causal, ws_ref[j], zero)
        pieces = [vg[c * BLOCK:(c + 1) * BLOCK, j * LANES:(j + 1) * LANES] for c in range(nb)]
        rhs = jnp.concatenate(
            [jnp.concatenate([jnp.where(lo, pc, zero) for pc in pieces], axis=1),
             jnp.concatenate([jnp.where(lo, zero, pc) for pc in pieces], axis=1)], axis=0)
        out = _dot(w_pair, rhs)
        for c in range(nb):
            sp_ref[c * BLOCK:(c + 1) * BLOCK, j * LANES:(j + 1) * LANES] = (
                out[:, c * LANES:(c + 1) * LANES])
    bsp = bsp_ref[...]
    gm = jnp.concatenate(
        [u[c * BLOCK:(c + 1) * BLOCK] * (sp_ref[c * BLOCK:(c + 1) * BLOCK, :] + bsp)
         for c in range(nb)], axis=0).astype(BF16)

    ya = _dot(attn_ref[...], wba_ref[...])
    yg = _dot(gm, wbg_ref[...])
    mix = (ya / (1.0 + jnp.exp(-ga)) + yg / (1.0 + jnp.exp(-gg))).astype(BF16)
    y = _dot(mix, wo_ref[...])
    o_ref[...] = _postnorm_residual(x, y, vec, 1.0)


def _mixer(rel, h, vec, bucket, sink_t, ln, bsp, ws_t, w_in, wba, wbg, wo, seq_len):
    t, d = h.shape
    tm = min(MIX_TOKEN_TILE, seq_len)
    tiles_per_seq = seq_len // tm
    kern = functools.partial(_mixer_kernel, tm=tm, sub=min(MIX_SUB_TILE, tm),
                             tiles_per_seq=tiles_per_seq)
    return pl.pallas_call(
        kern,
        grid=(t // tm,),
        in_specs=[pl.BlockSpec(memory_space=pltpu.SMEM),
                  pl.BlockSpec((tm, d), lambda i: (i, 0)),
                  pl.BlockSpec((1, 8, d), lambda i: (i // tiles_per_seq, 0, 0)),
                  _resident(bucket.shape), _resident(sink_t.shape), _resident(ln.shape),
                  _resident(bsp.shape), _resident(ws_t.shape), _resident(w_in.shape),
                  _resident(wba.shape), _resident(wbg.shape), _resident(wo.shape)],
        out_specs=pl.BlockSpec((tm, d), lambda i: (i, 0)),
        out_shape=jax.ShapeDtypeStruct((t, d), F32),
        scratch_shapes=[pltpu.VMEM((4, BLOCK, KV_W), BF16),
                        pltpu.VMEM((tm, Q_W), BF16),
                        pltpu.VMEM((tm, GMLP_WIDTH), F32),
                        pltpu.VMEM((4, 2 * BLOCK, 2 * BLOCK), F32)],
        compiler_params=pltpu.CompilerParams(dimension_semantics=("arbitrary",),
                                             vmem_limit_bytes=VMEM_LIMIT_BYTES),
        name="token_mixers",
    )(rel, h, vec, bucket, sink_t, ln, bsp, ws_t, w_in, wba, wbg, wo)


def _t5_bucket_table():
    qi = np.arange(BLOCK, dtype=np.int32)[:, None]
    kj = np.arange(2 * BLOCK, dtype=np.int32)[None, :]
    dist = qi + BLOCK - kj
    in_window = (dist >= 0) & (dist < BLOCK)
    dc = np.maximum(dist, 0)
    max_exact = N_BUCKETS // 2
    d_f = np.maximum(dc, max_exact).astype(np.float32)
    large = max_exact + (np.log(d_f / np.float32(max_exact)).astype(np.float32)
                         / np.float32(math.log(MAX_DISTANCE / max_exact))
                         * np.float32(N_BUCKETS - max_exact)).astype(np.int32)
    large = np.minimum(large, N_BUCKETS - 1)
    return np.where(in_window, np.where(dc < max_exact, dc, large), -1).astype(np.int32)


def _pair_rows(a):
    out = []
    for kvh in range(N_KV_HEADS):
        for par in range(2):
            out.append(jnp.concatenate([a[GROUP * kvh + par], a[GROUP * kvh + 2 + par]], axis=0))
    return jnp.stack(out)


def kernel(x, c, rel_bias, w_ada, b_ada, pre_norm_g, post_norm_g, w_ffn1_in, w_ffn1_out, w_in,
           sinks, gmlp_ln_g, gmlp_ln_b, gmlp_w_s, gmlp_b_s, w_br_attn, w_br_gmlp, w_out,
           w_ffn2_in, w_ffn2_out):
    bsz, seq, d = x.shape
    depth = w_ada.shape[0]
    bucket = jnp.asarray(_t5_bucket_table())
    h = x.reshape(bsz * seq, d)
    for l in range(depth):
        ada = _ada(c, w_ada[l], b_ada[l]).reshape(bsz, N_ADA, d)

        def vec_for(sub):
            g = jnp.broadcast_to(jnp.stack([pre_norm_g[l, sub], post_norm_g[l, sub]])[None],
                                 (bsz, 2, d))
            return jnp.concatenate([ada[:, 3 * sub:3 * sub + 3], g,
                                    jnp.zeros((bsz, 3, d), F32)], axis=1)

        sink_t = _pair_rows(jnp.broadcast_to(sinks[l].astype(F32)[:, None, None],
                                             (N_Q_HEADS, BLOCK, LANES)))
        ln = jnp.stack([gmlp_ln_g[l], gmlp_ln_b[l]])
        bsp = jnp.repeat(jnp.transpose(gmlp_b_s[l]), GMLP_WIDTH // GMLP_GROUPS, axis=1)
        ws_t = jnp.concatenate([gmlp_w_s[l, 0::2], gmlp_w_s[l, 1::2]], axis=2).astype(BF16)

        h = _ffn(h, vec_for(0), w_ffn1_in[l], w_ffn1_out[l], seq)
        h = _mixer(rel_bias.astype(F32), h, vec_for(1), bucket, sink_t, ln, bsp, ws_t,
                   w_in[l].astype(BF16), w_br_attn[l].astype(BF16), w_br_gmlp[l].astype(BF16),
                   w_out[l].astype(BF16), seq)
        h = _ffn(h, vec_for(2), w_ffn2_in[l], w_ffn2_out[l], seq)
    return h.reshape(bsz, seq, d)
```

```python
import functools
import math

import numpy as np
import jax
import jax.numpy as jnp
from jax import lax
from jax.experimental import pallas as pl
from jax.experimental.pallas import tpu as pltpu

D_MODEL = 1024
N_Q_HEADS = 8
N_KV_HEADS = 2
GROUP = N_Q_HEADS // N_KV_HEADS
HEAD_DIM = 64
BLOCK = 128
Q_W = N_Q_HEADS * HEAD_DIM
KV_W = N_KV_HEADS * HEAD_DIM
N_BUCKETS = 32
MAX_DISTANCE = 128
GMLP_GROUPS = 8
GMLP_WIDTH = 512
D_FF = 2816
FFN_RES = 0.5
N_ADA = 9
EPS = 1e-6
NEG = -1e30

LANES = 128
VMEM_LIMIT_BYTES = 56 * 1024 * 1024
FFN_TOKEN_TILE = 1024
FFN_SUB_TILE = 512
WIDE_STAGE_ROWS = 64
STAGE_ROWS = 256
STAGE_SLOTS = 3
MIX_TOKEN_TILE = 1024
MIX_SUB_TILE = 512
FFN_CHUNKS = (768, 768, 768, 512)

BF16 = jnp.bfloat16
F32 = jnp.float32


def _dot(a, b):
    return jnp.dot(a, b, preferred_element_type=F32)


def _dot_nt(a, b):
    return lax.dot_general(a, b, (((1,), (1,)), ((), ())), preferred_element_type=F32)


def _resident(shape):
    nd = len(shape)
    return pl.BlockSpec(shape, lambda i: (0,) * nd, pipeline_mode=pl.Buffered(1))


def _prenorm_mod(x, vec):
    r = lax.rsqrt(jnp.mean(x * x, axis=-1, keepdims=True) + EPS)
    gs = vec[3:4] * (1.0 + vec[1:2])
    return x * r * gs + vec[0:1]


def _postnorm_residual(x, y, vec, res):
    r = lax.rsqrt(jnp.mean(y * y, axis=-1, keepdims=True) + EPS)
    return x + (res * vec[2:3]) * (y * r * vec[4:5])


def _ada_kernel(c_ref, w_ref, b_ref, o_ref):
    c = c_ref[...]
    s = c / (1.0 + jnp.exp(-c))
    o_ref[...] = _dot(s.astype(BF16), w_ref[...].astype(BF16)) + b_ref[...]


def _ada(c, w_ada, b_ada):
    b, d = c.shape
    n = w_ada.shape[1]
    rows = 16
    tn = n // 8
    c_pad = jnp.zeros((rows, d), F32).at[:b].set(c)
    out = pl.pallas_call(
        _ada_kernel,
        grid=(n // tn,),
        in_specs=[pl.BlockSpec((rows, d), lambda j: (0, 0)),
                  pl.BlockSpec((d, tn), lambda j: (0, j)),
                  pl.BlockSpec((1, tn), lambda j: (0, j))],
        out_specs=pl.BlockSpec((rows, tn), lambda j: (0, j)),
        out_shape=jax.ShapeDtypeStruct((rows, n), F32),
        compiler_params=pltpu.CompilerParams(dimension_semantics=("arbitrary",),
                                             vmem_limit_bytes=VMEM_LIMIT_BYTES),
        name="ada_proj",
    )(c_pad, w_ada, b_ada.reshape(1, n))
    return out[:b]


def _load_as_bf16(src_hbm, dst_ref, stage_ref, sem):
    slots, rows = stage_ref.shape[0], stage_ref.shape[1]
    n_chunks = dst_ref.shape[0] // rows

    def copy(c):
        return pltpu.make_async_copy(src_hbm.at[pl.ds(c * rows, rows)], stage_ref.at[c % slots],
                                     sem.at[c % slots])

    for c in range(min(slots - 1, n_chunks)):
        copy(c).start()
    for c in range(n_chunks):
        if c + slots - 1 < n_chunks:
            copy(c + slots - 1).start()
        copy(c).wait()
        dst_ref[pl.ds(c * rows, rows), :] = stage_ref[c % slots].astype(BF16)


def _ffn_kernel(h_ref, vec_ref, w_in_hbm, w_out_hbm, o_ref, w_in_ref, w_out_ref, stage_in, stage_out,
                sem_in, sem_out, *, sub, chunks, d_ff, res):
    @pl.when(pl.program_id(0) == 0)
    def _():
        _load_as_bf16(w_in_hbm, w_in_ref, stage_in, sem_in)
        _load_as_bf16(w_out_hbm, w_out_ref, stage_out, sem_out)

    vec = vec_ref[0]
    starts = list(range(0, h_ref.shape[0], sub))

    def prenorm(r0):
        return _prenorm_mod(h_ref[r0:r0 + sub, :], vec).astype(BF16)

    def finish(r0, y):
        o_ref[r0:r0 + sub, :] = _postnorm_residual(h_ref[r0:r0 + sub, :], y, vec, res)

    n = prenorm(starts[0])
    pending = None
    for idx, r0 in enumerate(starts):
        y = None
        off = 0
        n_next = None
        for ci, width in enumerate(chunks):
            g = _dot(n, w_in_ref[:, off:off + width])
            u = _dot(n, w_in_ref[:, d_ff + off:d_ff + off + width])
            a = (g / (1.0 + jnp.exp(-g)) * u).astype(BF16)
            part = _dot(a, w_out_ref[off:off + width, :])
            y = part if y is None else y + part
            off += width
            if ci == 0 and pending is not None:
                finish(*pending)
                pending = None
            if ci == 1 and idx + 1 < len(starts):
                n_next = prenorm(starts[idx + 1])
        pending = (r0, y)
        n = n_next
    finish(*pending)


def _ffn(h, vec, w_in, w_out, seq_len):
    t, d = h.shape
    tm = min(FFN_TOKEN_TILE, seq_len)
    tiles_per_seq = seq_len // tm
    d_ff = w_out.shape[0]
    kern = functools.partial(_ffn_kernel, sub=min(FFN_SUB_TILE, tm), chunks=FFN_CHUNKS, d_ff=d_ff,
                             res=FFN_RES)
    return pl.pallas_call(
        kern,
        grid=(t // tm,),
        in_specs=[pl.BlockSpec((tm, d), lambda i: (i, 0)),
                  pl.BlockSpec((1, 8, d), lambda i: (i // tiles_per_seq, 0, 0)),
                  pl.BlockSpec(memory_space=pl.ANY),
                  pl.BlockSpec(memory_space=pl.ANY)],
        out_specs=pl.BlockSpec((tm, d), lambda i: (i, 0)),
        out_shape=jax.ShapeDtypeStruct((t, d), F32),
        scratch_shapes=[pltpu.VMEM(w_in.shape, BF16), pltpu.VMEM(w_out.shape, BF16),
                        pltpu.VMEM((STAGE_SLOTS, WIDE_STAGE_ROWS, w_in.shape[1]), F32),
                        pltpu.VMEM((STAGE_SLOTS, STAGE_ROWS, w_out.shape[1]), F32),
                        pltpu.SemaphoreType.DMA((STAGE_SLOTS,)),
                        pltpu.SemaphoreType.DMA((STAGE_SLOTS,))],
        compiler_params=pltpu.CompilerParams(dimension_semantics=("arbitrary",),
                                             vmem_limit_bytes=VMEM_LIMIT_BYTES),
        name="ffn_half_step",
    )(h, vec, w_in, w_out)


def _mixer_kernel(rel_ref, h_ref, vec_ref, bucket_ref, sink_ref, ln_ref, bsp_ref, ws_ref, w_in_ref,
                  wba_ref, wbg_ref, wo_ref, o_ref, carry_ref, attn_ref, sp_ref, bias_ref,
                  *, tm, sub, tiles_per_seq):
    first = (pl.program_id(0) % tiles_per_seq) == 0

    @pl.when(pl.program_id(0) == 0)
    def _():
        bucket = bucket_ref[...]
        hits = [bucket == bk for bk in range(N_BUCKETS)]
        for hd in range(N_Q_HEADS):
            acc = jnp.full(bucket.shape, NEG, F32)
            for bk in range(N_BUCKETS):
                acc = jnp.where(hits[bk], rel_ref[bk, hd], acc)
            kvh, g = divmod(hd, GROUP)
            bias_ref[2 * kvh + g % 2, (g // 2) * BLOCK:(g // 2 + 1) * BLOCK, :] = acc

    @pl.when(first)
    def _():
        carry_ref[...] = jnp.zeros_like(carry_ref)

    for r in range(0, tm, sub):
        rows = pl.ds(r, sub)
        _mixer_tile(first if r == 0 else None, h_ref.at[rows], vec_ref, sink_ref, ln_ref, bsp_ref,
                    ws_ref, w_in_ref, wba_ref, wbg_ref, wo_ref, o_ref.at[rows], carry_ref,
                    attn_ref.at[rows], sp_ref.at[rows], bias_ref, tm=sub)


def _mixer_tile(first, h_ref, vec_ref, sink_ref, ln_ref, bsp_ref, ws_ref, w_in_ref, wba_ref, wbg_ref,
                wo_ref, o_ref, carry_ref, attn_ref, sp_ref, bias_ref, *, tm):
    nb = tm // BLOCK
    x = h_ref[...]
    vec = vec_ref[0]
    n = _prenorm_mod(x, vec).astype(BF16)

    lo = lax.broadcasted_iota(jnp.int32, (1, LANES), 1) < HEAD_DIM

    q = (_dot(n, w_in_ref[:, 0:Q_W]) * (HEAD_DIM ** -0.5)).astype(BF16)
    w_kv = w_in_ref[:, Q_W:Q_W + 2 * KV_W]
    kv = jnp.concatenate([_dot(n[:tm // 2], w_kv), _dot(n[tm // 2:], w_kv)], axis=0)
    k_f, v_f = kv[:, :KV_W], kv[:, KV_W:]
    k_cur, k_rot = k_f.astype(BF16), pltpu.roll(k_f, HEAD_DIM, 1).astype(BF16)
    v_cur, v_rot = v_f.astype(BF16), pltpu.roll(v_f, HEAD_DIM, 1).astype(BF16)
    k_all = jnp.concatenate([carry_ref[0], k_cur], axis=0)
    k_all_rot = jnp.concatenate([carry_ref[1], k_rot], axis=0)
    v_all = jnp.concatenate([carry_ref[2], v_cur], axis=0)
    v_all_rot = jnp.concatenate([carry_ref[3], v_rot], axis=0)
    carry_ref[0] = k_cur[tm - BLOCK:]
    carry_ref[1] = k_rot[tm - BLOCK:]
    carry_ref[2] = v_cur[tm - BLOCK:]
    carry_ref[3] = v_rot[tm - BLOCK:]
    zero = jnp.zeros((), BF16)
    k_var = ((jnp.where(lo, k_all, zero), jnp.where(lo, zero, k_all_rot)),
             (jnp.where(lo, k_all_rot, zero), jnp.where(lo, zero, k_all)))
    v_var = ((jnp.where(lo, v_all, zero), jnp.where(lo, zero, v_all_rot)),
             (jnp.where(lo, v_all_rot, zero), jnp.where(lo, zero, v_all)))

    col = lax.broadcasted_iota(jnp.int32, (1, 2 * BLOCK), 1)
    no_prev = None if first is None else jnp.where(jnp.logical_and(first, col < BLOCK), NEG, 0.0)

    scores = []
    for b in range(nb):
        r0 = b * BLOCK
        keys = slice(r0, r0 + 2 * BLOCK)
        per_block = []
        for kvh in range(N_KV_HEADS):
            c0 = kvh * GROUP * HEAD_DIM
            q2 = jnp.concatenate([q[r0:r0 + BLOCK, c0:c0 + LANES],
                                  q[r0:r0 + BLOCK, c0 + LANES:c0 + 2 * LANES]], axis=0)
            for par in range(2):
                s = _dot_nt(q2, k_var[kvh][par][keys]) + bias_ref[2 * kvh + par]
                per_block.append(s + no_prev if (b == 0 and no_prev is not None) else s)
        scores.append(jnp.stack(per_block))
    sink = sink_ref[...]
    g0 = Q_W + 2 * KV_W + 2 * GMLP_WIDTH

    def softmax_block(b):
        s = scores[b]
        m = jnp.max(s, axis=-1, keepdims=True)
        return jnp.exp(s - m).astype(BF16), jnp.exp(sink - m)

    soft = [None] * nb
    zg = _dot(n, w_in_ref[:, Q_W + 2 * KV_W:g0])
    for b in range(0, nb // 2):
        soft[b] = softmax_block(b)
    ga = _dot(n, w_in_ref[:, g0:g0 + D_MODEL])
    for b in range(nb // 2, nb):
        soft[b] = softmax_block(b)
    gg = _dot(n, w_in_ref[:, g0 + D_MODEL:g0 + 2 * D_MODEL])

    rows2 = lax.broadcasted_iota(jnp.int32, (4 * BLOCK, LANES), 0) < 2 * BLOCK
    ones_cols = jnp.where(rows2 == lo, 1.0, 0.0).astype(BF16)
    for b in range(nb):
        r0 = b * BLOCK
        keys = slice(r0, r0 + 2 * BLOCK)
        probs, sink_term = soft[b]
        for kvh in range(N_KV_HEADS):
            c0 = kvh * GROUP * HEAD_DIM
            p2 = jnp.concatenate([probs[2 * kvh], probs[2 * kvh + 1]], axis=1)
            v2 = jnp.concatenate([v_var[kvh][0][keys], v_var[kvh][1][keys]], axis=0)
            o2 = _dot(p2, jnp.concatenate([v2, ones_cols], axis=1))
            denom = o2[:, LANES:] + jnp.where(lo, sink_term[2 * kvh], sink_term[2 * kvh + 1])
            o2 = o2[:, :LANES] / denom
            attn_ref[r0:r0 + BLOCK, c0:c0 + LANES] = o2[:BLOCK].astype(BF16)
            attn_ref[r0:r0 + BLOCK, c0 + LANES:c0 + 2 * LANES] = o2[BLOCK:].astype(BF16)

    zg = 0.5 * zg * (1.0 + jnp.tanh(math.sqrt(2.0 / math.pi) * (zg + 0.044715 * (zg * zg * zg))))
    u, vr = zg[:, :GMLP_WIDTH], zg[:, GMLP_WIDTH:]
    ln = ln_ref[...]
    mu = jnp.mean(vr, axis=-1, keepdims=True)
    vc = vr - mu
    var = jnp.mean(vc * vc, axis=-1, keepdims=True)
    vg = (vc * lax.rsqrt(var + EPS) * ln[0:1] + ln[1:2]).astype(BF16)
    row = lax.broadcasted_iota(jnp.int32, (BLOCK, 2 * BLOCK), 0)
    col2 = lax.broadcasted_iota(jnp.int32, (BLOCK, 2 * BLOCK), 1)
    causal = jnp.where(col2 < BLOCK, col2, col2 - BLOCK) <= row
    for j in range(GMLP_GROUPS // 2):
        w_pair = jnp.where(causal, ws_ref[j], zero)
        pieces = [vg[c * BLOCK:(c + 1) * BLOCK, j * LANES:(j + 1) * LANES] for c in range(nb)]
        rhs = jnp.concatenate(
            [jnp.concatenate([jnp.where(lo, pc, zero) for pc in pieces], axis=1),
             jnp.concatenate([jnp.where(lo, zero, pc) for pc in pieces], axis=1)], axis=0)
        out = _dot(w_pair, rhs)
        for c in range(nb):
            sp_ref[c * BLOCK:(c + 1) * BLOCK, j * LANES:(j + 1) * LANES] = (
                out[:, c * LANES:(c + 1) * LANES])
    bsp = bsp_ref[...]
    gm = jnp.concatenate(
        [u[c * BLOCK:(c + 1) * BLOCK] * (sp_ref[c * BLOCK:(c + 1) * BLOCK, :] + bsp)
         for c in range(nb)], axis=0).astype(BF16)

    ya = _dot(attn_ref[...], wba_ref[...])
    yg = _dot(gm, wbg_ref[...])
    mix = (ya / (1.0 + jnp.exp(-ga)) + yg / (1.0 + jnp.exp(-gg))).astype(BF16)
    y = _dot(mix, wo_ref[...])
    o_ref[...] = _postnorm_residual(x, y, vec, 1.0)


def _mixer(rel, h, vec, bucket, sink_t, ln, bsp, ws_t, w_in, wba, wbg, wo, seq_len):
    t, d = h.shape
    tm = min(MIX_TOKEN_TILE, seq_len)
    tiles_per_seq = seq_len // tm
    kern = functools.partial(_mixer_kernel, tm=tm, sub=min(MIX_SUB_TILE, tm),
                             tiles_per_seq=tiles_per_seq)
    return pl.pallas_call(
        kern,
        grid=(t // tm,),
        in_specs=[pl.BlockSpec(memory_space=pltpu.SMEM),
                  pl.BlockSpec((tm, d), lambda i: (i, 0)),
                  pl.BlockSpec((1, 8, d), lambda i: (i // tiles_per_seq, 0, 0)),
                  _resident(bucket.shape), _resident(sink_t.shape), _resident(ln.shape),
                  _resident(bsp.shape), _resident(ws_t.shape), _resident(w_in.shape),
                  _resident(wba.shape), _resident(wbg.shape), _resident(wo.shape)],
        out_specs=pl.BlockSpec((tm, d), lambda i: (i, 0)),
        out_shape=jax.ShapeDtypeStruct((t, d), F32),
        scratch_shapes=[pltpu.VMEM((4, BLOCK, KV_W), BF16),
                        pltpu.VMEM((tm, Q_W), BF16),
                        pltpu.VMEM((tm, GMLP_WIDTH), F32),
                        pltpu.VMEM((4, 2 * BLOCK, 2 * BLOCK), F32)],
        compiler_params=pltpu.CompilerParams(dimension_semantics=("arbitrary",),
                                             vmem_limit_bytes=VMEM_LIMIT_BYTES),
        name="token_mixers",
    )(rel, h, vec, bucket, sink_t, ln, bsp, ws_t, w_in, wba, wbg, wo)


def _t5_bucket_table():
    qi = np.arange(BLOCK, dtype=np.int32)[:, None]
    kj = np.arange(2 * BLOCK, dtype=np.int32)[None, :]
    dist = qi + BLOCK - kj
    in_window = (dist >= 0) & (dist < BLOCK)
    dc = np.maximum(dist, 0)
    max_exact = N_BUCKETS // 2
    d_f = np.maximum(dc, max_exact).astype(np.float32)
    large = max_exact + (np.log(d_f / np.float32(max_exact)).astype(np.float32)
                         / np.float32(math.log(MAX_DISTANCE / max_exact))
                         * np.float32(N_BUCKETS - max_exact)).astype(np.int32)
    large = np.minimum(large, N_BUCKETS - 1)
    return np.where(in_window, np.where(dc < max_exact, dc, large), -1).astype(np.int32)


def _pair_rows(a):
    out = []
    for kvh in range(N_KV_HEADS):
        for par in range(2):
            out.append(jnp.concatenate([a[GROUP * kvh + par], a[GROUP * kvh + 2 + par]], axis=0))
    return jnp.stack(out)


def kernel(x, c, rel_bias, w_ada, b_ada, pre_norm_g, post_norm_g, w_ffn1_in, w_ffn1_out, w_in,
           sinks, gmlp_ln_g, gmlp_ln_b, gmlp_w_s, gmlp_b_s, w_br_attn, w_br_gmlp, w_out,
           w_ffn2_in, w_ffn2_out):
    bsz, seq, d = x.shape
    depth = w_ada.shape[0]
    bucket = jnp.asarray(_t5_bucket_table())
    h = x.reshape(bsz * seq, d)
    for l in range(depth):
        ada = _ada(c, w_ada[l], b_ada[l]).reshape(bsz, N_ADA, d)

        def vec_for(sub):
            g = jnp.broadcast_to(jnp.stack([pre_norm_g[l, sub], post_norm_g[l, sub]])[None],
                                 (bsz, 2, d))
            return jnp.concatenate([ada[:, 3 * sub:3 * sub + 3], g,
                                    jnp.zeros((bsz, 3, d), F32)], axis=1)

        sink_t = _pair_rows(jnp.broadcast_to(sinks[l].astype(F32)[:, None, None],
                                             (N_Q_HEADS, BLOCK, LANES)))
        ln = jnp.stack([gmlp_ln_g[l], gmlp_ln_b[l]])
        bsp = jnp.repeat(jnp.transpose(gmlp_b_s[l]), GMLP_WIDTH // GMLP_GROUPS, axis=1)
        ws_t = jnp.concatenate([gmlp_w_s[l, 0::2], gmlp_w_s[l, 1::2]], axis=2).astype(BF16)

        h = _ffn(h, vec_for(0), w_ffn1_in[l], w_ffn1_out[l], seq)
        h = _mixer(rel_bias.astype(F32), h, vec_for(1), bucket, sink_t, ln, bsp, ws_t,
                   w_in[l].astype(BF16), w_br_attn[l].astype(BF16), w_br_gmlp[l].astype(BF16),
                   w_out[l].astype(BF16), seq)
        h = _ffn(h, vec_for(2), w_ffn2_in[l], w_ffn2_out[l], seq)
    return h.reshape(bsz, seq, d)
```

```python
import functools
import math

import numpy as np
import jax
import jax.numpy as jnp
from jax import lax
from jax.experimental import pallas as pl
from jax.experimental.pallas import tpu as pltpu

D_MODEL = 1024
N_Q_HEADS = 8
N_KV_HEADS = 2
GROUP = N_Q_HEADS // N_KV_HEADS
HEAD_DIM = 64
BLOCK = 128
Q_W = N_Q_HEADS * HEAD_DIM
KV_W = N_KV_HEADS * HEAD_DIM
N_BUCKETS = 32
MAX_DISTANCE = 128
GMLP_GROUPS = 8
GMLP_WIDTH = 512
D_FF = 2816
FFN_RES = 0.5
N_ADA = 9
EPS = 1e-6
NEG = -1e30

LANES = 128
BF16_SUBLANES = 16
VMEM_LIMIT_BYTES = 56 * 1024 * 1024
FFN_TOKEN_TILE = 1024
FFN_SUB_TILE = 512
WIDE_STAGE_ROWS = 64
STAGE_ROWS = 256
STAGE_SLOTS = 3
MIX_TOKEN_TILE = 1024
MIX_SUB_TILE = 512
FFN_CHUNKS = (512, 512, 512, 512, 512, 256)

BF16 = jnp.bfloat16
F32 = jnp.float32


def _dot(a, b):
    return jnp.dot(a, b, preferred_element_type=F32)


def _dot_nt(a, b):
    return lax.dot_general(a, b, (((1,), (1,)), ((), ())), preferred_element_type=F32)


def _resident(shape):
    nd = len(shape)
    return pl.BlockSpec(shape, lambda i: (0,) * nd, pipeline_mode=pl.Buffered(1))


def _prenorm_mod(x, vec):
    r = lax.rsqrt(jnp.mean(x * x, axis=-1, keepdims=True) + EPS)
    gs = vec[3:4] * (1.0 + vec[1:2])
    return x * r * gs + vec[0:1]


def _postnorm_residual(x, y, vec, res):
    r = lax.rsqrt(jnp.mean(y * y, axis=-1, keepdims=True) + EPS)
    return x + (res * vec[2:3]) * (y * r * vec[4:5])


def _ada_kernel(c_ref, w_ref, b_ref, o_ref):
    c = c_ref[...]
    s = c / (1.0 + jnp.exp(-c))
    o_ref[...] = _dot(s.astype(BF16), w_ref[...].astype(BF16)) + b_ref[...]


def _ada(c, w_ada, b_ada):
    b, d = c.shape
    n = w_ada.shape[1]
    rows = 16
    tn = n // 8
    c_pad = jnp.zeros((rows, d), F32).at[:b].set(c)
    out = pl.pallas_call(
        _ada_kernel,
        grid=(n // tn,),
        in_specs=[pl.BlockSpec((rows, d), lambda j: (0, 0)),
                  pl.BlockSpec((d, tn), lambda j: (0, j)),
                  pl.BlockSpec((1, tn), lambda j: (0, j))],
        out_specs=pl.BlockSpec((rows, tn), lambda j: (0, j)),
        out_shape=jax.ShapeDtypeStruct((rows, n), F32),
        compiler_params=pltpu.CompilerParams(dimension_semantics=("arbitrary",),
                                             vmem_limit_bytes=VMEM_LIMIT_BYTES),
        name="ada_proj",
    )(c_pad, w_ada, b_ada.reshape(1, n))
    return out[:b]


def _load_as_bf16(src_hbm, dst_ref, stage_ref, sem):
    slots, rows = stage_ref.shape[0], stage_ref.shape[1]
    n_chunks = dst_ref.shape[0] // rows

    def copy(c):
        return pltpu.make_async_copy(src_hbm.at[pl.ds(c * rows, rows)], stage_ref.at[c % slots],
                                     sem.at[c % slots])

    for c in range(min(slots - 1, n_chunks)):
        copy(c).start()
    for c in range(n_chunks):
        if c + slots - 1 < n_chunks:
            copy(c + slots - 1).start()
        copy(c).wait()
        dst_ref[pl.ds(c * rows, rows), :] = stage_ref[c % slots].astype(BF16)


def _ffn_kernel(*refs, n_side, load_weights, sub, chunks, d_ff, res):
    h_ref, vec_ref, w_in_src, w_out_src = refs[:4]
    side_in = refs[4:4 + n_side]
    o_ref = refs[4 + n_side]
    side_out = refs[5 + n_side:5 + 2 * n_side]
    scratch = refs[5 + 2 * n_side:]
    if load_weights:
        w_in_ref, w_out_ref, stage_in, stage_out, sem_in, sem_out = scratch

        @pl.when(pl.program_id(0) == 0)
        def _():
            _load_as_bf16(w_in_src, w_in_ref, stage_in, sem_in)
            _load_as_bf16(w_out_src, w_out_ref, stage_out, sem_out)
    else:
        w_in_ref, w_out_ref = w_in_src, w_out_src

    for src, dst in zip(side_in, side_out):
        dst[...] = src[...].astype(BF16)

    vec = vec_ref[0]
    starts = list(range(0, h_ref.shape[0], sub))

    def prenorm(r0):
        return _prenorm_mod(h_ref[r0:r0 + sub, :], vec).astype(BF16)

    def finish(r0, y):
        o_ref[r0:r0 + sub, :] = _postnorm_residual(h_ref[r0:r0 + sub, :], y, vec, res)

    n = prenorm(starts[0])
    pending = None
    for idx, r0 in enumerate(starts):
        y = None
        off = 0
        n_next = None
        for ci, width in enumerate(chunks):
            g = _dot(n, w_in_ref[:, off:off + width])
            u = _dot(n, w_in_ref[:, d_ff + off:d_ff + off + width])
            a = (g / (1.0 + jnp.exp(-g)) * u).astype(BF16)
            part = _dot(a, w_out_ref[off:off + width, :])
            y = part if y is None else y + part
            off += width
            if ci == 0 and pending is not None:
                finish(*pending)
                pending = None
            if ci == 1 and idx + 1 < len(starts):
                n_next = prenorm(starts[idx + 1])
        pending = (r0, y)
        n = n_next
    finish(*pending)


def _side_block_spec(shape, steps):
    rows = shape[0] // steps
    if shape[0] % steps or rows % BF16_SUBLANES:
        rows = STAGE_ROWS
    assert shape[0] % rows == 0 and shape[0] // rows <= steps, (shape, steps)
    last = shape[0] // rows - 1
    return pl.BlockSpec((rows, shape[1]), lambda i: (jnp.minimum(i, last), 0))


def _ffn(h, vec, w_in, w_out, seq_len, side=()):
    t, d = h.shape
    tm = min(FFN_TOKEN_TILE, seq_len)
    tiles_per_seq = seq_len // tm
    steps = t // tm
    d_ff = w_out.shape[0]
    load_weights = w_in.dtype != BF16
    kern = functools.partial(_ffn_kernel, n_side=len(side), load_weights=load_weights,
                             sub=min(FFN_SUB_TILE, tm), chunks=FFN_CHUNKS, d_ff=d_ff, res=FFN_RES)
    if load_weights:
        weight_specs = [pl.BlockSpec(memory_space=pl.ANY)] * 2
        scratch = [pltpu.VMEM(w_in.shape, BF16), pltpu.VMEM(w_out.shape, BF16),
                   pltpu.VMEM((STAGE_SLOTS, WIDE_STAGE_ROWS, w_in.shape[1]), F32),
                   pltpu.VMEM((STAGE_SLOTS, STAGE_ROWS, w_out.shape[1]), F32),
                   pltpu.SemaphoreType.DMA((STAGE_SLOTS,)), pltpu.SemaphoreType.DMA((STAGE_SLOTS,))]
    else:
        weight_specs = [_resident(w_in.shape), _resident(w_out.shape)]
        scratch = []
    side_specs = [_side_block_spec(a.shape, steps) for a in side]
    out = pl.pallas_call(
        kern,
        grid=(steps,),
        in_specs=[pl.BlockSpec((tm, d), lambda i: (i, 0)),
                  pl.BlockSpec((1, 8, d), lambda i: (i // tiles_per_seq, 0, 0))]
                 + weight_specs + side_specs,
        out_specs=[pl.BlockSpec((tm, d), lambda i: (i, 0))] + side_specs,
        out_shape=[jax.ShapeDtypeStruct((t, d), F32)]
                  + [jax.ShapeDtypeStruct(a.shape, BF16) for a in side],
        scratch_shapes=scratch,
        compiler_params=pltpu.CompilerParams(dimension_semantics=("arbitrary",),
                                             vmem_limit_bytes=VMEM_LIMIT_BYTES),
        name="ffn_half_step",
    )(h, vec, w_in, w_out, *side)
    return out[0], out[1:]


def _mixer_kernel(rel_ref, h_ref, vec_ref, bucket_ref, sink_ref, ln_ref, bsp_ref, ws_ref, w_in_ref,
                  wba_ref, wbg_ref, wo_ref, o_ref, carry_ref, attn_ref, sp_ref, bias_ref,
                  *, tm, sub, tiles_per_seq):
    first = (pl.program_id(0) % tiles_per_seq) == 0

    @pl.when(pl.program_id(0) == 0)
    def _():
        bucket = bucket_ref[...]
        hits = [bucket == bk for bk in range(N_BUCKETS)]
        for hd in range(N_Q_HEADS):
            acc = jnp.full(bucket.shape, NEG, F32)
            for bk in range(N_BUCKETS):
                acc = jnp.where(hits[bk], rel_ref[bk, hd], acc)
            kvh, g = divmod(hd, GROUP)
            bias_ref[2 * kvh + g % 2, (g // 2) * BLOCK:(g // 2 + 1) * BLOCK, :] = acc

    @pl.when(first)
    def _():
        carry_ref[...] = jnp.zeros_like(carry_ref)

    for r in range(0, tm, sub):
        rows = pl.ds(r, sub)
        _mixer_tile(first if r == 0 else None, h_ref.at[rows], vec_ref, sink_ref, ln_ref, bsp_ref,
                    ws_ref, w_in_ref, wba_ref, wbg_ref, wo_ref, o_ref.at[rows], carry_ref,
                    attn_ref.at[rows], sp_ref.at[rows], bias_ref, tm=sub)


def _mixer_tile(first, h_ref, vec_ref, sink_ref, ln_ref, bsp_ref, ws_ref, w_in_ref, wba_ref, wbg_ref,
                wo_ref, o_ref, carry_ref, attn_ref, sp_ref, bias_ref, *, tm):
    nb = tm // BLOCK
    x = h_ref[...]
    vec = vec_ref[0]
    n = _prenorm_mod(x, vec).astype(BF16)

    lo = lax.broadcasted_iota(jnp.int32, (1, LANES), 1) < HEAD_DIM

    q = (_dot(n, w_in_ref[:, 0:Q_W]) * (HEAD_DIM ** -0.5)).astype(BF16)
    w_kv = w_in_ref[:, Q_W:Q_W + 2 * KV_W]
    kv = jnp.concatenate([_dot(n[:tm // 2], w_kv), _dot(n[tm // 2:], w_kv)], axis=0)
    k_f, v_f = kv[:, :KV_W], kv[:, KV_W:]
    k_cur, k_rot = k_f.astype(BF16), pltpu.roll(k_f, HEAD_DIM, 1).astype(BF16)
    v_cur, v_rot = v_f.astype(BF16), pltpu.roll(v_f, HEAD_DIM, 1).astype(BF16)
    k_all = jnp.concatenate([carry_ref[0], k_cur], axis=0)
    k_all_rot = jnp.concatenate([carry_ref[1], k_rot], axis=0)
    v_all = jnp.concatenate([carry_ref[2], v_cur], axis=0)
    v_all_rot = jnp.concatenate([carry_ref[3], v_rot], axis=0)
    carry_ref[0] = k_cur[tm - BLOCK:]
    carry_ref[1] = k_rot[tm - BLOCK:]
    carry_ref[2] = v_cur[tm - BLOCK:]
    carry_ref[3] = v_rot[tm - BLOCK:]
    zero = jnp.zeros((), BF16)
    k_var = ((jnp.where(lo, k_all, zero), jnp.where(lo, zero, k_all_rot)),
             (jnp.where(lo, k_all_rot, zero), jnp.where(lo, zero, k_all)))
    v_var = ((jnp.where(lo, v_all, zero), jnp.where(lo, zero, v_all_rot)),
             (jnp.where(lo, v_all_rot, zero), jnp.where(lo, zero, v_all)))

    col = lax.broadcasted_iota(jnp.int32, (1, 2 * BLOCK), 1)
    no_prev = None if first is None else jnp.where(jnp.logical_and(first, col < BLOCK), NEG, 0.0)

    scores = []
    for b in range(nb):
        r0 = b * BLOCK
        keys = slice(r0, r0 + 2 * BLOCK)
        per_block = []
        for kvh in range(N_KV_HEADS):
            c0 = kvh * GROUP * HEAD_DIM
            q2 = jnp.concatenate([q[r0:r0 + BLOCK, c0:c0 + LANES],
                                  q[r0:r0 + BLOCK, c0 + LANES:c0 + 2 * LANES]], axis=0)
            for par in range(2):
                s = _dot_nt(q2, k_var[kvh][par][keys]) + bias_ref[2 * kvh + par]
                per_block.append(s + no_prev if (b == 0 and no_prev is not None) else s)
        scores.append(jnp.stack(per_block))
    sink = sink_ref[...]
    g0 = Q_W + 2 * KV_W + 2 * GMLP_WIDTH

    def softmax_block(b):
        s = scores[b]
        m = jnp.max(s, axis=-1, keepdims=True)
        return jnp.exp(s - m).astype(BF16), jnp.exp(sink - m)

    soft = [None] * nb
    zg = _dot(n, w_in_ref[:, Q_W + 2 * KV_W:g0])
    for b in range(0, nb // 2):
        soft[b] = softmax_block(b)
    ga = _dot(n, w_in_ref[:, g0:g0 + D_MODEL])
    for b in range(nb // 2, nb):
        soft[b] = softmax_block(b)
    gg = _dot(n, w_in_ref[:, g0 + D_MODEL:g0 + 2 * D_MODEL])

    rows2 = lax.broadcasted_iota(jnp.int32, (4 * BLOCK, LANES), 0) < 2 * BLOCK
    ones_cols = jnp.where(rows2 == lo, 1.0, 0.0).astype(BF16)
    for b in range(nb):
        r0 = b * BLOCK
        keys = slice(r0, r0 + 2 * BLOCK)
        probs, sink_term = soft[b]
        for kvh in range(N_KV_HEADS):
            c0 = kvh * GROUP * HEAD_DIM
            p2 = jnp.concatenate([probs[2 * kvh], probs[2 * kvh + 1]], axis=1)
            v2 = jnp.concatenate([v_var[kvh][0][keys], v_var[kvh][1][keys]], axis=0)
            o2 = _dot(p2, jnp.concatenate([v2, ones_cols], axis=1))
            denom = o2[:, LANES:] + jnp.where(lo, sink_term[2 * kvh], sink_term[2 * kvh + 1])
            o2 = o2[:, :LANES] / denom
            attn_ref[r0:r0 + BLOCK, c0:c0 + LANES] = o2[:BLOCK].astype(BF16)
            attn_ref[r0:r0 + BLOCK, c0 + LANES:c0 + 2 * LANES] = o2[BLOCK:].astype(BF16)

    zg = 0.5 * zg * (1.0 + jnp.tanh(math.sqrt(2.0 / math.pi) * (zg + 0.044715 * (zg * zg * zg))))
    u, vr = zg[:, :GMLP_WIDTH], zg[:, GMLP_WIDTH:]
    ln = ln_ref[...]
    mu = jnp.mean(vr, axis=-1, keepdims=True)
    vc = vr - mu
    var = jnp.mean(vc * vc, axis=-1, keepdims=True)
    vg = (vc * lax.rsqrt(var + EPS) * ln[0:1] + ln[1:2]).astype(BF16)
    row = lax.broadcasted_iota(jnp.int32, (BLOCK, 2 * BLOCK), 0)
    col2 = lax.broadcasted_iota(jnp.int32, (BLOCK, 2 * BLOCK), 1)
    causal = jnp.where(col2 < BLOCK, col2, col2 - BLOCK) <= row
    for j in range(GMLP_GROUPS // 2):
        w_pair = jnp.where(causal, ws_ref[j], zero)
        pieces = [vg[c * BLOCK:(c + 1) * BLOCK, j * LANES:(j + 1) * LANES] for c in range(nb)]
        rhs = jnp.concatenate(
            [jnp.concatenate([jnp.where(lo, pc, zero) for pc in pieces], axis=1),
             jnp.concatenate([jnp.where(lo, zero, pc) for pc in pieces], axis=1)], axis=0)
        out = _dot(w_pair, rhs)
        for c in range(nb):
            sp_ref[c * BLOCK:(c + 1) * BLOCK, j * LANES:(j + 1) * LANES] = (
                out[:, c * LANES:(c + 1) * LANES])
    bsp = bsp_ref[...]
    gm = jnp.concatenate(
        [u[c * BLOCK:(c + 1) * BLOCK] * (sp_ref[c * BLOCK:(c + 1) * BLOCK, :] + bsp)
         for c in range(nb)], axis=0).astype(BF16)

    ya = _dot(attn_ref[...], wba_ref[...])
    yg = _dot(gm, wbg_ref[...])
    mix = (ya / (1.0 + jnp.exp(-ga)) + yg / (1.0 + jnp.exp(-gg))).astype(BF16)
    y = _dot(mix, wo_ref[...])
    o_ref[...] = _postnorm_residual(x, y, vec, 1.0)


def _mixer(rel, h, vec, bucket, sink_t, ln, bsp, ws_t, w_in, wba, wbg, wo, seq_len):
    t, d = h.shape
    tm = min(MIX_TOKEN_TILE, seq_len)
    tiles_per_seq = seq_len // tm
    kern = functools.partial(_mixer_kernel, tm=tm, sub=min(MIX_SUB_TILE, tm),
                             tiles_per_seq=tiles_per_seq)
    return pl.pallas_call(
        kern,
        grid=(t // tm,),
        in_specs=[pl.BlockSpec(memory_space=pltpu.SMEM),
                  pl.BlockSpec((tm, d), lambda i: (i, 0)),
                  pl.BlockSpec((1, 8, d), lambda i: (i // tiles_per_seq, 0, 0)),
                  _resident(bucket.shape), _resident(sink_t.shape), _resident(ln.shape),
                  _resident(bsp.shape), _resident(ws_t.shape), _resident(w_in.shape),
                  _resident(wba.shape), _resident(wbg.shape), _resident(wo.shape)],
        out_specs=pl.BlockSpec((tm, d), lambda i: (i, 0)),
        out_shape=jax.ShapeDtypeStruct((t, d), F32),
        scratch_shapes=[pltpu.VMEM((4, BLOCK, KV_W), BF16),
                        pltpu.VMEM((tm, Q_W), BF16),
                        pltpu.VMEM((tm, GMLP_WIDTH), F32),
                        pltpu.VMEM((4, 2 * BLOCK, 2 * BLOCK), F32)],
        compiler_params=pltpu.CompilerParams(dimension_semantics=("arbitrary",),
                                             vmem_limit_bytes=VMEM_LIMIT_BYTES),
        name="token_mixers",
    )(rel, h, vec, bucket, sink_t, ln, bsp, ws_t, w_in, wba, wbg, wo)


def _t5_bucket_table():
    qi = np.arange(BLOCK, dtype=np.int32)[:, None]
    kj = np.arange(2 * BLOCK, dtype=np.int32)[None, :]
    dist = qi + BLOCK - kj
    in_window = (dist >= 0) & (dist < BLOCK)
    dc = np.maximum(dist, 0)
    max_exact = N_BUCKETS // 2
    d_f = np.maximum(dc, max_exact).astype(np.float32)
    large = max_exact + (np.log(d_f / np.float32(max_exact)).astype(np.float32)
                         / np.float32(math.log(MAX_DISTANCE / max_exact))
                         * np.float32(N_BUCKETS - max_exact)).astype(np.int32)
    large = np.minimum(large, N_BUCKETS - 1)
    return np.where(in_window, np.where(dc < max_exact, dc, large), -1).astype(np.int32)


def _pair_rows(a):
    out = []
    for kvh in range(N_KV_HEADS):
        for par in range(2):
            out.append(jnp.concatenate([a[GROUP * kvh + par], a[GROUP * kvh + 2 + par]], axis=0))
    return jnp.stack(out)


def kernel(x, c, rel_bias, w_ada, b_ada, pre_norm_g, post_norm_g, w_ffn1_in, w_ffn1_out, w_in,
           sinks, gmlp_ln_g, gmlp_ln_b, gmlp_w_s, gmlp_b_s, w_br_attn, w_br_gmlp, w_out,
           w_ffn2_in, w_ffn2_out):
    bsz, seq, d = x.shape
    depth = w_ada.shape[0]
    bucket = jnp.asarray(_t5_bucket_table())
    h = x.reshape(bsz * seq, d)
    for l in range(depth):
        ada = _ada(c, w_ada[l], b_ada[l]).reshape(bsz, N_ADA, d)

        def vec_for(sub):
            g = jnp.broadcast_to(jnp.stack([pre_norm_g[l, sub], post_norm_g[l, sub]])[None],
                                 (bsz, 2, d))
            return jnp.concatenate([ada[:, 3 * sub:3 * sub + 3], g,
                                    jnp.zeros((bsz, 3, d), F32)], axis=1)

        sink_t = _pair_rows(jnp.broadcast_to(sinks[l].astype(F32)[:, None, None],
                                             (N_Q_HEADS, BLOCK, LANES)))
        ln = jnp.stack([gmlp_ln_g[l], gmlp_ln_b[l]])
        bsp = jnp.repeat(jnp.transpose(gmlp_b_s[l]), GMLP_WIDTH // GMLP_GROUPS, axis=1)
        ws_t = jnp.concatenate([gmlp_w_s[l, 0::2], gmlp_w_s[l, 1::2]], axis=2).astype(BF16)

        h, later = _ffn(h, vec_for(0), w_ffn1_in[l], w_ffn1_out[l], seq,
                        side=(w_in[l], w_br_attn[l], w_br_gmlp[l], w_out[l],
                              w_ffn2_in[l], w_ffn2_out[l]))
        h = _mixer(rel_bias.astype(F32), h, vec_for(1), bucket, sink_t, ln, bsp, ws_t, *later[:4], seq)
        h, _ = _ffn(h, vec_for(2), later[4], later[5], seq)
    return h.reshape(bsz, seq, d)
```

```python
import functools
import math

import numpy as np
import jax
import jax.numpy as jnp
from jax import lax
from jax.experimental import pallas as pl
from jax.experimental.pallas import tpu as pltpu

D_MODEL = 1024
N_Q_HEADS = 8
N_KV_HEADS = 2
GROUP = N_Q_HEADS // N_KV_HEADS
HEAD_DIM = 64
BLOCK = 128
Q_W = N_Q_HEADS * HEAD_DIM
KV_W = N_KV_HEADS * HEAD_DIM
N_BUCKETS = 32
MAX_DISTANCE = 128
GMLP_GROUPS = 8
GMLP_WIDTH = 512
D_FF = 2816
FFN_RES = 0.5
N_ADA = 9
EPS = 1e-6
NEG = -1e30

LANES = 128
BF16_SUBLANES = 16
VMEM_LIMIT_BYTES = 56 * 1024 * 1024
FFN_TOKEN_TILE = 1024
FFN_SUB_TILE = 512
WIDE_STAGE_ROWS = 64
STAGE_ROWS = 256
STAGE_SLOTS = 3
MIX_TOKEN_TILE = 1024
MIX_SUB_TILE = 512
FFN_CHUNKS = (512, 512, 512, 512, 512, 256)

BF16 = jnp.bfloat16
F32 = jnp.float32


def _dot(a, b):
    return jnp.dot(a, b, preferred_element_type=F32)


def _dot_nt(a, b):
    return lax.dot_general(a, b, (((1,), (1,)), ((), ())), preferred_element_type=F32)


def _resident(shape):
    nd = len(shape)
    return pl.BlockSpec(shape, lambda i: (0,) * nd, pipeline_mode=pl.Buffered(1))


def _modulation(ada_ref, pre_g_ref, post_g_ref, sub_layer, batch):
    rows = tuple(ada_ref[3 * sub_layer + k, pl.ds(batch, 1), :] for k in range(3))
    return rows + (pre_g_ref[0, sub_layer:sub_layer + 1, :], post_g_ref[0, sub_layer:sub_layer + 1, :])


def _mod_specs(ada, pre_g, post_g, layer):
    per_layer = lambda i: (layer, 0, 0)
    return [_resident(ada.shape), pl.BlockSpec((1,) + pre_g.shape[1:], per_layer),
            pl.BlockSpec((1,) + post_g.shape[1:], per_layer)]


def _prenorm_mod(x, mod):
    r = lax.rsqrt(jnp.mean(x * x, axis=-1, keepdims=True) + EPS)
    gs = mod[3] * (1.0 + mod[1])
    return x * r * gs + mod[0]


def _postnorm_residual(x, y, mod, res):
    r = lax.rsqrt(jnp.mean(y * y, axis=-1, keepdims=True) + EPS)
    return x + (res * mod[2]) * (y * r * mod[4])


def _ada_kernel(c_ref, w_ref, b_ref, o_ref):
    c = c_ref[...]
    s = c / (1.0 + jnp.exp(-c))
    o_ref[0] = _dot(s.astype(BF16), w_ref[...].astype(BF16)) + b_ref[...]


def _ada(c, w_ada, b_ada):
    b, d = c.shape
    rows = BF16_SUBLANES
    c_pad = jnp.zeros((rows, d), F32).at[:b].set(c)
    return pl.pallas_call(
        _ada_kernel,
        grid=(N_ADA,),
        in_specs=[pl.BlockSpec((rows, d), lambda j: (0, 0)),
                  pl.BlockSpec((d, d), lambda j: (0, j)),
                  pl.BlockSpec((1, d), lambda j: (0, j))],
        out_specs=pl.BlockSpec((1, rows, d), lambda j: (j, 0, 0)),
        out_shape=jax.ShapeDtypeStruct((N_ADA, rows, d), F32),
        compiler_params=pltpu.CompilerParams(dimension_semantics=("arbitrary",),
                                             vmem_limit_bytes=VMEM_LIMIT_BYTES),
        name="ada_proj",
    )(c_pad, w_ada, b_ada.reshape(1, N_ADA * d))


def _load_as_bf16(src_hbm, dst_ref, stage_ref, sem):
    slots, rows = stage_ref.shape[0], stage_ref.shape[1]
    n_chunks = dst_ref.shape[0] // rows

    def copy(c):
        return pltpu.make_async_copy(src_hbm.at[pl.ds(c * rows, rows)], stage_ref.at[c % slots],
                                     sem.at[c % slots])

    for c in range(min(slots - 1, n_chunks)):
        copy(c).start()
    for c in range(n_chunks):
        if c + slots - 1 < n_chunks:
            copy(c + slots - 1).start()
        copy(c).wait()
        dst_ref[pl.ds(c * rows, rows), :] = stage_ref[c % slots].astype(BF16)


def _ffn_kernel(*refs, n_side, load_weights, sub_layer, tiles_per_seq, sub, chunks, d_ff, res):
    h_ref, ada_ref, pre_g_ref, post_g_ref, w_in_src, w_out_src = refs[:6]
    side_in = refs[6:6 + n_side]
    o_ref = refs[6 + n_side]
    side_out = refs[7 + n_side:7 + 2 * n_side]
    scratch = refs[7 + 2 * n_side:]
    if load_weights:
        w_in_ref, w_out_ref, stage_in, stage_out, sem_in, sem_out = scratch

        @pl.when(pl.program_id(0) == 0)
        def _():
            _load_as_bf16(w_in_src, w_in_ref, stage_in, sem_in)
            _load_as_bf16(w_out_src, w_out_ref, stage_out, sem_out)
    else:
        w_in_ref, w_out_ref = w_in_src, w_out_src

    for src, dst in zip(side_in, side_out):
        dst[...] = src[...].astype(BF16)

    vec = _modulation(ada_ref, pre_g_ref, post_g_ref, sub_layer, pl.program_id(0) // tiles_per_seq)
    starts = list(range(0, h_ref.shape[0], sub))

    def prenorm(r0):
        return _prenorm_mod(h_ref[r0:r0 + sub, :], vec).astype(BF16)

    def finish(r0, y):
        o_ref[r0:r0 + sub, :] = _postnorm_residual(h_ref[r0:r0 + sub, :], y, vec, res)

    n = prenorm(starts[0])
    pending = None
    for idx, r0 in enumerate(starts):
        y = None
        off = 0
        n_next = None
        for ci, width in enumerate(chunks):
            g = _dot(n, w_in_ref[:, off:off + width])
            u = _dot(n, w_in_ref[:, d_ff + off:d_ff + off + width])
            a = (g / (1.0 + jnp.exp(-g)) * u).astype(BF16)
            part = _dot(a, w_out_ref[off:off + width, :])
            y = part if y is None else y + part
            off += width
            if ci == 0 and pending is not None:
                finish(*pending)
                pending = None
            if ci == 1 and idx + 1 < len(starts):
                n_next = prenorm(starts[idx + 1])
        pending = (r0, y)
        n = n_next
    finish(*pending)


def _side_block_spec(shape, steps):
    rows = shape[0] // steps
    if shape[0] % steps or rows % BF16_SUBLANES:
        rows = STAGE_ROWS
    assert shape[0] % rows == 0 and shape[0] // rows <= steps, (shape, steps)
    last = shape[0] // rows - 1
    return pl.BlockSpec((rows, shape[1]), lambda i: (jnp.minimum(i, last), 0))


def _ffn(h, mod_arrays, layer, sub_layer, w_in, w_out, seq_len, side=()):
    t, d = h.shape
    tm = min(FFN_TOKEN_TILE, seq_len)
    tiles_per_seq = seq_len // tm
    steps = t // tm
    d_ff = w_out.shape[0]
    load_weights = w_in.dtype != BF16
    kern = functools.partial(_ffn_kernel, n_side=len(side), load_weights=load_weights,
                             sub_layer=sub_layer, tiles_per_seq=tiles_per_seq,
                             sub=min(FFN_SUB_TILE, tm), chunks=FFN_CHUNKS, d_ff=d_ff, res=FFN_RES)
    if load_weights:
        weight_specs = [pl.BlockSpec(memory_space=pl.ANY)] * 2
        scratch = [pltpu.VMEM(w_in.shape, BF16), pltpu.VMEM(w_out.shape, BF16),
                   pltpu.VMEM((STAGE_SLOTS, WIDE_STAGE_ROWS, w_in.shape[1]), F32),
                   pltpu.VMEM((STAGE_SLOTS, STAGE_ROWS, w_out.shape[1]), F32),
                   pltpu.SemaphoreType.DMA((STAGE_SLOTS,)), pltpu.SemaphoreType.DMA((STAGE_SLOTS,))]
    else:
        weight_specs = [_resident(w_in.shape), _resident(w_out.shape)]
        scratch = []
    side_specs = [_side_block_spec(a.shape, steps) for a in side]
    out = pl.pallas_call(
        kern,
        grid=(steps,),
        in_specs=[pl.BlockSpec((tm, d), lambda i: (i, 0))] + _mod_specs(*mod_arrays, layer)
                 + weight_specs + side_specs,
        out_specs=[pl.BlockSpec((tm, d), lambda i: (i, 0))] + side_specs,
        out_shape=[jax.ShapeDtypeStruct((t, d), F32)]
                  + [jax.ShapeDtypeStruct(a.shape, BF16) for a in side],
        scratch_shapes=scratch,
        compiler_params=pltpu.CompilerParams(dimension_semantics=("arbitrary",),
                                             vmem_limit_bytes=VMEM_LIMIT_BYTES),
        name="ffn_half_step",
    )(h, *mod_arrays, w_in, w_out, *side)
    return out[0], out[1:]


def _mixer_kernel(rel_ref, h_ref, ada_ref, pre_g_ref, post_g_ref, bucket_ref, sink_ref, ln_ref,
                  bsp_ref, ws_ref, w_in_ref, wba_ref, wbg_ref, wo_ref, o_ref, carry_ref, attn_ref,
                  sp_ref, bias_ref, *, tm, sub, tiles_per_seq):
    first = (pl.program_id(0) % tiles_per_seq) == 0
    mod = _modulation(ada_ref, pre_g_ref, post_g_ref, 1, pl.program_id(0) // tiles_per_seq)

    @pl.when(pl.program_id(0) == 0)
    def _():
        bucket = bucket_ref[...]
        hits = [bucket == bk for bk in range(N_BUCKETS)]
        for hd in range(N_Q_HEADS):
            acc = jnp.full(bucket.shape, NEG, F32)
            for bk in range(N_BUCKETS):
                acc = jnp.where(hits[bk], rel_ref[bk, hd], acc)
            kvh, g = divmod(hd, GROUP)
            bias_ref[2 * kvh + g % 2, (g // 2) * BLOCK:(g // 2 + 1) * BLOCK, :] = acc

    @pl.when(first)
    def _():
        carry_ref[...] = jnp.zeros_like(carry_ref)

    for r in range(0, tm, sub):
        rows = pl.ds(r, sub)
        _mixer_tile(first if r == 0 else None, mod, h_ref.at[rows], sink_ref, ln_ref, bsp_ref,
                    ws_ref, w_in_ref, wba_ref, wbg_ref, wo_ref, o_ref.at[rows], carry_ref,
                    attn_ref.at[rows], sp_ref.at[rows], bias_ref, tm=sub)


def _mixer_tile(first, vec, h_ref, sink_ref, ln_ref, bsp_ref, ws_ref, w_in_ref, wba_ref, wbg_ref,
                wo_ref, o_ref, carry_ref, attn_ref, sp_ref, bias_ref, *, tm):
    nb = tm // BLOCK
    x = h_ref[...]
    n = _prenorm_mod(x, vec).astype(BF16)

    lo = lax.broadcasted_iota(jnp.int32, (1, LANES), 1) < HEAD_DIM

    q = (_dot(n, w_in_ref[:, 0:Q_W]) * (HEAD_DIM ** -0.5)).astype(BF16)
    w_kv = w_in_ref[:, Q_W:Q_W + 2 * KV_W]
    kv = jnp.concatenate([_dot(n[:tm // 2], w_kv), _dot(n[tm // 2:], w_kv)], axis=0)
    k_f, v_f = kv[:, :KV_W], kv[:, KV_W:]
    k_cur, k_rot = k_f.astype(BF16), pltpu.roll(k_f, HEAD_DIM, 1).astype(BF16)
    v_cur, v_rot = v_f.astype(BF16), pltpu.roll(v_f, HEAD_DIM, 1).astype(BF16)
    k_all = jnp.concatenate([carry_ref[0], k_cur], axis=0)
    k_all_rot = jnp.concatenate([carry_ref[1], k_rot], axis=0)
    v_all = jnp.concatenate([carry_ref[2], v_cur], axis=0)
    v_all_rot = jnp.concatenate([carry_ref[3], v_rot], axis=0)
    carry_ref[0] = k_cur[tm - BLOCK:]
    carry_ref[1] = k_rot[tm - BLOCK:]
    carry_ref[2] = v_cur[tm - BLOCK:]
    carry_ref[3] = v_rot[tm - BLOCK:]
    zero = jnp.zeros((), BF16)
    k_var = ((jnp.where(lo, k_all, zero), jnp.where(lo, zero, k_all_rot)),
             (jnp.where(lo, k_all_rot, zero), jnp.where(lo, zero, k_all)))
    v_var = ((jnp.where(lo, v_all, zero), jnp.where(lo, zero, v_all_rot)),
             (jnp.where(lo, v_all_rot, zero), jnp.where(lo, zero, v_all)))

    col = lax.broadcasted_iota(jnp.int32, (1, 2 * BLOCK), 1)
    no_prev = None if first is None else jnp.where(jnp.logical_and(first, col < BLOCK), NEG, 0.0)

    scores = []
    for b in range(nb):
        r0 = b * BLOCK
        keys = slice(r0, r0 + 2 * BLOCK)
        per_block = []
        for kvh in range(N_KV_HEADS):
            c0 = kvh * GROUP * HEAD_DIM
            q2 = jnp.concatenate([q[r0:r0 + BLOCK, c0:c0 + LANES],
                                  q[r0:r0 + BLOCK, c0 + LANES:c0 + 2 * LANES]], axis=0)
            for par in range(2):
                s = _dot_nt(q2, k_var[kvh][par][keys]) + bias_ref[2 * kvh + par]
                per_block.append(s + no_prev if (b == 0 and no_prev is not None) else s)
        scores.append(jnp.stack(per_block))
    sink = sink_ref[...]
    g0 = Q_W + 2 * KV_W + 2 * GMLP_WIDTH

    def softmax_block(b):
        s = scores[b]
        m = jnp.max(s, axis=-1, keepdims=True)
        return jnp.exp(s - m).astype(BF16), jnp.exp(sink - m)

    soft = [None] * nb
    zg = _dot(n, w_in_ref[:, Q_W + 2 * KV_W:g0])
    for b in range(0, nb // 2):
        soft[b] = softmax_block(b)
    ga = _dot(n, w_in_ref[:, g0:g0 + D_MODEL])
    for b in range(nb // 2, nb):
        soft[b] = softmax_block(b)
    gg = _dot(n, w_in_ref[:, g0 + D_MODEL:g0 + 2 * D_MODEL])

    rows2 = lax.broadcasted_iota(jnp.int32, (4 * BLOCK, LANES), 0) < 2 * BLOCK
    ones_cols = jnp.where(rows2 == lo, 1.0, 0.0).astype(BF16)
    for b in range(nb):
        r0 = b * BLOCK
        keys = slice(r0, r0 + 2 * BLOCK)
        probs, sink_term = soft[b]
        for kvh in range(N_KV_HEADS):
            c0 = kvh * GROUP * HEAD_DIM
            p2 = jnp.concatenate([probs[2 * kvh], probs[2 * kvh + 1]], axis=1)
            v2 = jnp.concatenate([v_var[kvh][0][keys], v_var[kvh][1][keys]], axis=0)
            o2 = _dot(p2, jnp.concatenate([v2, ones_cols], axis=1))
            denom = o2[:, LANES:] + jnp.where(lo, sink_term[2 * kvh], sink_term[2 * kvh + 1])
            o2 = o2[:, :LANES] / denom
            attn_ref[r0:r0 + BLOCK, c0:c0 + LANES] = o2[:BLOCK].astype(BF16)
            attn_ref[r0:r0 + BLOCK, c0 + LANES:c0 + 2 * LANES] = o2[BLOCK:].astype(BF16)

    zg = 0.5 * zg * (1.0 + jnp.tanh(math.sqrt(2.0 / math.pi) * (zg + 0.044715 * (zg * zg * zg))))
    u, vr = zg[:, :GMLP_WIDTH], zg[:, GMLP_WIDTH:]
    ln = ln_ref[...]
    mu = jnp.mean(vr, axis=-1, keepdims=True)
    vc = vr - mu
    var = jnp.mean(vc * vc, axis=-1, keepdims=True)
    vg = (vc * lax.rsqrt(var + EPS) * ln[0:1] + ln[1:2]).astype(BF16)
    row = lax.broadcasted_iota(jnp.int32, (BLOCK, 2 * BLOCK), 0)
    col2 = lax.broadcasted_iota(jnp.int32, (BLOCK, 2 * BLOCK), 1)
    causal = jnp.where(col2 < BLOCK, col2, col2 - BLOCK) <= row
    for j in range(GMLP_GROUPS // 2):
        w_pair = jnp.where(causal, ws_ref[j], zero)
        pieces = [vg[c * BLOCK:(c + 1) * BLOCK, j * LANES:(j + 1) * LANES] for c in range(nb)]
        rhs = jnp.concatenate(
            [jnp.concatenate([jnp.where(lo, pc, zero) for pc in pieces], axis=1),
             jnp.concatenate([jnp.where(lo, zero, pc) for pc in pieces], axis=1)], axis=0)
        out = _dot(w_pair, rhs)
        for c in range(nb):
            sp_ref[c * BLOCK:(c + 1) * BLOCK, j * LANES:(j + 1) * LANES] = (
                out[:, c * LANES:(c + 1) * LANES])
    bsp = bsp_ref[...]
    gm = jnp.concatenate(
        [u[c * BLOCK:(c + 1) * BLOCK] * (sp_ref[c * BLOCK:(c + 1) * BLOCK, :] + bsp)
         for c in range(nb)], axis=0).astype(BF16)

    ya = _dot(attn_ref[...], wba_ref[...])
    yg = _dot(gm, wbg_ref[...])
    mix = (ya / (1.0 + jnp.exp(-ga)) + yg / (1.0 + jnp.exp(-gg))).astype(BF16)
    y = _dot(mix, wo_ref[...])
    o_ref[...] = _postnorm_residual(x, y, vec, 1.0)


def _mixer(rel, h, mod_arrays, layer, bucket, sink_t, ln, bsp, ws_t, w_in, wba, wbg, wo, seq_len):
    t, d = h.shape
    tm = min(MIX_TOKEN_TILE, seq_len)
    tiles_per_seq = seq_len // tm
    kern = functools.partial(_mixer_kernel, tm=tm, sub=min(MIX_SUB_TILE, tm),
                             tiles_per_seq=tiles_per_seq)
    return pl.pallas_call(
        kern,
        grid=(t // tm,),
        in_specs=[pl.BlockSpec(memory_space=pltpu.SMEM),
                  pl.BlockSpec((tm, d), lambda i: (i, 0))] + _mod_specs(*mod_arrays, layer) + [
                  _resident(bucket.shape), _resident(sink_t.shape), _resident(ln.shape),
                  _resident(bsp.shape), _resident(ws_t.shape), _resident(w_in.shape),
                  _resident(wba.shape), _resident(wbg.shape), _resident(wo.shape)],
        out_specs=pl.BlockSpec((tm, d), lambda i: (i, 0)),
        out_shape=jax.ShapeDtypeStruct((t, d), F32),
        scratch_shapes=[pltpu.VMEM((4, BLOCK, KV_W), BF16),
                        pltpu.VMEM((tm, Q_W), BF16),
                        pltpu.VMEM((tm, GMLP_WIDTH), F32),
                        pltpu.VMEM((4, 2 * BLOCK, 2 * BLOCK), F32)],
        compiler_params=pltpu.CompilerParams(dimension_semantics=("arbitrary",),
                                             vmem_limit_bytes=VMEM_LIMIT_BYTES),
        name="token_mixers",
    )(rel, h, *mod_arrays, bucket, sink_t, ln, bsp, ws_t, w_in, wba, wbg, wo)


def _t5_bucket_table():
    qi = np.arange(BLOCK, dtype=np.int32)[:, None]
    kj = np.arange(2 * BLOCK, dtype=np.int32)[None, :]
    dist = qi + BLOCK - kj
    in_window = (dist >= 0) & (dist < BLOCK)
    dc = np.maximum(dist, 0)
    max_exact = N_BUCKETS // 2
    d_f = np.maximum(dc, max_exact).astype(np.float32)
    large = max_exact + (np.log(d_f / np.float32(max_exact)).astype(np.float32)
                         / np.float32(math.log(MAX_DISTANCE / max_exact))
                         * np.float32(N_BUCKETS - max_exact)).astype(np.int32)
    large = np.minimum(large, N_BUCKETS - 1)
    return np.where(in_window, np.where(dc < max_exact, dc, large), -1).astype(np.int32)


def _pair_rows(a):
    out = []
    for kvh in range(N_KV_HEADS):
        for par in range(2):
            out.append(jnp.concatenate([a[GROUP * kvh + par], a[GROUP * kvh + 2 + par]], axis=0))
    return jnp.stack(out)


def kernel(x, c, rel_bias, w_ada, b_ada, pre_norm_g, post_norm_g, w_ffn1_in, w_ffn1_out, w_in,
           sinks, gmlp_ln_g, gmlp_ln_b, gmlp_w_s, gmlp_b_s, w_br_attn, w_br_gmlp, w_out,
           w_ffn2_in, w_ffn2_out):
    bsz, seq, d = x.shape
    depth = w_ada.shape[0]
    bucket = jnp.asarray(_t5_bucket_table())
    h = x.reshape(bsz * seq, d)
    for l in range(depth):
        mods = (_ada(c, w_ada[l], b_ada[l]), pre_norm_g, post_norm_g)

        sink_t = _pair_rows(jnp.broadcast_to(sinks[l].astype(F32)[:, None, None],
                                             (N_Q_HEADS, BLOCK, LANES)))
        ln = jnp.stack([gmlp_ln_g[l], gmlp_ln_b[l]])
        bsp = jnp.repeat(jnp.transpose(gmlp_b_s[l]), GMLP_WIDTH // GMLP_GROUPS, axis=1)
        ws_t = jnp.concatenate([gmlp_w_s[l, 0::2], gmlp_w_s[l, 1::2]], axis=2).astype(BF16)

        h, later = _ffn(h, mods, l, 0, w_ffn1_in[l], w_ffn1_out[l], seq,
                        side=(w_in[l], w_br_attn[l], w_br_gmlp[l], w_out[l],
                              w_ffn2_in[l], w_ffn2_out[l]))
        h = _mixer(rel_bias.astype(F32), h, mods, l, bucket, sink_t, ln, bsp, ws_t, *later[:4], seq)
        h, _ = _ffn(h, mods, l, 2, later[4], later[5], seq)
    return h.reshape(bsz, seq, d)
```

```python
import functools
import math

import numpy as np
import jax
import jax.numpy as jnp
from jax import lax
from jax.experimental import pallas as pl
from jax.experimental.pallas import tpu as pltpu

D_MODEL = 1024
N_Q_HEADS = 8
N_KV_HEADS = 2
GROUP = N_Q_HEADS // N_KV_HEADS
HEAD_DIM = 64
BLOCK = 128
Q_W = N_Q_HEADS * HEAD_DIM
KV_W = N_KV_HEADS * HEAD_DIM
N_BUCKETS = 32
MAX_DISTANCE = 128
GMLP_GROUPS = 8
GMLP_WIDTH = 512
D_FF = 2816
FFN_RES = 0.5
N_ADA = 9
EPS = 1e-6
NEG = -1e30

LANES = 128
BF16_SUBLANES = 16
VMEM_LIMIT_BYTES = 56 * 1024 * 1024
FFN_TOKEN_TILE = 512
FFN_PIECES = 4
WIDE_STAGE_ROWS = 64
STAGE_ROWS = 256
STAGE_SLOTS = 3
MIX_TOKEN_TILE = 1024
MIX_SUB_TILE = 512
FFN_CHUNKS = (512, 512, 512, 512, 512, 256)

BF16 = jnp.bfloat16
F32 = jnp.float32


def _dot(a, b):
    return jnp.dot(a, b, preferred_element_type=F32)


def _dot_nt(a, b):
    return lax.dot_general(a, b, (((1,), (1,)), ((), ())), preferred_element_type=F32)


def _resident(shape):
    nd = len(shape)
    return pl.BlockSpec(shape, lambda i: (0,) * nd, pipeline_mode=pl.Buffered(1))


def _modulation(ada_ref, pre_g_ref, post_g_ref, sub_layer, batch):
    rows = tuple(ada_ref[3 * sub_layer + k, pl.ds(batch, 1), :] for k in range(3))
    return rows + (pre_g_ref[0, sub_layer:sub_layer + 1, :], post_g_ref[0, sub_layer:sub_layer + 1, :])


def _mod_specs(ada, pre_g, post_g, layer):
    per_layer = lambda i: (layer, 0, 0)
    return [_resident(ada.shape), pl.BlockSpec((1,) + pre_g.shape[1:], per_layer),
            pl.BlockSpec((1,) + post_g.shape[1:], per_layer)]


def _prenorm_mod(x, mod):
    r = lax.rsqrt(jnp.mean(x * x, axis=-1, keepdims=True) + EPS)
    gs = mod[3] * (1.0 + mod[1])
    return x * r * gs + mod[0]


def _postnorm_residual(x, y, mod, res):
    r = lax.rsqrt(jnp.mean(y * y, axis=-1, keepdims=True) + EPS)
    return x + (res * mod[2]) * (y * r * mod[4])


def _ada_kernel(c_ref, w_ref, b_ref, o_ref):
    c = c_ref[...]
    s = c / (1.0 + jnp.exp(-c))
    o_ref[0] = _dot(s.astype(BF16), w_ref[...].astype(BF16)) + b_ref[...]


def _ada(c, w_ada, b_ada):
    b, d = c.shape
    rows = BF16_SUBLANES
    c_pad = jnp.zeros((rows, d), F32).at[:b].set(c)
    return pl.pallas_call(
        _ada_kernel,
        grid=(N_ADA,),
        in_specs=[pl.BlockSpec((rows, d), lambda j: (0, 0)),
                  pl.BlockSpec((d, d), lambda j: (0, j)),
                  pl.BlockSpec((1, d), lambda j: (0, j))],
        out_specs=pl.BlockSpec((1, rows, d), lambda j: (j, 0, 0)),
        out_shape=jax.ShapeDtypeStruct((N_ADA, rows, d), F32),
        compiler_params=pltpu.CompilerParams(dimension_semantics=("arbitrary",),
                                             vmem_limit_bytes=VMEM_LIMIT_BYTES),
        name="ada_proj",
    )(c_pad, w_ada, b_ada.reshape(1, N_ADA * d))


def _load_as_bf16(src_hbm, dst_ref, stage_ref, sem):
    slots, rows = stage_ref.shape[0], stage_ref.shape[1]
    n_chunks = dst_ref.shape[0] // rows

    def copy(c):
        return pltpu.make_async_copy(src_hbm.at[pl.ds(c * rows, rows)], stage_ref.at[c % slots],
                                     sem.at[c % slots])

    for c in range(min(slots - 1, n_chunks)):
        copy(c).start()
    for c in range(n_chunks):
        if c + slots - 1 < n_chunks:
            copy(c + slots - 1).start()
        copy(c).wait()
        dst_ref[pl.ds(c * rows, rows), :] = stage_ref[c % slots].astype(BF16)


def _ffn_kernel(*refs, n_side, load_weights, sub_layer, tiles_per_seq, n_tiles, pieces, chunks,
                d_ff, res):
    h_next_ref, h_prev_ref, ada_ref, pre_g_ref, post_g_ref, w_in_src, w_out_src = refs[:7]
    side_in = refs[7:7 + n_side]
    o_ref = refs[7 + n_side]
    side_out = refs[8 + n_side:8 + 2 * n_side]
    n_ref, y_ref = refs[8 + 2 * n_side:10 + 2 * n_side]
    step = pl.program_id(0)
    tm = y_ref.shape[0]

    def modulation(tile):
        return _modulation(ada_ref, pre_g_ref, post_g_ref, sub_layer, tile // tiles_per_seq)

    if load_weights:
        w_in_ref, w_out_ref, stage_in, stage_out, sem_in, sem_out = refs[10 + 2 * n_side:]
    else:
        w_in_ref, w_out_ref = w_in_src, w_out_src

    @pl.when(step == 0)
    def _():
        if load_weights:
            _load_as_bf16(w_in_src, w_in_ref, stage_in, sem_in)
            _load_as_bf16(w_out_src, w_out_ref, stage_out, sem_out)
        n_ref[0] = _prenorm_mod(h_prev_ref[...], modulation(0)).astype(BF16)
        y_ref[...] = jnp.zeros_like(y_ref)

    for src, dst in zip(side_in, side_out):
        dst[...] = src[...].astype(BF16)

    mod_prev = modulation(jnp.maximum(step - 1, 0))

    def finish_rows(rows):
        o_ref[rows, :] = _postnorm_residual(h_prev_ref[rows, :], y_ref[rows, :], mod_prev, res)

    @pl.when(step < n_tiles)
    def _():
        slot = step % 2
        mod_next = modulation(jnp.minimum(step + 1, n_tiles - 1))
        n = n_ref[slot]
        rows_of = [pl.ds(p * (tm // pieces), tm // pieces) for p in range(pieces)]
        jobs = [functools.partial(finish_rows, r) for r in rows_of]
        for r in rows_of:
            def prepare(r=r):
                n_ref[1 - slot, r, :] = _prenorm_mod(h_next_ref[r, :], mod_next).astype(BF16)
            jobs.append(prepare)
        per_chunk = -(-len(jobs) // (len(chunks) - 1))
        y = None
        off = 0
        for ci, width in enumerate(chunks):
            g = _dot(n, w_in_ref[:, off:off + width])
            u = _dot(n, w_in_ref[:, d_ff + off:d_ff + off + width])
            a = (g / (1.0 + jnp.exp(-g)) * u).astype(BF16)
            part = _dot(a, w_out_ref[off:off + width, :])
            y = part if y is None else y + part
            off += width
            for job in jobs[ci * per_chunk:(ci + 1) * per_chunk]:
                job()
        y_ref[...] = y

    @pl.when(step == n_tiles)
    def _():
        finish_rows(slice(None))


def _side_block_spec(shape, steps):
    rows = shape[0] // steps
    if shape[0] % steps or rows % BF16_SUBLANES:
        rows = STAGE_ROWS
    assert shape[0] % rows == 0 and shape[0] // rows <= steps, (shape, steps)
    last = shape[0] // rows - 1
    return pl.BlockSpec((rows, shape[1]), lambda i: (jnp.minimum(i, last), 0))


def _ffn(h, mod_arrays, layer, sub_layer, w_in, w_out, seq_len, side=()):
    t, d = h.shape
    tm = min(FFN_TOKEN_TILE, seq_len)
    tiles_per_seq = seq_len // tm
    n_tiles = t // tm
    d_ff = w_out.shape[0]
    load_weights = w_in.dtype != BF16
    kern = functools.partial(_ffn_kernel, n_side=len(side), load_weights=load_weights,
                             sub_layer=sub_layer, tiles_per_seq=tiles_per_seq, n_tiles=n_tiles,
                             pieces=FFN_PIECES, chunks=FFN_CHUNKS, d_ff=d_ff, res=FFN_RES)
    scratch = [pltpu.VMEM((2, tm, d), BF16), pltpu.VMEM((tm, d), F32)]
    if load_weights:
        weight_specs = [pl.BlockSpec(memory_space=pl.ANY)] * 2
        scratch += [pltpu.VMEM(w_in.shape, BF16), pltpu.VMEM(w_out.shape, BF16),
                    pltpu.VMEM((STAGE_SLOTS, WIDE_STAGE_ROWS, w_in.shape[1]), F32),
                    pltpu.VMEM((STAGE_SLOTS, STAGE_ROWS, w_out.shape[1]), F32),
                    pltpu.SemaphoreType.DMA((STAGE_SLOTS,)), pltpu.SemaphoreType.DMA((STAGE_SLOTS,))]
    else:
        weight_specs = [_resident(w_in.shape), _resident(w_out.shape)]
    side_specs = [_side_block_spec(a.shape, n_tiles) for a in side]
    prev_tile = lambda i: (jnp.maximum(i - 1, 0), 0)
    out = pl.pallas_call(
        kern,
        grid=(n_tiles + 1,),
        in_specs=[pl.BlockSpec((tm, d), lambda i: (jnp.minimum(i + 1, n_tiles - 1), 0)),
                  pl.BlockSpec((tm, d), prev_tile)] + _mod_specs(*mod_arrays, layer)
                 + weight_specs + side_specs,
        out_specs=[pl.BlockSpec((tm, d), prev_tile)] + side_specs,
        out_shape=[jax.ShapeDtypeStruct((t, d), F32)]
                  + [jax.ShapeDtypeStruct(a.shape, BF16) for a in side],
        scratch_shapes=scratch,
        compiler_params=pltpu.CompilerParams(dimension_semantics=("arbitrary",),
                                             vmem_limit_bytes=VMEM_LIMIT_BYTES),
        name="ffn_half_step",
    )(h, h, *mod_arrays, w_in, w_out, *side)
    return out[0], out[1:]


def _mixer_kernel(rel_ref, h_ref, ada_ref, pre_g_ref, post_g_ref, bucket_ref, sink_ref, ln_ref,
                  bsp_ref, ws_ref, w_in_ref, wba_ref, wbg_ref, wo_ref, o_ref, carry_ref, attn_ref,
                  sp_ref, bias_ref, *, tm, sub, tiles_per_seq):
    first = (pl.program_id(0) % tiles_per_seq) == 0
    mod = _modulation(ada_ref, pre_g_ref, post_g_ref, 1, pl.program_id(0) // tiles_per_seq)

    @pl.when(pl.program_id(0) == 0)
    def _():
        bucket = bucket_ref[...]
        hits = [bucket == bk for bk in range(N_BUCKETS)]
        for hd in range(N_Q_HEADS):
            acc = jnp.full(bucket.shape, NEG, F32)
            for bk in range(N_BUCKETS):
                acc = jnp.where(hits[bk], rel_ref[bk, hd], acc)
            kvh, g = divmod(hd, GROUP)
            bias_ref[2 * kvh + g % 2, (g // 2) * BLOCK:(g // 2 + 1) * BLOCK, :] = acc

    @pl.when(first)
    def _():
        carry_ref[...] = jnp.zeros_like(carry_ref)

    for r in range(0, tm, sub):
        rows = pl.ds(r, sub)
        _mixer_tile(first if r == 0 else None, mod, h_ref.at[rows], sink_ref, ln_ref, bsp_ref,
                    ws_ref, w_in_ref, wba_ref, wbg_ref, wo_ref, o_ref.at[rows], carry_ref,
                    attn_ref.at[rows], sp_ref.at[rows], bias_ref, tm=sub)


def _mixer_tile(first, vec, h_ref, sink_ref, ln_ref, bsp_ref, ws_ref, w_in_ref, wba_ref, wbg_ref,
                wo_ref, o_ref, carry_ref, attn_ref, sp_ref, bias_ref, *, tm):
    nb = tm // BLOCK
    x = h_ref[...]
    n = _prenorm_mod(x, vec).astype(BF16)

    lo = lax.broadcasted_iota(jnp.int32, (1, LANES), 1) < HEAD_DIM

    q = (_dot(n, w_in_ref[:, 0:Q_W]) * (HEAD_DIM ** -0.5)).astype(BF16)
    w_kv = w_in_ref[:, Q_W:Q_W + 2 * KV_W]
    kv = jnp.concatenate([_dot(n[:tm // 2], w_kv), _dot(n[tm // 2:], w_kv)], axis=0)
    k_f, v_f = kv[:, :KV_W], kv[:, KV_W:]
    k_cur, k_rot = k_f.astype(BF16), pltpu.roll(k_f, HEAD_DIM, 1).astype(BF16)
    v_cur, v_rot = v_f.astype(BF16), pltpu.roll(v_f, HEAD_DIM, 1).astype(BF16)
    k_all = jnp.concatenate([carry_ref[0], k_cur], axis=0)
    k_all_rot = jnp.concatenate([carry_ref[1], k_rot], axis=0)
    v_all = jnp.concatenate([carry_ref[2], v_cur], axis=0)
    v_all_rot = jnp.concatenate([carry_ref[3], v_rot], axis=0)
    carry_ref[0] = k_cur[tm - BLOCK:]
    carry_ref[1] = k_rot[tm - BLOCK:]
    carry_ref[2] = v_cur[tm - BLOCK:]
    carry_ref[3] = v_rot[tm - BLOCK:]
    zero = jnp.zeros((), BF16)
    k_var = ((jnp.where(lo, k_all, zero), jnp.where(lo, zero, k_all_rot)),
             (jnp.where(lo, k_all_rot, zero), jnp.where(lo, zero, k_all)))
    v_var = ((jnp.where(lo, v_all, zero), jnp.where(lo, zero, v_all_rot)),
             (jnp.where(lo, v_all_rot, zero), jnp.where(lo, zero, v_all)))

    col = lax.broadcasted_iota(jnp.int32, (1, 2 * BLOCK), 1)
    no_prev = None if first is None else jnp.where(jnp.logical_and(first, col < BLOCK), NEG, 0.0)

    scores = []
    for b in range(nb):
        r0 = b * BLOCK
        keys = slice(r0, r0 + 2 * BLOCK)
        per_block = []
        for kvh in range(N_KV_HEADS):
            c0 = kvh * GROUP * HEAD_DIM
            q2 = jnp.concatenate([q[r0:r0 + BLOCK, c0:c0 + LANES],
                                  q[r0:r0 + BLOCK, c0 + LANES:c0 + 2 * LANES]], axis=0)
            for par in range(2):
                s = _dot_nt(q2, k_var[kvh][par][keys]) + bias_ref[2 * kvh + par]
                per_block.append(s + no_prev if (b == 0 and no_prev is not None) else s)
        scores.append(jnp.stack(per_block))
    sink = sink_ref[...]
    g0 = Q_W + 2 * KV_W + 2 * GMLP_WIDTH

    def softmax_block(b):
        s = scores[b]
        m = jnp.max(s, axis=-1, keepdims=True)
        return jnp.exp(s - m).astype(BF16), jnp.exp(sink - m)

    soft = [None] * nb
    zg = _dot(n, w_in_ref[:, Q_W + 2 * KV_W:g0])
    for b in range(0, nb // 2):
        soft[b] = softmax_block(b)
    ga = _dot(n, w_in_ref[:, g0:g0 + D_MODEL])
    for b in range(nb // 2, nb):
        soft[b] = softmax_block(b)
    gg = _dot(n, w_in_ref[:, g0 + D_MODEL:g0 + 2 * D_MODEL])

    rows2 = lax.broadcasted_iota(jnp.int32, (4 * BLOCK, LANES), 0) < 2 * BLOCK
    ones_cols = jnp.where(rows2 == lo, 1.0, 0.0).astype(BF16)
    for b in range(nb):
        r0 = b * BLOCK
        keys = slice(r0, r0 + 2 * BLOCK)
        probs, sink_term = soft[b]
        for kvh in range(N_KV_HEADS):
            c0 = kvh * GROUP * HEAD_DIM
            p2 = jnp.concatenate([probs[2 * kvh], probs[2 * kvh + 1]], axis=1)
            v2 = jnp.concatenate([v_var[kvh][0][keys], v_var[kvh][1][keys]], axis=0)
            o2 = _dot(p2, jnp.concatenate([v2, ones_cols], axis=1))
            denom = o2[:, LANES:] + jnp.where(lo, sink_term[2 * kvh], sink_term[2 * kvh + 1])
            o2 = o2[:, :LANES] / denom
            attn_ref[r0:r0 + BLOCK, c0:c0 + LANES] = o2[:BLOCK].astype(BF16)
            attn_ref[r0:r0 + BLOCK, c0 + LANES:c0 + 2 * LANES] = o2[BLOCK:].astype(BF16)

    zg = 0.5 * zg * (1.0 + jnp.tanh(math.sqrt(2.0 / math.pi) * (zg + 0.044715 * (zg * zg * zg))))
    u, vr = zg[:, :GMLP_WIDTH], zg[:, GMLP_WIDTH:]
    ln = ln_ref[...]
    mu = jnp.mean(vr, axis=-1, keepdims=True)
    vc = vr - mu
    var = jnp.mean(vc * vc, axis=-1, keepdims=True)
    vg = (vc * lax.rsqrt(var + EPS) * ln[0:1] + ln[1:2]).astype(BF16)
    row = lax.broadcasted_iota(jnp.int32, (BLOCK, 2 * BLOCK), 0)
    col2 = lax.broadcasted_iota(jnp.int32, (BLOCK, 2 * BLOCK), 1)
    causal = jnp.where(col2 < BLOCK, col2, col2 - BLOCK) <= row
    for j in range(GMLP_GROUPS // 2):
        w_pair = jnp.where(causal, ws_ref[j], zero)
        pieces = [vg[c * BLOCK:(c + 1) * BLOCK, j * LANES:(j + 1) * LANES] for c in range(nb)]
        rhs = jnp.concatenate(
            [jnp.concatenate([jnp.where(lo, pc, zero) for pc in pieces], axis=1),
             jnp.concatenate([jnp.where(lo, zero, pc) for pc in pieces], axis=1)], axis=0)
        out = _dot(w_pair, rhs)
        for c in range(nb):
            sp_ref[c * BLOCK:(c + 1) * BLOCK, j * LANES:(j + 1) * LANES] = (
                out[:, c * LANES:(c + 1) * LANES])
    bsp = bsp_ref[...]
    gm = jnp.concatenate(
        [u[c * BLOCK:(c + 1) * BLOCK] * (sp_ref[c * BLOCK:(c + 1) * BLOCK, :] + bsp)
         for c in range(nb)], axis=0).astype(BF16)

    ya = _dot(attn_ref[...], wba_ref[...])
    yg = _dot(gm, wbg_ref[...])
    mix = (ya / (1.0 + jnp.exp(-ga)) + yg / (1.0 + jnp.exp(-gg))).astype(BF16)
    y = _dot(mix, wo_ref[...])
    o_ref[...] = _postnorm_residual(x, y, vec, 1.0)


def _mixer(rel, h, mod_arrays, layer, bucket, sink_t, ln, bsp, ws_t, w_in, wba, wbg, wo, seq_len):
    t, d = h.shape
    tm = min(MIX_TOKEN_TILE, seq_len)
    tiles_per_seq = seq_len // tm
    kern = functools.partial(_mixer_kernel, tm=tm, sub=min(MIX_SUB_TILE, tm),
                             tiles_per_seq=tiles_per_seq)
    return pl.pallas_call(
        kern,
        grid=(t // tm,),
        in_specs=[pl.BlockSpec(memory_space=pltpu.SMEM),
                  pl.BlockSpec((tm, d), lambda i: (i, 0))] + _mod_specs(*mod_arrays, layer) + [
                  _resident(bucket.shape), _resident(sink_t.shape), _resident(ln.shape),
                  _resident(bsp.shape), _resident(ws_t.shape), _resident(w_in.shape),
                  _resident(wba.shape), _resident(wbg.shape), _resident(wo.shape)],
        out_specs=pl.BlockSpec((tm, d), lambda i: (i, 0)),
        out_shape=jax.ShapeDtypeStruct((t, d), F32),
        scratch_shapes=[pltpu.VMEM((4, BLOCK, KV_W), BF16),
                        pltpu.VMEM((tm, Q_W), BF16),
                        pltpu.VMEM((tm, GMLP_WIDTH), F32),
                        pltpu.VMEM((4, 2 * BLOCK, 2 * BLOCK), F32)],
        compiler_params=pltpu.CompilerParams(dimension_semantics=("arbitrary",),
                                             vmem_limit_bytes=VMEM_LIMIT_BYTES),
        name="token_mixers",
    )(rel, h, *mod_arrays, bucket, sink_t, ln, bsp, ws_t, w_in, wba, wbg, wo)


def _t5_bucket_table():
    qi = np.arange(BLOCK, dtype=np.int32)[:, None]
    kj = np.arange(2 * BLOCK, dtype=np.int32)[None, :]
    dist = qi + BLOCK - kj
    in_window = (dist >= 0) & (dist < BLOCK)
    dc = np.maximum(dist, 0)
    max_exact = N_BUCKETS // 2
    d_f = np.maximum(dc, max_exact).astype(np.float32)
    large = max_exact + (np.log(d_f / np.float32(max_exact)).astype(np.float32)
                         / np.float32(math.log(MAX_DISTANCE / max_exact))
                         * np.float32(N_BUCKETS - max_exact)).astype(np.int32)
    large = np.minimum(large, N_BUCKETS - 1)
    return np.where(in_window, np.where(dc < max_exact, dc, large), -1).astype(np.int32)


def _pair_rows(a):
    out = []
    for kvh in range(N_KV_HEADS):
        for par in range(2):
            out.append(jnp.concatenate([a[GROUP * kvh + par], a[GROUP * kvh + 2 + par]], axis=0))
    return jnp.stack(out)


def kernel(x, c, rel_bias, w_ada, b_ada, pre_norm_g, post_norm_g, w_ffn1_in, w_ffn1_out, w_in,
           sinks, gmlp_ln_g, gmlp_ln_b, gmlp_w_s, gmlp_b_s, w_br_attn, w_br_gmlp, w_out,
           w_ffn2_in, w_ffn2_out):
    bsz, seq, d = x.shape
    depth = w_ada.shape[0]
    bucket = jnp.asarray(_t5_bucket_table())
    h = x.reshape(bsz * seq, d)
    for l in range(depth):
        mods = (_ada(c, w_ada[l], b_ada[l]), pre_norm_g, post_norm_g)

        sink_t = _pair_rows(jnp.broadcast_to(sinks[l].astype(F32)[:, None, None],
                                             (N_Q_HEADS, BLOCK, LANES)))
        ln = jnp.stack([gmlp_ln_g[l], gmlp_ln_b[l]])
        bsp = jnp.repeat(jnp.transpose(gmlp_b_s[l]), GMLP_WIDTH // GMLP_GROUPS, axis=1)
        ws_t = jnp.concatenate([gmlp_w_s[l, 0::2], gmlp_w_s[l, 1::2]], axis=2).astype(BF16)

        h, later = _ffn(h, mods, l, 0, w_ffn1_in[l], w_ffn1_out[l], seq,
                        side=(w_in[l], w_br_attn[l], w_br_gmlp[l], w_out[l],
                              w_ffn2_in[l], w_ffn2_out[l]))
        h = _mixer(rel_bias.astype(F32), h, mods, l, bucket, sink_t, ln, bsp, ws_t, *later[:4], seq)
        h, _ = _ffn(h, mods, l, 2, later[4], later[5], seq)
    return h.reshape(bsz, seq, d)
```
